```python
import math
import jax, jax.numpy as jnp
from jax import lax
import numpy as np

D_MODEL = 2048
BATCH = 32
SEQ = 256
DEPTH = 2
DEC_BATCH = 4
DEC_SEQ = 1024
PAST_LEN = 256

GRID_W = 64
HEAD_DIM = 128
N_HEADS_WIN = 8
N_KV_WIN = 2
N_HEADS_GLOB = 8
N_KV_GLOB = 2
WINDOW = 128
BLOCK = 128
SSM_WIDTH = 1024
SSM_GROUP_CH = 16
SSM_GROUPS = SSM_WIDTH // SSM_GROUP_CH
SSM_STATE = 64
N_EXPERTS = 16
EXPERT_FF = 2048
EC_FACTOR = 2
ROPE_THETA = 10000.0
WIN_WIDTH = N_HEADS_WIN * HEAD_DIM
KV_WIN_WIDTH = N_KV_WIN * HEAD_DIM
GLOB_WIDTH = N_HEADS_GLOB * HEAD_DIM
KV_GLOB_WIDTH = N_KV_GLOB * HEAD_DIM
IN_COLS = WIN_WIDTH + 2 * KV_WIN_WIDTH + SSM_WIDTH + GLOB_WIDTH + 2 * KV_GLOB_WIDTH + 3 * D_MODEL
DEEPNORM_ALPHA = (2.0 * DEPTH) ** 0.25
DEEPNORM_BETA = (8.0 * DEPTH) ** -0.25
ATTN_SCALE = HEAD_DIM ** -0.5
EPS = 1e-6
NEG_INF = -1e30

kernel_name = 'hybrid_diffusion_prefix_trunk_step'

F32 = jnp.float32


def _layer_norm(x, g=None, b=None):
    xf = x.astype(F32)
    mu = jnp.mean(xf, axis=-1, keepdims=True)
    var = jnp.mean(jnp.square(xf - mu), axis=-1, keepdims=True)
    y = (xf - mu) * lax.rsqrt(var + EPS)
    if g is not None:
        y = y * g.astype(F32) + b.astype(F32)
    return y.astype(x.dtype)


def _rms_norm(x, w):
    xf = x.astype(F32)
    y = xf * lax.rsqrt(jnp.mean(jnp.square(xf), axis=-1, keepdims=True) + EPS) * w.astype(F32)
    return y.astype(x.dtype)


def _rope_1d(x, pos):
    half = x.shape[-1] // 2
    inv = ROPE_THETA ** (-jnp.arange(half, dtype=F32) / half)
    ang = pos.astype(F32)[:, None] * inv[None, :]
    cos = jnp.cos(ang)[:, None, :]
    sin = jnp.sin(ang)[:, None, :]
    x1, x2 = x[..., :half], x[..., half:]
    return jnp.concatenate([x1 * cos - x2 * sin, x2 * cos + x1 * sin], axis=-1)


def _axial_rope(x):
    seq_len = x.shape[1]
    rows = seq_len // GRID_W
    row = jnp.repeat(jnp.arange(rows), GRID_W)
    col = jnp.tile(jnp.arange(GRID_W), rows)
    xf = x.astype(F32)
    half = x.shape[-1] // 2
    out = jnp.concatenate([_rope_1d(xf[..., :half], row), _rope_1d(xf[..., half:], col)], axis=-1)
    return out.astype(x.dtype)


def _attend_blocked(q, k, v, sink):
    b, lq, h, hd = q.shape
    kv = k.shape[2]
    rep = h // kv
    nqb = lq // BLOCK
    qb = q.reshape(b, nqb, BLOCK, kv, rep, hd).transpose(1, 0, 2, 3, 4, 5)

    def one_block(qblk):
        s = jnp.einsum('bqgrd,bkgd->bgrqk', qblk, k).astype(F32) * ATTN_SCALE
        if sink is None:
            p = jax.nn.softmax(s, axis=-1)
        else:
            sk = sink.astype(F32).reshape(1, kv, rep, 1, 1)
            m = jnp.maximum(jnp.max(s, axis=-1, keepdims=True), sk)
            e = jnp.exp(s - m)
            p = e / (jnp.sum(e, axis=-1, keepdims=True) + jnp.exp(sk - m))
        return jnp.einsum('bgrqk,bkgd->bqgrd', p.astype(v.dtype), v)

    o = lax.map(one_block, qb)
    return o.transpose(1, 0, 2, 3, 4, 5).reshape(b, lq, h * hd)


def _window_attn_latent(q, k, v, k_ctx, v_ctx, sink):
    b, seq_len, h, hd = q.shape
    kv = k.shape[2]
    rep = h // kv
    nb = seq_len // BLOCK
    qb = q.reshape(b, nb, BLOCK, kv, rep, hd)

    def bands(t):
        tp = jnp.pad(t, ((0, 0), (BLOCK, BLOCK), (0, 0), (0, 0))).reshape(b, nb + 2, BLOCK, kv, hd)
        return jnp.concatenate([tp[:, :-2], tp[:, 1:-1], tp[:, 2:]], axis=2)

    kw, vw = bands(k), bands(v)
    blk = jnp.arange(nb)[:, None, None]
    qpos = blk * BLOCK + jnp.arange(BLOCK)[None, :, None]
    kpos = (blk - 1) * BLOCK + jnp.arange(3 * BLOCK)[None, None, :]
    valid = (kpos >= 0) & (kpos < seq_len) & (jnp.abs(qpos - kpos) <= WINDOW)
    s_w = jnp.einsum('bnqgrd,bnkgd->bngrqk', qb, kw).astype(F32) * ATTN_SCALE
    s_w = jnp.where(valid[None, :, None, None], s_w, NEG_INF)
    s_c = jnp.einsum('bnqgrd,bkgd->bngrqk', qb, k_ctx).astype(F32) * ATTN_SCALE
    sk = sink.astype(F32).reshape(1, 1, kv, rep, 1, 1)
    m = jnp.maximum(jnp.maximum(jnp.max(s_w, axis=-1, keepdims=True), jnp.max(s_c, axis=-1, keepdims=True)), sk)
    e_w = jnp.exp(s_w - m)
    e_c = jnp.exp(s_c - m)
    denom = jnp.sum(e_w, axis=-1, keepdims=True) + jnp.sum(e_c, axis=-1, keepdims=True) + jnp.exp(sk - m)
    o = (jnp.einsum('bngrqk,bnkgd->bnqgrd', (e_w / denom).astype(v.dtype), vw)
         + jnp.einsum('bngrqk,bkgd->bnqgrd', (e_c / denom).astype(v.dtype), v_ctx))
    return o.reshape(b, seq_len, h * hd)


def _complex_affine_combine(e1, e2):
    ar1, ai1, br1, bi1 = e1
    ar2, ai2, br2, bi2 = e2
    return (ar2 * ar1 - ai2 * ai1,
            ar2 * ai1 + ai2 * ar1,
            ar2 * br1 - ai2 * bi1 + br2,
            ar2 * bi1 + ai2 * br1 + bi2)


def _s5_branch(u, p, init_re, init_im):
    b, seq_len, w = u.shape
    uf = u.astype(F32).reshape(b, seq_len, SSM_GROUPS, SSM_GROUP_CH)
    ys, fin_re, fin_im = [], [], []
    for di in range(2):
        dt = jnp.exp(p['ssm_log_dt'][di].astype(F32))[:, None]
        ar = p['ssm_a_re'][di].astype(F32)
        ai = p['ssm_a_im'][di].astype(F32)
        mag = jnp.exp(dt * ar)
        lam_re = mag * jnp.cos(dt * ai)
        lam_im = mag * jnp.sin(dt * ai)
        den = ar * ar + ai * ai
        coef_re = ((lam_re - 1.0) * ar + lam_im * ai) / den
        coef_im = (lam_im * ar - (lam_re - 1.0) * ai) / den
        br = p['ssm_b_re'][di].astype(F32)
        bi = p['ssm_b_im'][di].astype(F32)
        bbar_re = coef_re[..., None] * br - coef_im[..., None] * bi
        bbar_im = coef_re[..., None] * bi + coef_im[..., None] * br
        seq = uf if di == 0 else jnp.flip(uf, axis=1)
        bu_re = jnp.einsum('blgc,gpc->blgp', seq, bbar_re)
        bu_im = jnp.einsum('blgc,gpc->blgp', seq, bbar_im)
        if init_re is not None:
            s0r = init_re[:, di].astype(F32)
            s0i = init_im[:, di].astype(F32)
            bu_re = bu_re.at[:, 0].add(lam_re * s0r - lam_im * s0i)
            bu_im = bu_im.at[:, 0].add(lam_re * s0i + lam_im * s0r)
        lr = jnp.broadcast_to(lam_re, bu_re.shape)
        li = jnp.broadcast_to(lam_im, bu_im.shape)
        _, _, s_re, s_im = lax.associative_scan(_complex_affine_combine, (lr, li, bu_re, bu_im), axis=1)
        y = (jnp.einsum('blgp,gcp->blgc', s_re, p['ssm_c_re'][di].astype(F32))
             - jnp.einsum('blgp,gcp->blgc', s_im, p['ssm_c_im'][di].astype(F32)))
        if di == 1:
            y = jnp.flip(y, axis=1)
        ys.append(y)
        fin_re.append(s_re[:, -1])
        fin_im.append(s_im[:, -1])
    y = (ys[0] + ys[1]).reshape(b, seq_len, w) + p['ssm_d'].astype(F32) * u.astype(F32)
    z = jax.nn.gelu(y).astype(u.dtype)
    g = z @ p['ssm_w_glu']
    out = g[..., :w] * jax.nn.sigmoid(g[..., w:])
    return out, jnp.stack(fin_re, axis=1), jnp.stack(fin_im, axis=1)


def _split_proj(h, w_in):
    sizes = [WIN_WIDTH, KV_WIN_WIDTH, KV_WIN_WIDTH, SSM_WIDTH, GLOB_WIDTH, KV_GLOB_WIDTH, KV_GLOB_WIDTH]
    points = []
    acc = 0
    for s in sizes:
        acc += s
        points.append(acc)
    return jnp.split(h @ w_in, points, axis=-1)


def _merge(o_win, o_ssm, o_glob, gates, p):
    g_win, g_ssm, g_glob = jnp.split(gates, 3, axis=-1)
    y = (jax.nn.sigmoid(g_win) * (o_win @ p['w_up_win'])
         + jax.nn.sigmoid(g_ssm) * (o_ssm @ p['w_up_ssm'])
         + jax.nn.sigmoid(g_glob) * (o_glob @ p['w_up_glob']))
    return y @ p['w_out']


def _mixers_context(h, p):
    b, seq_len, _ = h.shape
    q_w, k_w, v_w, u_s, q_g, k_g, v_g, gates = _split_proj(h, p['w_in'])
    q_w = q_w.reshape(b, seq_len, N_HEADS_WIN, HEAD_DIM)
    k_w = k_w.reshape(b, seq_len, N_KV_WIN, HEAD_DIM)
    v_w = v_w.reshape(b, seq_len, N_KV_WIN, HEAD_DIM)
    o_win = _attend_blocked(q_w, k_w, v_w, p['win_sink'])
    o_ssm, s_re, s_im = _s5_branch(u_s, p, None, None)
    q_g = _rms_norm(q_g.reshape(b, seq_len, N_HEADS_GLOB, HEAD_DIM), p['q_norm'])
    k_g = _rms_norm(k_g.reshape(b, seq_len, N_KV_GLOB, HEAD_DIM), p['k_norm'])
    v_g = v_g.reshape(b, seq_len, N_KV_GLOB, HEAD_DIM)
    o_glob = _attend_blocked(q_g, k_g, v_g, None)
    return _merge(o_win, o_ssm, o_glob, gates, p), (k_w, v_w, k_g, v_g, s_re, s_im)


def _mixers_latent(h, p, ck_w, cv_w, ck_g, cv_g, init_re, init_im):
    b, seq_len, _ = h.shape
    q_w, k_w, v_w, u_s, q_g, k_g, v_g, gates = _split_proj(h, p['w_in'])
    q_w = _axial_rope(q_w.reshape(b, seq_len, N_HEADS_WIN, HEAD_DIM))
    k_w = _axial_rope(k_w.reshape(b, seq_len, N_KV_WIN, HEAD_DIM))
    v_w = v_w.reshape(b, seq_len, N_KV_WIN, HEAD_DIM)
    o_win = _window_attn_latent(q_w, k_w, v_w, ck_w, cv_w, p['win_sink'])
    o_ssm, _, _ = _s5_branch(u_s, p, init_re, init_im)
    q_g = _axial_rope(_rms_norm(q_g.reshape(b, seq_len, N_HEADS_GLOB, HEAD_DIM), p['q_norm']))
    k_g = _axial_rope(_rms_norm(k_g.reshape(b, seq_len, N_KV_GLOB, HEAD_DIM), p['k_norm']))
    v_g = v_g.reshape(b, seq_len, N_KV_GLOB, HEAD_DIM)
    k_all = jnp.concatenate([k_g, ck_g.astype(k_g.dtype)], axis=1)
    v_all = jnp.concatenate([v_g, cv_g.astype(v_g.dtype)], axis=1)
    o_glob = _attend_blocked(q_g, k_all, v_all, None)
    return _merge(o_win, o_ssm, o_glob, gates, p), ()


def _expert_choice_ffn(x, p):
    b, n, d = x.shape
    cap = EC_FACTOR * n // N_EXPERTS
    aff = jax.nn.softmax((x @ p['router_w']).astype(F32), axis=-1)
    vals, idx = lax.top_k(jnp.transpose(aff, (0, 2, 1)), cap)
    x_sel = jax.vmap(lambda xb, ib: xb[ib])(x, idx)
    hid = (jax.nn.silu(jnp.einsum('becd,edf->becf', x_sel, p['exp_w_gate']))
           * jnp.einsum('becd,edf->becf', x_sel, p['exp_w_up']))
    out = jnp.einsum('becf,efd->becd', hid, p['exp_w_down']) * vals[..., None].astype(x.dtype)
    return jax.vmap(lambda ob, ib: jnp.zeros((n, d), ob.dtype).at[ib.reshape(-1)].add(ob.reshape(-1, d)))(out, idx)


def _modulation(cond, w_mod, b_mod):
    mod = jax.nn.silu(cond) @ w_mod + b_mod
    return jnp.split(mod[:, None, :], 6, axis=-1)


def _trunk_layer(x, cond, mixer_fn, p):
    sh1, sc1, g1, sh2, sc2, g2 = _modulation(cond, p['w_mod'], p['b_mod'])
    h = _layer_norm(x) * (1.0 + sc1) + sh1
    o, extras = mixer_fn(h)
    x = _layer_norm(DEEPNORM_ALPHA * x + g1 * o, p['ln1_g'], p['ln1_b'])
    h2 = _layer_norm(x) * (1.0 + sc2) + sh2
    f = _expert_choice_ffn(h2, p)
    x = _layer_norm(DEEPNORM_ALPHA * x + g2 * f, p['ln2_g'], p['ln2_b'])
    return x, extras


def setup_inputs(seed: int = 0) -> dict:
    key = jax.random.key(seed)
    ks = jax.random.split(key, 40)

    def nrm(i, shape, scale):
        return jax.random.normal(ks[i], shape, F32) * scale

    G, P, GC = SSM_GROUPS, SSM_STATE, SSM_GROUP_CH
    n_idx = jnp.arange(P, dtype=F32)
    return {
        'x_prompt': nrm(0, (BATCH, SEQ, D_MODEL), 1.0),
        'x_sample': nrm(1, (DEC_BATCH, DEC_SEQ, D_MODEL), 1.0),
        'cache_win_k': nrm(2, (DEC_BATCH, DEPTH, PAST_LEN, N_KV_WIN, HEAD_DIM), 1.0),
        'cache_win_v': nrm(3, (DEC_BATCH, DEPTH, PAST_LEN, N_KV_WIN, HEAD_DIM), 1.0),
        'cache_glob_k': nrm(4, (DEC_BATCH, DEPTH, PAST_LEN, N_KV_GLOB, HEAD_DIM), 1.0),
        'cache_glob_v': nrm(5, (DEC_BATCH, DEPTH, PAST_LEN, N_KV_GLOB, HEAD_DIM), 1.0),
        'state_ssm_re': nrm(6, (DEC_BATCH, DEPTH, 2, G, P), 0.5),
        'state_ssm_im': nrm(7, (DEC_BATCH, DEPTH, 2, G, P), 0.5),
        'c': nrm(8, (DEC_BATCH, D_MODEL), 1.0),
        'c_ctx': nrm(9, (D_MODEL,), 1.0),
        'w_mod': nrm(10, (DEPTH, D_MODEL, 6 * D_MODEL), 0.5 * D_MODEL ** -0.5),
        'b_mod': nrm(11, (DEPTH, 6 * D_MODEL), 0.02),
        'w_in': nrm(12, (DEPTH, D_MODEL, IN_COLS), D_MODEL ** -0.5),
        'win_sink': nrm(13, (DEPTH, N_HEADS_WIN), 0.5),
        'ssm_a_re': -0.5 + nrm(14, (DEPTH, 2, G, P), 0.01),
        'ssm_a_im': math.pi * n_idx + nrm(15, (DEPTH, 2, G, P), 0.01),
        'ssm_log_dt': jax.random.uniform(ks[16], (DEPTH, 2, G), F32, math.log(1e-3), math.log(1e-1)),
        'ssm_b_re': nrm(17, (DEPTH, 2, G, P, GC), (2.0 * GC) ** -0.5),
        'ssm_b_im': nrm(18, (DEPTH, 2, G, P, GC), (2.0 * GC) ** -0.5),
        'ssm_c_re': nrm(19, (DEPTH, 2, G, GC, P), (2.0 * P) ** -0.5),
        'ssm_c_im': nrm(20, (DEPTH, 2, G, GC, P), (2.0 * P) ** -0.5),
        'ssm_d': nrm(21, (DEPTH, SSM_WIDTH), 0.5),
        'ssm_w_glu': nrm(22, (DEPTH, SSM_WIDTH, 2 * SSM_WIDTH), SSM_WIDTH ** -0.5),
        'q_norm': 1.0 + nrm(23, (DEPTH, HEAD_DIM), 0.02),
        'k_norm': 1.0 + nrm(24, (DEPTH, HEAD_DIM), 0.02),
        'w_up_win': nrm(25, (DEPTH, WIN_WIDTH, D_MODEL), DEEPNORM_BETA * WIN_WIDTH ** -0.5),
        'w_up_ssm': nrm(26, (DEPTH, SSM_WIDTH, D_MODEL), DEEPNORM_BETA * SSM_WIDTH ** -0.5),
        'w_up_glob': nrm(27, (DEPTH, GLOB_WIDTH, D_MODEL), DEEPNORM_BETA * GLOB_WIDTH ** -0.5),
        'w_out': nrm(28, (DEPTH, D_MODEL, D_MODEL), DEEPNORM_BETA * D_MODEL ** -0.5),
        'ln1_g': 1.0 + nrm(29, (DEPTH, D_MODEL), 0.02),
        'ln1_b': nrm(30, (DEPTH, D_MODEL), 0.02),
        'ln2_g': 1.0 + nrm(31, (DEPTH, D_MODEL), 0.02),
        'ln2_b': nrm(32, (DEPTH, D_MODEL), 0.02),
        'router_w': nrm(33, (DEPTH, D_MODEL, N_EXPERTS), D_MODEL ** -0.5),
        'exp_w_gate': nrm(34, (DEPTH, N_EXPERTS, D_MODEL, EXPERT_FF), D_MODEL ** -0.5),
        'exp_w_up': nrm(35, (DEPTH, N_EXPERTS, D_MODEL, EXPERT_FF), D_MODEL ** -0.5),
        'exp_w_down': nrm(36, (DEPTH, N_EXPERTS, EXPERT_FF, D_MODEL), DEEPNORM_BETA * EXPERT_FF ** -0.5),
    }


def reference(x_prompt, x_sample, cache_win_k, cache_win_v, cache_glob_k, cache_glob_v,
              state_ssm_re, state_ssm_im, c, c_ctx, w_mod, b_mod, w_in, win_sink,
              ssm_a_re, ssm_a_im, ssm_log_dt, ssm_b_re, ssm_b_im, ssm_c_re, ssm_c_im,
              ssm_d, ssm_w_glu, q_norm, k_norm, w_up_win, w_up_ssm, w_up_glob, w_out,
              ln1_g, ln1_b, ln2_g, ln2_b, router_w, exp_w_gate, exp_w_up, exp_w_down):
    xp = x_prompt
    xs = x_sample
    new_wk, new_wv, new_gk, new_gv, new_sre, new_sim = [], [], [], [], [], []
    for l in range(DEPTH):
        p = dict(w_mod=w_mod[l], b_mod=b_mod[l], w_in=w_in[l], win_sink=win_sink[l],
                 ssm_a_re=ssm_a_re[l], ssm_a_im=ssm_a_im[l], ssm_log_dt=ssm_log_dt[l],
                 ssm_b_re=ssm_b_re[l], ssm_b_im=ssm_b_im[l], ssm_c_re=ssm_c_re[l], ssm_c_im=ssm_c_im[l],
                 ssm_d=ssm_d[l], ssm_w_glu=ssm_w_glu[l], q_norm=q_norm[l], k_norm=k_norm[l],
                 w_up_win=w_up_win[l], w_up_ssm=w_up_ssm[l], w_up_glob=w_up_glob[l], w_out=w_out[l],
                 ln1_g=ln1_g[l], ln1_b=ln1_b[l], ln2_g=ln2_g[l], ln2_b=ln2_b[l], router_w=router_w[l],
                 exp_w_gate=exp_w_gate[l], exp_w_up=exp_w_up[l], exp_w_down=exp_w_down[l])
        xp, (kw, vw, kg, vg, sre, sim) = _trunk_layer(
            xp, c_ctx[None, :], lambda h, p=p: _mixers_context(h, p), p)
        new_wk.append(kw)
        new_wv.append(vw)
        new_gk.append(kg)
        new_gv.append(vg)
        new_sre.append(sre)
        new_sim.append(sim)
        xs, _ = _trunk_layer(
            xs, c,
            lambda h, p=p, l=l: _mixers_latent(h, p, cache_win_k[:, l], cache_win_v[:, l],
                                               cache_glob_k[:, l], cache_glob_v[:, l],
                                               state_ssm_re[:, l], state_ssm_im[:, l]),
            p)
    new_win_k = jnp.stack(new_wk, axis=1)
    new_win_v = jnp.stack(new_wv, axis=1)
    new_glob_k = jnp.stack(new_gk, axis=1)
    new_glob_v = jnp.stack(new_gv, axis=1)
    new_ssm_re = jnp.stack(new_sre, axis=1)
    new_ssm_im = jnp.stack(new_sim, axis=1)
    return (xp, xs, new_win_k, new_win_v, new_glob_k, new_glob_v, new_ssm_re, new_ssm_im)
```

```python
import functools
import math

import numpy as np
import jax
import jax.numpy as jnp
from jax import lax
from jax.experimental import pallas as pl
from jax.experimental.pallas import tpu as pltpu

F32 = jnp.float32
BF16 = jnp.bfloat16
HIGHEST = lax.Precision.HIGHEST

HEAD_DIM = 128
LANES = 128
GRID_W = 64
WINDOW = 128
EC_FACTOR = 2
ROPE_THETA = 10000.0
EPS = 1e-6
NEG_INF = -1e30
ATTN_SCALE = HEAD_DIM ** -0.5
S5_CHUNK = 16
S5_OCT = 8
ROW_BLOCK = 256
FFN_ROW_CHUNK = 512
VMEM_LIMIT = 60 * 1024 * 1024


def _cparams(n_axes):
    return pltpu.CompilerParams(dimension_semantics=("arbitrary",) * n_axes,
                                vmem_limit_bytes=VMEM_LIMIT)


def _tile(dim, pref):
    return pref if dim % pref == 0 else dim


def _ln(x):
    mu = jnp.mean(x, axis=-1, keepdims=True)
    xc = x - mu
    var = jnp.mean(xc * xc, axis=-1, keepdims=True)
    return xc * lax.rsqrt(var + EPS)


def _mm_kernel(x_ref, w_ref, *rest, act, has_bias):
    if has_bias:
        b_ref, o_ref, wbf_ref = rest
    else:
        o_ref, wbf_ref = rest

    @pl.when(pl.program_id(1) == 0)
    def _():
        wbf_ref[...] = w_ref[...].astype(BF16)

    x = x_ref[...]
    if act == "silu":
        x = jax.nn.silu(x.astype(F32))
    acc = jnp.dot(x.astype(BF16), wbf_ref[...], preferred_element_type=F32)
    if has_bias:
        acc = acc + b_ref[...]
    o_ref[...] = acc.astype(o_ref.dtype)


def _matmul(x, w, col0, ncols, *, bias=None, act=None, out_dtype=F32, tm=1024, tn=512):
    m, k = x.shape
    tm = _tile(m, tm)
    tn = next(t for t in (tn, 512, 256, LANES) if ncols % t == 0 and col0 % t == 0)
    c0 = col0 // tn
    in_specs = [pl.BlockSpec((tm, k), lambda n, i: (i, 0)),
                pl.BlockSpec((k, tn), lambda n, i: (0, c0 + n))]
    args = [x, w]
    if bias is not None:
        in_specs.append(pl.BlockSpec((1, tn), lambda n, i: (0, c0 + n)))
        args.append(bias)
    return pl.pallas_call(
        functools.partial(_mm_kernel, act=act, has_bias=bias is not None),
        grid=(ncols // tn, m // tm),
        in_specs=in_specs,
        out_specs=pl.BlockSpec((tm, tn), lambda n, i: (i, n)),
        out_shape=jax.ShapeDtypeStruct((m, ncols), out_dtype),
        scratch_shapes=[pltpu.VMEM((k, tn), BF16)],
        compiler_params=_cparams(2),
    )(*args)


def _ln_mod_kernel(x_ref, sc_ref, sh_ref, h_ref):
    h_ref[...] = (_ln(x_ref[...]) * (1.0 + sc_ref[...]) + sh_ref[...]).astype(h_ref.dtype)


def _mod_spec(d, chunk, mod_row):
    return pl.BlockSpec((None, None, 1, d), lambda i: (mod_row(i), chunk, 0, 0))


def _ln_mod(x, mod4, mod_row):
    t, d = x.shape
    return pl.pallas_call(
        _ln_mod_kernel,
        grid=(t // ROW_BLOCK,),
        in_specs=[pl.BlockSpec((ROW_BLOCK, d), lambda i: (i, 0)),
                  _mod_spec(d, 1, mod_row), _mod_spec(d, 0, mod_row)],
        out_specs=pl.BlockSpec((ROW_BLOCK, d), lambda i: (i, 0)),
        out_shape=jax.ShapeDtypeStruct((t, d), BF16),
        compiler_params=_cparams(1),
    )(x, mod4, mod4)


def _rope_tables(seq_len):
    half = HEAD_DIM // 4
    inv = ROPE_THETA ** (-np.arange(half, dtype=np.float64) / half)
    tok = np.arange(seq_len)
    ang_r = (tok // GRID_W)[:, None] * inv[None, :]
    ang_c = (tok % GRID_W)[:, None] * inv[None, :]
    cos = np.concatenate([np.cos(ang_r), np.cos(ang_r), np.cos(ang_c), np.cos(ang_c)], axis=-1)
    sin = np.concatenate([-np.sin(ang_r), np.sin(ang_r), -np.sin(ang_c), np.sin(ang_c)], axis=-1)
    return jnp.asarray(cos, F32), jnp.asarray(sin, F32)


def _rope(x, cos, sin_signed):
    lane = lax.broadcasted_iota(jnp.int32, x.shape, 1)
    swapped = jnp.where((lane % 64) < 32, pltpu.roll(x, 96, 1), pltpu.roll(x, 32, 1))
    return x * cos + swapped * sin_signed


def _rms(x, w):
    return x * lax.rsqrt(jnp.mean(x * x, axis=-1, keepdims=True) + EPS) * w


def _attn_kernel(*refs, heads, kv, tq, rope, norm, sink, ctx, band, emit_k):
    it = iter(refs)
    q_ref, k_ref, v_ref = next(it), next(it), next(it)
    kc_ref = vc_ref = sink_ref = qn_ref = kn_ref = cq_ref = sq_ref = ck_ref = sk_ref = None
    if ctx:
        kc_ref, vc_ref = next(it), next(it)
    if sink:
        sink_ref = next(it)
    if norm:
        qn_ref, kn_ref = next(it), next(it)
    if rope:
        cq_ref, sq_ref, ck_ref, sk_ref = next(it), next(it), next(it), next(it)
    o_ref = next(it)
    ko_ref = next(it) if emit_k else None
    kp_ref, vp_ref = next(it), next(it)

    qi = pl.program_id(1)
    rep = heads // kv

    @pl.when(qi == 0)
    def _():
        for g in range(kv):
            sl = slice(g * HEAD_DIM, (g + 1) * HEAD_DIM)
            kg = k_ref[:, sl]
            if norm:
                kg = _rms(kg, kn_ref[...])
            if rope:
                kg = _rope(kg, ck_ref[...], sk_ref[...])
            if emit_k:
                ko_ref[:, sl] = kg
            kp_ref[:, sl] = kg.astype(BF16)
        vp_ref[...] = v_ref[...].astype(BF16)

    nt = (((1,), (1,)), ((), ()))
    for h in range(heads):
        g = h // rep
        gs = slice(g * HEAD_DIM, (g + 1) * HEAD_DIM)
        qh = q_ref[:, h * HEAD_DIM:(h + 1) * HEAD_DIM]
        if norm:
            qh = _rms(qh, qn_ref[...])
        if rope:
            qh = _rope(qh, cq_ref[...], sq_ref[...])
        qh = qh.astype(BF16)
        s = lax.dot_general(qh, kp_ref[:, gs], nt, preferred_element_type=F32) * ATTN_SCALE
        if band:
            qpos = qi * tq + lax.broadcasted_iota(jnp.int32, s.shape, 0)
            kpos = lax.broadcasted_iota(jnp.int32, s.shape, 1)
            s = jnp.where(jnp.abs(qpos - kpos) <= WINDOW, s, NEG_INF)
        m = jnp.max(s, axis=-1, keepdims=True)
        if ctx:
            sc = lax.dot_general(qh, kc_ref[:, gs].astype(BF16), nt,
                                 preferred_element_type=F32) * ATTN_SCALE
            m = jnp.maximum(m, jnp.max(sc, axis=-1, keepdims=True))
        if sink:
            m = jnp.maximum(m, sink_ref[h])
        e = jnp.exp(s - m)
        den = jnp.sum(e, axis=-1, keepdims=True)
        o = jnp.dot(e.astype(BF16), vp_ref[:, gs], preferred_element_type=F32)
        if ctx:
            ec = jnp.exp(sc - m)
            den = den + jnp.sum(ec, axis=-1, keepdims=True)
            o = o + jnp.dot(ec.astype(BF16), vc_ref[:, gs].astype(BF16), preferred_element_type=F32)
        if sink:
            den = den + jnp.exp(sink_ref[h] - m)
        o_ref[:, h * HEAD_DIM:(h + 1) * HEAD_DIM] = (o / den).astype(o_ref.dtype)


def _attention(q, k, v, *, batch, seq, k_ctx=None, v_ctx=None, sink=None, q_norm=None, k_norm=None,
               rope=False, band=False, emit_k=False):
    t, qw = q.shape
    kw = k.shape[1]
    heads, kv = qw // HEAD_DIM, kw // HEAD_DIM
    tq = _tile(seq, 256)
    nq = seq // tq
    ctx, has_sink, norm = k_ctx is not None, sink is not None, q_norm is not None
    in_specs = [pl.BlockSpec((tq, qw), lambda b, i: (b * nq + i, 0)),
                pl.BlockSpec((seq, kw), lambda b, i: (b, 0)),
                pl.BlockSpec((seq, kw), lambda b, i: (b, 0))]
    args = [q, k, v]
    if ctx:
        lc = k_ctx.shape[1]
        in_specs += [pl.BlockSpec((None, lc, kw), lambda b, i: (b, 0, 0))] * 2
        args += [k_ctx, v_ctx]
    if has_sink:
        in_specs.append(pl.BlockSpec(memory_space=pltpu.SMEM))
        args.append(sink)
    if norm:
        in_specs += [pl.BlockSpec((1, HEAD_DIM), lambda b, i: (0, 0))] * 2
        args += [q_norm.reshape(1, HEAD_DIM), k_norm.reshape(1, HEAD_DIM)]
    if rope:
        cos, sin = _rope_tables(seq)
        in_specs += [pl.BlockSpec((tq, HEAD_DIM), lambda b, i: (i, 0))] * 2
        in_specs += [pl.BlockSpec((seq, HEAD_DIM), lambda b, i: (0, 0))] * 2
        args += [cos, sin, cos, sin]
    out_specs = [pl.BlockSpec((tq, qw), lambda b, i: (b * nq + i, 0))]
    out_shape = [jax.ShapeDtypeStruct((t, qw), BF16)]
    if emit_k:
        out_specs.append(pl.BlockSpec((seq, kw), lambda b, i: (b, 0)))
        out_shape.append(jax.ShapeDtypeStruct((t, kw), F32))
    res = pl.pallas_call(
        functools.partial(_attn_kernel, heads=heads, kv=kv, tq=tq, rope=rope, norm=norm,
                          sink=has_sink, ctx=ctx, band=band, emit_k=emit_k),
        grid=(batch, nq),
        in_specs=in_specs,
        out_specs=out_specs,
        out_shape=out_shape,
        scratch_shapes=[pltpu.VMEM((seq, kw), BF16), pltpu.VMEM((seq, kw), BF16)],
        compiler_params=_cparams(2),
    )(*args)
    return res if emit_k else res[0]


def _s5_params(p):
    tc = S5_CHUNK
    dt = jnp.exp(p["ssm_log_dt"].astype(F32))[..., None]
    ar, ai = p["ssm_a_re"].astype(F32), p["ssm_a_im"].astype(F32)
    g_n, p_n = ar.shape[1], ar.shape[2]
    mag = jnp.exp(dt * ar)
    lam_re, lam_im = mag * jnp.cos(dt * ai), mag * jnp.sin(dt * ai)
    den = ar * ar + ai * ai
    coef_re = ((lam_re - 1.0) * ar + lam_im * ai) / den
    coef_im = (lam_im * ar - (lam_re - 1.0) * ai) / den
    br, bi = p["ssm_b_re"].astype(F32), p["ssm_b_im"].astype(F32)
    bb_re = coef_re[..., None] * br - coef_im[..., None] * bi
    bb_im = coef_re[..., None] * bi + coef_im[..., None] * br
    cr, ci = p["ssm_c_re"].astype(F32), p["ssm_c_im"].astype(F32)
    ks = jnp.arange(tc + 1, dtype=F32)[:, None, None, None]
    pw_mag = jnp.exp(ks * (dt * ar)[None])
    pw_re, pw_im = pw_mag * jnp.cos(ks * (dt * ai)[None]), pw_mag * jnp.sin(ks * (dt * ai)[None])

    lb_re = pw_re[..., None] * bb_re[None] - pw_im[..., None] * bb_im[None]
    lb_im = pw_re[..., None] * bb_im[None] + pw_im[..., None] * bb_re[None]
    kk = (jnp.einsum("dgcp,kdgpe->kdgce", cr, lb_re[:tc], precision=HIGHEST)
          - jnp.einsum("dgcp,kdgpe->kdgce", ci, lb_im[:tc], precision=HIGHEST))
    j = jnp.arange(tc)[:, None]
    t = jnp.arange(tc)[None, :]
    kf = jnp.where((t >= j)[:, :, None, None, None], kk[jnp.clip(t - j, 0, tc - 1), 0], 0.0)
    kb = jnp.where((j >= t)[:, :, None, None, None], kk[jnp.clip(j - t, 0, tc - 1), 1], 0.0)
    m_g = jnp.transpose(kf + kb, (2, 0, 4, 1, 3))
    no = g_n // S5_OCT
    gc = br.shape[-1]
    eye = jnp.eye(S5_OCT, dtype=F32)

    def octets(x):
        return x.reshape((no, S5_OCT) + x.shape[1:])

    m_o = jnp.einsum("oajetc,ab->ojaetbc", octets(m_g), eye)
    m_o = m_o.reshape(no, tc * S5_OCT * gc, tc * S5_OCT * gc)

    wf_re, wf_im = lb_re[:tc][::-1, 0], lb_im[:tc][::-1, 0]
    wb_re, wb_im = lb_re[:tc, 1], lb_im[:tc, 1]
    w_g = jnp.stack([wf_re, wf_im, wb_re, wb_im], axis=0)
    w_g = jnp.transpose(w_g, (2, 1, 4, 0, 3))
    w_o = jnp.einsum("oajesp,ab->ojaesbp", octets(w_g), eye)
    w_o = w_o.reshape(no, tc * S5_OCT * gc, 4 * S5_OCT * p_n)

    def c_lam(d, pows):
        pr, pi = pw_re[pows, d], pw_im[pows, d]
        z_re = cr[d][None] * pr[:, :, None, :] - ci[d][None] * pi[:, :, None, :]
        z_im = cr[d][None] * pi[:, :, None, :] + ci[d][None] * pr[:, :, None, :]
        return z_re, -z_im
    zf_re, zf_im = c_lam(0, jnp.arange(1, tc + 1))
    zb_re, zb_im = c_lam(1, jnp.arange(tc, 0, -1))
    z_g = jnp.stack([zf_re, zf_im, zb_re, zb_im], axis=0)
    z_g = jnp.transpose(z_g, (2, 0, 4, 1, 3))
    z_o = jnp.einsum("oasptc,ab->osaptbc", octets(z_g), eye)
    z_o = z_o.reshape(no, 4 * S5_OCT * p_n, tc * S5_OCT * gc)

    lam_t = jnp.stack([pw_re[tc, 0], pw_im[tc, 0], pw_re[tc, 1], pw_im[tc, 1]], axis=0)
    lam_o = jnp.transpose(lam_t.reshape(4, no, S5_OCT * p_n), (1, 0, 2))
    return m_o.astype(BF16), w_o.astype(BF16), z_o.astype(BF16), lam_o


def _s5_state_kernel(u_ref, w_ref, lam_ref, s0_ref, sp_ref, fin_ref, ubf_ref, vs_ref, st_ref,
                     *, batch, nchunk):
    tc = S5_CHUNK
    for t in range(tc):
        ubf_ref[:, t * LANES:(t + 1) * LANES] = u_ref[:, t, :].astype(BF16)
    q = st_ref.shape[-1] // 4
    vs_ref[...] = jnp.dot(ubf_ref[...], w_ref[...],
                          preferred_element_type=F32).reshape(batch, nchunk, 4 * q)
    fa, fb = lam_ref[0:1, :], lam_ref[1:2, :]
    ba, bb = lam_ref[2:3, :], lam_ref[3:4, :]
    s_re, s_im = s0_ref[:, 0:q], s0_ref[:, q:2 * q]
    for c in range(nchunk):
        st_ref[:, c, 0:q] = s_re
        st_ref[:, c, q:2 * q] = s_im
        v_re, v_im = vs_ref[:, c, 0:q], vs_ref[:, c, q:2 * q]
        s_re, s_im = fa * s_re - fb * s_im + v_re, fa * s_im + fb * s_re + v_im
    fin_ref[:, 0:q] = s_re
    fin_ref[:, q:2 * q] = s_im
    s_re, s_im = s0_ref[:, 2 * q:3 * q], s0_ref[:, 3 * q:4 * q]
    for c in range(nchunk - 1, -1, -1):
        st_ref[:, c, 2 * q:3 * q] = s_re
        st_ref[:, c, 3 * q:4 * q] = s_im
        v_re, v_im = vs_ref[:, c, 2 * q:3 * q], vs_ref[:, c, 3 * q:4 * q]
        s_re, s_im = ba * s_re - bb * s_im + v_re, ba * s_im + bb * s_re + v_im
    fin_ref[:, 2 * q:3 * q] = s_re
    fin_ref[:, 3 * q:4 * q] = s_im
    sp_ref[...] = st_ref[...].reshape(batch * nchunk, 4 * q).astype(sp_ref.dtype)


def _s5_out_kernel(u_ref, sp_ref, m_ref, z_ref, y_ref, ubf_ref):
    tc = S5_CHUNK
    for t in range(tc):
        ubf_ref[:, t * LANES:(t + 1) * LANES] = u_ref[:, t, :].astype(BF16)
    y = (jnp.dot(ubf_ref[...], m_ref[...], preferred_element_type=F32)
         + jnp.dot(sp_ref[...], z_ref[...], preferred_element_type=F32))
    for t in range(tc):
        y_ref[:, t, :] = y[:, t * LANES:(t + 1) * LANES]


def _s5_mix(u, m_o, w_o, z_o, lam_o, s0, *, batch, seq):
    t, width = u.shape
    tc = S5_CHUNK
    no = width // LANES
    r = t // tc
    nchunk = seq // tc
    sw = w_o.shape[-1]
    kw = tc * LANES
    u3 = u.reshape(r, tc, width)
    sp, fin = pl.pallas_call(
        functools.partial(_s5_state_kernel, batch=batch, nchunk=nchunk),
        grid=(no,),
        in_specs=[pl.BlockSpec((r, tc, LANES), lambda o: (0, 0, o)),
                  pl.BlockSpec((None, kw, sw), lambda o: (o, 0, 0)),
                  pl.BlockSpec((None, 4, sw // 4), lambda o: (o, 0, 0)),
                  pl.BlockSpec((None, batch, sw), lambda o: (o, 0, 0))],
        out_specs=[pl.BlockSpec((None, r, sw), lambda o: (o, 0, 0)),
                   pl.BlockSpec((None, batch, sw), lambda o: (o, 0, 0))],
        out_shape=[jax.ShapeDtypeStruct((no, r, sw), BF16),
                   jax.ShapeDtypeStruct((no, batch, sw), F32)],
        scratch_shapes=[pltpu.VMEM((r, kw), BF16),
                        pltpu.VMEM((batch, nchunk, sw), F32),
                        pltpu.VMEM((batch, nchunk, sw), F32)],
        compiler_params=_cparams(1),
    )(u3, w_o, lam_o, s0)
    tr = _tile(r, 256)
    y3 = pl.pallas_call(
        _s5_out_kernel,
        grid=(no, r // tr),
        in_specs=[pl.BlockSpec((tr, tc, LANES), lambda o, i: (i, 0, o)),
                  pl.BlockSpec((None, tr, sw), lambda o, i: (o, i, 0)),
                  pl.BlockSpec((None, kw, kw), lambda o, i: (o, 0, 0)),
                  pl.BlockSpec((None, sw, kw), lambda o, i: (o, 0, 0))],
        out_specs=pl.BlockSpec((tr, tc, LANES), lambda o, i: (i, 0, o)),
        out_shape=jax.ShapeDtypeStruct((r, tc, width), F32),
        scratch_shapes=[pltpu.VMEM((tr, kw), BF16)],
        compiler_params=_cparams(2),
    )(u3, sp, m_o, z_o)
    return y3.reshape(t, width), fin


def _state_to_lanes(s_re, s_im):
    b, _, g_n, p_n = s_re.shape
    no = g_n // S5_OCT
    parts = jnp.stack([s_re[:, 0], s_im[:, 0], s_re[:, 1], s_im[:, 1]], axis=1)
    parts = parts.reshape(b, 4, no, S5_OCT * p_n)
    return jnp.transpose(parts, (2, 0, 1, 3)).reshape(no, b, 4 * S5_OCT * p_n).astype(F32)


def _lanes_to_state(fin, p_n):
    no, b, _ = fin.shape
    parts = jnp.transpose(fin.reshape(no, b, 4, S5_OCT, p_n), (1, 2, 0, 3, 4)).reshape(b, 4, no * S5_OCT, p_n)
    return jnp.stack([parts[:, 0], parts[:, 2]], axis=1), jnp.stack([parts[:, 1], parts[:, 3]], axis=1)


def _glu_kernel(y_ref, u_ref, d_ref, w_ref, o_ref):
    z = jax.nn.gelu(y_ref[...] + d_ref[...] * u_ref[...]).astype(BF16)
    g = jnp.dot(z, w_ref[...], preferred_element_type=F32)
    w = o_ref.shape[-1]
    o_ref[...] = (g[:, :w] * jax.nn.sigmoid(g[:, w:])).astype(o_ref.dtype)


def _glu(y, u, d, w_glu):
    t, width = y.shape
    tm = _tile(t, 512)
    return pl.pallas_call(
        _glu_kernel,
        grid=(t // tm,),
        in_specs=[pl.BlockSpec((tm, width), lambda i: (i, 0)),
                  pl.BlockSpec((tm, width), lambda i: (i, 0)),
                  pl.BlockSpec((1, width), lambda i: (0, 0)),
                  pl.BlockSpec((width, 2 * width), lambda i: (0, 0))],
        out_specs=pl.BlockSpec((tm, width), lambda i: (i, 0)),
        out_shape=jax.ShapeDtypeStruct((t, width), BF16),
        compiler_params=_cparams(1),
    )(y, u, d, w_glu)


def _merge_kernel(ow_ref, os_ref, og_ref, gw_ref, gs_ref, gg_ref, ww_ref, ws_ref, wg_ref, y_ref):
    y = (jax.nn.sigmoid(gw_ref[...]) * jnp.dot(ow_ref[...], ww_ref[...], preferred_element_type=F32)
         + jax.nn.sigmoid(gs_ref[...]) * jnp.dot(os_ref[...], ws_ref[...], preferred_element_type=F32)
         + jax.nn.sigmoid(gg_ref[...]) * jnp.dot(og_ref[...], wg_ref[...], preferred_element_type=F32))
    y_ref[...] = y.astype(y_ref.dtype)


def _merge(o_win, o_ssm, o_glob, gates, w_win, w_ssm, w_glob):
    t = o_win.shape[0]
    d = w_win.shape[1]
    tm = _tile(t, 256)

    def rows(w):
        return pl.BlockSpec((tm, w), lambda i: (i, 0))

    def whole(a):
        return pl.BlockSpec(a.shape, lambda i: (0, 0))

    return pl.pallas_call(
        _merge_kernel,
        grid=(t // tm,),
        in_specs=[rows(o_win.shape[1]), rows(o_ssm.shape[1]), rows(o_glob.shape[1]),
                  pl.BlockSpec((tm, d), lambda i: (i, 0)),
                  pl.BlockSpec((tm, d), lambda i: (i, 1)),
                  pl.BlockSpec((tm, d), lambda i: (i, 2)),
                  whole(w_win), whole(w_ssm), whole(w_glob)],
        out_specs=rows(d),
        out_shape=jax.ShapeDtypeStruct((t, d), BF16),
        compiler_params=_cparams(1),
    )(o_win, o_ssm, o_glob, gates, gates, gates, w_win, w_ssm, w_glob)


def _outproj_kernel(y_ref, x_ref, w_ref, g1_ref, sc2_ref, sh2_ref, lg_ref, lb_ref, rw_ref,
                    x1_ref, h2_ref, logit_ref, *, alpha):
    o = jnp.dot(y_ref[...], w_ref[...], preferred_element_type=F32)
    x1 = _ln(alpha * x_ref[...] + g1_ref[...] * o) * lg_ref[...] + lb_ref[...]
    x1_ref[...] = x1
    h2 = _ln(x1) * (1.0 + sc2_ref[...]) + sh2_ref[...]
    h2_ref[...] = h2.astype(h2_ref.dtype)
    logit_ref[...] = jnp.dot(h2, rw_ref[...], preferred_element_type=F32, precision=HIGHEST)


def _outproj(y, x, w_out, mod4, mod_row, ln_g, ln_b, router_pad, alpha):
    t, d = x.shape
    tm = ROW_BLOCK

    def rows(w):
        return pl.BlockSpec((tm, w), lambda i: (i, 0))

    def vec():
        return pl.BlockSpec((1, d), lambda i: (0, 0))

    return pl.pallas_call(
        functools.partial(_outproj_kernel, alpha=alpha),
        grid=(t // tm,),
        in_specs=[rows(d), rows(d), pl.BlockSpec((d, d), lambda i: (0, 0)),
                  _mod_spec(d, 2, mod_row), _mod_spec(d, 4, mod_row), _mod_spec(d, 3, mod_row),
                  vec(), vec(), pl.BlockSpec((d, LANES), lambda i: (0, 0))],
        out_specs=[rows(d), rows(d), rows(LANES)],
        out_shape=[jax.ShapeDtypeStruct((t, d), F32), jax.ShapeDtypeStruct((t, d), BF16),
                   jax.ShapeDtypeStruct((t, LANES), F32)],
        compiler_params=_cparams(1),
    )(y, x, w_out, mod4, mod4, mod4, ln_g, ln_b, router_pad)


def _route_kernel(logit_ref, h_ref, xsel_ref, vals_ref, rank_ref, *, n_exp, cap):
    n = logit_ref.shape[0]
    lg = logit_ref[...]
    col = lax.broadcasted_iota(jnp.int32, lg.shape, 1)
    lg = jnp.where(col < n_exp, lg, -jnp.inf)
    ex = jnp.exp(lg - jnp.max(lg, axis=-1, keepdims=True))
    aff = ex / jnp.sum(ex, axis=-1, keepdims=True)
    aff_t = aff.T
    jj = lax.broadcasted_iota(jnp.int32, (n, n), 0)
    ii = lax.broadcasted_iota(jnp.int32, (n, n), 1)
    slot = lax.broadcasted_iota(jnp.int32, (cap, n), 0)
    h = h_ref[...]
    for e in range(n_exp):
        a_row = aff_t[e:e + 1, :]
        a_col = aff[:, e:e + 1]
        beats = (a_col > a_row) | ((a_col == a_row) & (jj < ii))
        rank = jnp.sum(beats.astype(jnp.int32), axis=0, keepdims=True)
        pick = slot == rank
        xsel_ref[e] = jnp.dot(pick.astype(BF16), h, preferred_element_type=F32).astype(xsel_ref.dtype)
        vals_ref[e] = jnp.sum(jnp.where(pick, a_row, 0.0), axis=1, keepdims=True)
        rank_ref[e:e + 1, :] = rank


def _route(logits, h2, *, batch, seq, n_exp):
    t, d = h2.shape
    cap = EC_FACTOR * seq // n_exp
    return pl.pallas_call(
        functools.partial(_route_kernel, n_exp=n_exp, cap=cap),
        grid=(batch,),
        in_specs=[pl.BlockSpec((seq, LANES), lambda b: (b, 0)),
                  pl.BlockSpec((seq, d), lambda b: (b, 0))],
        out_specs=[pl.BlockSpec((n_exp, cap, d), lambda b: (0, b, 0)),
                   pl.BlockSpec((n_exp, cap, 1), lambda b: (0, b, 0)),
                   pl.BlockSpec((None, n_exp, seq), lambda b: (b, 0, 0))],
        out_shape=[jax.ShapeDtypeStruct((n_exp, batch * cap, d), BF16),
                   jax.ShapeDtypeStruct((n_exp, batch * cap, 1), F32),
                   jax.ShapeDtypeStruct((batch, n_exp, seq), jnp.int32)],
        compiler_params=_cparams(1),
    )(logits, h2)


def _ffn_kernel(xp_ref, xs_ref, vp_ref, vs_ref, wg_ref, wu_ref, wd_ref, op_ref, os_ref, ap_ref, as_ref):
    f = pl.program_id(1)
    wg = wg_ref[...].astype(BF16)
    wu = wu_ref[...].astype(BF16)
    wd = wd_ref[...].astype(BF16)
    for x_ref, acc_ref, v_ref, o_ref in ((xp_ref, ap_ref, vp_ref, op_ref), (xs_ref, as_ref, vs_ref, os_ref)):
        rows = x_ref.shape[0]
        rc = _tile(rows, FFN_ROW_CHUNK)
        for r0 in range(0, rows, rc):
            rs = slice(r0, r0 + rc)
            x = x_ref[rs, :]
            hid = (jax.nn.silu(jnp.dot(x, wg, preferred_element_type=F32))
                   * jnp.dot(x, wu, preferred_element_type=F32)).astype(BF16)
            part = jnp.dot(hid, wd, preferred_element_type=F32)

            @pl.when(f == 0)
            def _():
                acc_ref[rs, :] = part

            @pl.when(f > 0)
            def _():
                acc_ref[rs, :] += part

            @pl.when(f == pl.num_programs(1) - 1)
            def _():
                o_ref[rs, :] = (acc_ref[rs, :] * v_ref[rs, :]).astype(o_ref.dtype)


def _ffn(xsel_p, xsel_s, vals_p, vals_s, w_gate, w_up, w_down):
    n_exp, rp, d = xsel_p.shape
    rs = xsel_s.shape[1]
    ff = w_gate.shape[-1]
    tf = _tile(ff, 256)

    def per_expert(r, w):
        return pl.BlockSpec((None, r, w), lambda e, f: (e, 0, 0))

    return pl.pallas_call(
        _ffn_kernel,
        grid=(n_exp, ff // tf),
        in_specs=[per_expert(rp, d), per_expert(rs, d), per_expert(rp, 1), per_expert(rs, 1),
                  pl.BlockSpec((None, d, tf), lambda e, f: (e, 0, f)),
                  pl.BlockSpec((None, d, tf), lambda e, f: (e, 0, f)),
                  pl.BlockSpec((None, tf, d), lambda e, f: (e, f, 0))],
        out_specs=[per_expert(rp, d), per_expert(rs, d)],
        out_shape=[jax.ShapeDtypeStruct((n_exp, rp, d), BF16), jax.ShapeDtypeStruct((n_exp, rs, d), BF16)],
        scratch_shapes=[pltpu.VMEM((rp, d), F32), pltpu.VMEM((rs, d), F32)],
        compiler_params=_cparams(2),
    )(xsel_p, xsel_s, vals_p, vals_s, w_gate, w_up, w_down)


def _scatter_kernel(out_ref, rank_ref, x_ref, g2_ref, lg_ref, lb_ref, x2_ref, *, alpha):
    n_exp, cap, d = out_ref.shape
    n = x_ref.shape[0]
    slot = lax.broadcasted_iota(jnp.int32, (cap, n), 0)
    pick = jnp.concatenate([(slot == rank_ref[e:e + 1, :]).astype(F32) for e in range(n_exp)], axis=0)
    f = jnp.dot(pick.T.astype(BF16), out_ref[...].reshape(n_exp * cap, d), preferred_element_type=F32)
    x2_ref[...] = _ln(alpha * x_ref[...] + g2_ref[...] * f) * lg_ref[...] + lb_ref[...]


def _scatter(out, rank, x1, mod4, mod_row, ln_g, ln_b, alpha, *, batch, seq):
    n_exp, _, d = out.shape
    cap = out.shape[1] // batch
    blocks_per_seq = seq // ROW_BLOCK
    return pl.pallas_call(
        functools.partial(_scatter_kernel, alpha=alpha),
        grid=(batch,),
        in_specs=[pl.BlockSpec((n_exp, cap, d), lambda b: (0, b, 0)),
                  pl.BlockSpec((None, n_exp, seq), lambda b: (b, 0, 0)),
                  pl.BlockSpec((seq, d), lambda b: (b, 0)),
                  pl.BlockSpec((None, None, 1, d), lambda b: (mod_row(b * blocks_per_seq), 5, 0, 0)),
                  pl.BlockSpec((1, d), lambda b: (0, 0)),
                  pl.BlockSpec((1, d), lambda b: (0, 0))],
        out_specs=pl.BlockSpec((seq, d), lambda b: (b, 0)),
        out_shape=jax.ShapeDtypeStruct((batch * seq, d), F32),
        compiler_params=_cparams(1),
    )(out, rank, x1, mod4, ln_g, ln_b)


def _mixers(h, p, st):
    batch, seq = st["batch"], st["seq"]
    w_in = p["w_in"]
    cols = {}
    c0 = 0
    for name, width in p["segments"]:
        cols[name] = _matmul(h, w_in, c0, width, tn=_tile(width, 512))
        c0 += width
    if st["latent"]:
        o_win = _attention(cols["q_w"], cols["k_w"], cols["v_w"], batch=batch, seq=seq,
                           k_ctx=st["ck_w"], v_ctx=st["cv_w"], sink=p["win_sink"], rope=True, band=True)
        o_glob = _attention(cols["q_g"], cols["k_g"], cols["v_g"], batch=batch, seq=seq,
                            k_ctx=st["ck_g"], v_ctx=st["cv_g"], q_norm=p["q_norm"], k_norm=p["k_norm"],
                            rope=True)
        k_g = None
    else:
        o_win = _attention(cols["q_w"], cols["k_w"], cols["v_w"], batch=batch, seq=seq, sink=p["win_sink"])
        o_glob, k_g = _attention(cols["q_g"], cols["k_g"], cols["v_g"], batch=batch, seq=seq,
                                 q_norm=p["q_norm"], k_norm=p["k_norm"], emit_k=True)
    y_ssm, fin = _s5_mix(cols["u_s"], *p["s5"], st["s0"], batch=batch, seq=seq)
    o_ssm = _glu(y_ssm, cols["u_s"], p["ssm_d"], p["ssm_w_glu"])
    y = _merge(o_win, o_ssm, o_glob, cols["gates"], p["w_up_win"], p["w_up_ssm"], p["w_up_glob"])
    return y, (cols["k_w"], cols["v_w"], k_g, cols["v_g"], fin)


def kernel(x_prompt, x_sample, cache_win_k, cache_win_v, cache_glob_k, cache_glob_v, state_ssm_re, state_ssm_im, c, c_ctx, w_mod, b_mod, w_in, win_sink, ssm_a_re, ssm_a_im, ssm_log_dt, ssm_b_re, ssm_b_im, ssm_c_re, ssm_c_im, ssm_d, ssm_w_glu, q_norm, k_norm, w_up_win, w_up_ssm, w_up_glob, w_out, ln1_g, ln1_b, ln2_g, ln2_b, router_w, exp_w_gate, exp_w_up, exp_w_down):
    bp, lp, d = x_prompt.shape
    bs, ls, _ = x_sample.shape
    depth = w_mod.shape[0]
    past = cache_win_k.shape[2]
    kvw = cache_win_k.shape[3] * HEAD_DIM
    kvg = cache_glob_k.shape[3] * HEAD_DIM
    win_w, ssm_w, glob_w = w_up_win.shape[1], w_up_ssm.shape[1], w_up_glob.shape[1]
    n_exp = router_w.shape[-1]
    p_n = ssm_a_re.shape[-1]
    alpha = (2.0 * depth) ** 0.25
    segments = (("q_w", win_w), ("k_w", kvw), ("v_w", kvw), ("u_s", ssm_w),
                ("q_g", glob_w), ("k_g", kvg), ("v_g", kvg), ("gates", 3 * d))
    assert lp % ROW_BLOCK == 0 and ls % ROW_BLOCK == 0 and 1 + bs <= 8

    cond = jnp.zeros((8, d), F32).at[0].set(c_ctx).at[1:1 + bs].set(c)
    sample_blocks = ls // ROW_BLOCK
    streams = [
        dict(batch=bp, seq=lp, latent=False, mod_row=lambda i: 0,
             s0=jnp.zeros((ssm_w // LANES, bp, 4 * S5_OCT * p_n), F32)),
        dict(batch=bs, seq=ls, latent=True, mod_row=lambda i: 1 + i // sample_blocks),
    ]
    xs = [x_prompt.reshape(bp * lp, d), x_sample.reshape(bs * ls, d)]
    new = {k: [] for k in ("wk", "wv", "gk", "gv", "sre", "sim")}

    for l in range(depth):
        p = dict(w_in=w_in[l], win_sink=win_sink[l], segments=segments,
                 q_norm=q_norm[l], k_norm=k_norm[l], ssm_d=ssm_d[l].reshape(1, ssm_w),
                 ssm_w_glu=ssm_w_glu[l].astype(BF16), w_up_win=w_up_win[l].astype(BF16),
                 w_up_ssm=w_up_ssm[l].astype(BF16), w_up_glob=w_up_glob[l].astype(BF16))
        p["s5"] = _s5_params(dict(ssm_log_dt=ssm_log_dt[l], ssm_a_re=ssm_a_re[l], ssm_a_im=ssm_a_im[l],
                                  ssm_b_re=ssm_b_re[l], ssm_b_im=ssm_b_im[l],
                                  ssm_c_re=ssm_c_re[l], ssm_c_im=ssm_c_im[l]))
        w_out_bf = w_out[l].astype(BF16)
        router_pad = jnp.zeros((d, LANES), F32).at[:, :n_exp].set(router_w[l])
        mod = _matmul(cond, w_mod[l], 0, 6 * d, bias=b_mod[l].reshape(1, 6 * d), act="silu", tn=1024)
        mod4 = mod.reshape(8, 6, 1, d)
        streams[1].update(
            ck_w=cache_win_k[:, l].reshape(bs, past, kvw), cv_w=cache_win_v[:, l].reshape(bs, past, kvw),
            ck_g=cache_glob_k[:, l].reshape(bs, past, kvg), cv_g=cache_glob_v[:, l].reshape(bs, past, kvg),
            s0=_state_to_lanes(state_ssm_re[:, l], state_ssm_im[:, l]))

        routed = []
        for si, st in enumerate(streams):
            x = xs[si]
            h = _ln_mod(x, mod4, st["mod_row"])
            y, extras = _mixers(h, p, st)
            x1, h2, logits = _outproj(y, x, w_out_bf, mod4, st["mod_row"], ln1_g[l].reshape(1, d),
                                      ln1_b[l].reshape(1, d), router_pad, alpha)
            xsel, vals, rank = _route(logits, h2, batch=st["batch"], seq=st["seq"], n_exp=n_exp)
            routed.append((x1, xsel, vals, rank))
            if not st["latent"]:
                k_w, v_w, k_g, v_g, fin = extras
                new["wk"].append(k_w.reshape(bp, lp, -1, HEAD_DIM))
                new["wv"].append(v_w.reshape(bp, lp, -1, HEAD_DIM))
                new["gk"].append(k_g.reshape(bp, lp, -1, HEAD_DIM))
                new["gv"].append(v_g.reshape(bp, lp, -1, HEAD_DIM))
                s_re, s_im = _lanes_to_state(fin, p_n)
                new["sre"].append(s_re)
                new["sim"].append(s_im)
        outs = _ffn(routed[0][1], routed[1][1], routed[0][2], routed[1][2],
                    exp_w_gate[l], exp_w_up[l], exp_w_down[l])
        for si, st in enumerate(streams):
            xs[si] = _scatter(outs[si], routed[si][3], routed[si][0], mod4, st["mod_row"],
                              ln2_g[l].reshape(1, d), ln2_b[l].reshape(1, d), alpha,
                              batch=st["batch"], seq=st["seq"])

    return (xs[0].reshape(bp, lp, d), xs[1].reshape(bs, ls, d),
            jnp.stack(new["wk"], axis=1), jnp.stack(new["wv"], axis=1),
            jnp.stack(new["gk"], axis=1), jnp.stack(new["gv"], axis=1),
            jnp.stack(new["sre"], axis=1), jnp.stack(new["sim"], axis=1))
```

```python
import functools

import numpy as np
import jax
import jax.numpy as jnp
from jax import lax
from jax.experimental import pallas as pl
from jax.experimental.pallas import tpu as pltpu

F32 = jnp.float32
BF16 = jnp.bfloat16
HIGHEST = lax.Precision.HIGHEST

HEAD_DIM = 128
LANES = 128
GRID_W = 64
WINDOW = 128
EC_FACTOR = 2
ROPE_THETA = 10000.0
EPS = 1e-6
NEG_INF = -1e30
ATTN_SCALE = HEAD_DIM ** -0.5
S5_CHUNK = 16
S5_OCT = 8
ROW_BLOCK = 256
VMEM_LIMIT = 60 * 1024 * 1024
RESIDENT = pl.Buffered(1)


def _cparams(n_axes):
    return pltpu.CompilerParams(dimension_semantics=("arbitrary",) * n_axes,
                                vmem_limit_bytes=VMEM_LIMIT)


def _tile(dim, pref):
    return pref if dim % pref == 0 else dim


def _ln(x):
    mu = jnp.mean(x, axis=-1, keepdims=True)
    xc = x - mu
    var = jnp.mean(xc * xc, axis=-1, keepdims=True)
    return xc * lax.rsqrt(var + EPS)


def _mm_kernel(x_ref, w_ref, *rest, act, has_bias):
    if has_bias:
        b_ref, o_ref, wbf_ref = rest
    else:
        o_ref, wbf_ref = rest

    @pl.when(pl.program_id(1) == 0)
    def _():
        wbf_ref[...] = w_ref[...].astype(BF16)

    x = x_ref[...]
    if act == "silu":
        x = jax.nn.silu(x.astype(F32))
    acc = jnp.dot(x.astype(BF16), wbf_ref[...], preferred_element_type=F32)
    if has_bias:
        acc = acc + b_ref[...]
    o_ref[...] = acc.astype(o_ref.dtype)


def _matmul(x, w, layer, col0, ncols, *, bias=None, act=None, out_dtype=F32, tm=1024, tn=512):
    m, k = x.shape
    tm = _tile(m, tm)
    tn = next(t for t in (tn, 512, 256, LANES) if ncols % t == 0 and col0 % t == 0)
    c0 = col0 // tn
    in_specs = [pl.BlockSpec((tm, k), lambda n, i: (i, 0)),
                pl.BlockSpec((None, k, tn), lambda n, i: (layer, 0, c0 + n))]
    args = [x, w]
    if bias is not None:
        in_specs.append(pl.BlockSpec((None, 1, tn), lambda n, i: (layer, 0, c0 + n)))
        args.append(bias)
    return pl.pallas_call(
        functools.partial(_mm_kernel, act=act, has_bias=bias is not None),
        grid=(ncols // tn, m // tm),
        in_specs=in_specs,
        out_specs=pl.BlockSpec((tm, tn), lambda n, i: (i, n)),
        out_shape=jax.ShapeDtypeStruct((m, ncols), out_dtype),
        scratch_shapes=[pltpu.VMEM((k, tn), BF16)],
        compiler_params=_cparams(2),
    )(*args)


def _ln_mod_kernel(x_ref, sc_ref, sh_ref, h_ref):
    h_ref[...] = (_ln(x_ref[...]) * (1.0 + sc_ref[...]) + sh_ref[...]).astype(h_ref.dtype)


def _mod_spec(d, chunk, mod_row, rows_per_step=ROW_BLOCK):
    scale = rows_per_step // ROW_BLOCK
    return pl.BlockSpec((None, None, 1, d), lambda i: (mod_row(i * scale), chunk, 0, 0))


def _ln_mod(x, mod4, mod_row):
    t, d = x.shape
    return pl.pallas_call(
        _ln_mod_kernel,
        grid=(t // ROW_BLOCK,),
        in_specs=[pl.BlockSpec((ROW_BLOCK, d), lambda i: (i, 0)),
                  _mod_spec(d, 1, mod_row), _mod_spec(d, 0, mod_row)],
        out_specs=pl.BlockSpec((ROW_BLOCK, d), lambda i: (i, 0)),
        out_shape=jax.ShapeDtypeStruct((t, d), BF16),
        compiler_params=_cparams(1),
    )(x, mod4, mod4)


def _rope_tables(seq_len):
    half = HEAD_DIM // 4
    inv = ROPE_THETA ** (-np.arange(half, dtype=np.float64) / half)
    tok = np.arange(seq_len)
    ang_r = (tok // GRID_W)[:, None] * inv[None, :]
    ang_c = (tok % GRID_W)[:, None] * inv[None, :]
    cos = np.concatenate([np.cos(ang_r), np.cos(ang_r), np.cos(ang_c), np.cos(ang_c)], axis=-1)
    sin = np.concatenate([-np.sin(ang_r), np.sin(ang_r), -np.sin(ang_c), np.sin(ang_c)], axis=-1)
    return jnp.asarray(cos, F32), jnp.asarray(sin, F32)


def _rope(x, cos, sin_signed):
    lane = lax.broadcasted_iota(jnp.int32, x.shape, 1)
    swapped = jnp.where((lane % 64) < 32, pltpu.roll(x, 96, 1), pltpu.roll(x, 32, 1))
    return x * cos + swapped * sin_signed


def _rms(x, w):
    return x * lax.rsqrt(jnp.mean(x * x, axis=-1, keepdims=True) + EPS) * w


def _attn_kernel(*refs, heads, kv, tq, rope, norm, sink, ctx, band, emit_k):
    it = iter(refs)
    q_ref, k_ref, v_ref = next(it), next(it), next(it)
    kc_ref = vc_ref = sink_ref = qn_ref = kn_ref = cq_ref = sq_ref = ck_ref = sk_ref = None
    if ctx:
        kc_ref, vc_ref = next(it), next(it)
    if sink:
        sink_ref = next(it)
    if norm:
        qn_ref, kn_ref = next(it), next(it)
    if rope:
        cq_ref, sq_ref, ck_ref, sk_ref = next(it), next(it), next(it), next(it)
    o_ref = next(it)
    ko_ref = next(it) if emit_k else None
    kp_ref, vp_ref = next(it), next(it)

    qi = pl.program_id(1)
    rep = heads // kv

    @pl.when(qi == 0)
    def _():
        for g in range(kv):
            sl = slice(g * HEAD_DIM, (g + 1) * HEAD_DIM)
            kg = k_ref[:, sl]
            if norm:
                kg = _rms(kg, kn_ref[...])
            if rope:
                kg = _rope(kg, ck_ref[...], sk_ref[...])
            if emit_k:
                ko_ref[:, sl] = kg
            kp_ref[:, sl] = kg.astype(BF16)
        vp_ref[...] = v_ref[...].astype(BF16)

    nt = (((1,), (1,)), ((), ()))
    for h in range(heads):
        g = h // rep
        gs = slice(g * HEAD_DIM, (g + 1) * HEAD_DIM)
        qh = q_ref[:, h * HEAD_DIM:(h + 1) * HEAD_DIM]
        if norm:
            qh = _rms(qh, qn_ref[...])
        if rope:
            qh = _rope(qh, cq_ref[...], sq_ref[...])
        qh = qh.astype(BF16)
        s = lax.dot_general(qh, kp_ref[:, gs], nt, preferred_element_type=F32) * ATTN_SCALE
        if band:
            qpos = qi * tq + lax.broadcasted_iota(jnp.int32, s.shape, 0)
            kpos = lax.broadcasted_iota(jnp.int32, s.shape, 1)
            s = jnp.where(jnp.abs(qpos - kpos) <= WINDOW, s, NEG_INF)
        m = jnp.max(s, axis=-1, keepdims=True)
        if ctx:
            sc = lax.dot_general(qh, kc_ref[:, gs].astype(BF16), nt,
                                 preferred_element_type=F32) * ATTN_SCALE
            m = jnp.maximum(m, jnp.max(sc, axis=-1, keepdims=True))
        if sink:
            m = jnp.maximum(m, sink_ref[h])
        e = jnp.exp(s - m)
        den = jnp.sum(e, axis=-1, keepdims=True)
        o = jnp.dot(e.astype(BF16), vp_ref[:, gs], preferred_element_type=F32)
        if ctx:
            ec = jnp.exp(sc - m)
            den = den + jnp.sum(ec, axis=-1, keepdims=True)
            o = o + jnp.dot(ec.astype(BF16), vc_ref[:, gs].astype(BF16), preferred_element_type=F32)
        if sink:
            den = den + jnp.exp(sink_ref[h] - m)
        o_ref[:, h * HEAD_DIM:(h + 1) * HEAD_DIM] = (o / den).astype(o_ref.dtype)


def _attention(qa, q0, ka, k0, va, v0, *, heads, kv, batch, seq, k_ctx=None, v_ctx=None, sink=None,
               q_norm=None, k_norm=None, rope=False, band=False, emit_k=False):
    t = qa.shape[0]
    qw, kw = heads * HEAD_DIM, kv * HEAD_DIM
    assert q0 % qw == 0 and k0 % kw == 0 and v0 % kw == 0
    qc, kc, vc = q0 // qw, k0 // kw, v0 // kw
    tq = _tile(seq, 256)
    nq = seq // tq
    ctx, has_sink, norm = k_ctx is not None, sink is not None, q_norm is not None
    in_specs = [pl.BlockSpec((tq, qw), lambda b, i: (b * nq + i, qc)),
                pl.BlockSpec((seq, kw), lambda b, i: (b, kc)),
                pl.BlockSpec((seq, kw), lambda b, i: (b, vc))]
    args = [qa, ka, va]
    if ctx:
        lc = k_ctx.shape[1]
        in_specs += [pl.BlockSpec((None, lc, kw), lambda b, i: (b, 0, 0))] * 2
        args += [k_ctx, v_ctx]
    if has_sink:
        in_specs.append(pl.BlockSpec(memory_space=pltpu.SMEM))
        args.append(sink)
    if norm:
        in_specs += [pl.BlockSpec((1, HEAD_DIM), lambda b, i: (0, 0))] * 2
        args += [q_norm.reshape(1, HEAD_DIM), k_norm.reshape(1, HEAD_DIM)]
    if rope:
        cos, sin = _rope_tables(seq)
        in_specs += [pl.BlockSpec((tq, HEAD_DIM), lambda b, i: (i, 0))] * 2
        in_specs += [pl.BlockSpec((seq, HEAD_DIM), lambda b, i: (0, 0))] * 2
        args += [cos, sin, cos, sin]
    out_specs = [pl.BlockSpec((tq, qw), lambda b, i: (b * nq + i, 0))]
    out_shape = [jax.ShapeDtypeStruct((t, qw), BF16)]
    if emit_k:
        out_specs.append(pl.BlockSpec((seq, kw), lambda b, i: (b, 0)))
        out_shape.append(jax.ShapeDtypeStruct((t, kw), F32))
    res = pl.pallas_call(
        functools.partial(_attn_kernel, heads=heads, kv=kv, tq=tq, rope=rope, norm=norm,
                          sink=has_sink, ctx=ctx, band=band, emit_k=emit_k),
        grid=(batch, nq),
        in_specs=in_specs,
        out_specs=out_specs,
        out_shape=out_shape,
        scratch_shapes=[pltpu.VMEM((seq, kw), BF16), pltpu.VMEM((seq, kw), BF16)],
        compiler_params=_cparams(2),
    )(*args)
    return res if emit_k else res[0]


def _s5_layouts(log_dt, a_re, a_im, b_re, b_im, c_re, c_im):
    dep, _, g_n, p_n = a_re.shape
    gc = b_re.shape[-1]
    no = g_n // S5_OCT
    sw = S5_OCT * p_n
    ldt = jnp.broadcast_to(log_dt[..., None], a_re.shape)
    prm = jnp.stack([ldt, a_re, a_im], axis=2).astype(F32)
    prow = jnp.pad(prm.reshape(dep, 2, 3, no, sw).transpose(0, 1, 3, 2, 4),
                   ((0, 0),) * 3 + ((0, 5), (0, 0)))
    pcol = jnp.pad(prm.reshape(dep, 2, 3, no, sw).transpose(0, 1, 3, 4, 2),
                   ((0, 0),) * 4 + ((0, 5),))
    pcmp = jnp.broadcast_to(prm[:, :, :, :, None, :], (dep, 2, 3, g_n, gc, p_n))
    pcmp = pcmp.reshape(dep, 2, 3, no, S5_OCT * gc, p_n).transpose(0, 1, 3, 2, 4, 5)

    def b_views(b):
        bt = jnp.swapaxes(b.astype(F32), -1, -2).reshape(dep, 2, no, S5_OCT * gc, p_n)
        return bt, jnp.tile(bt, (1, 1, 1, 1, S5_OCT))

    def c_views(c):
        cc = c.astype(F32).reshape(dep, 2, no, S5_OCT * gc, p_n)
        ct = jnp.swapaxes(c.astype(F32), -1, -2).reshape(dep, 2, no, sw, gc)
        return cc, jnp.tile(ct, (1, 1, 1, 1, S5_OCT))

    bc_re, bt_re = b_views(b_re)
    bc_im, bt_im = b_views(b_im)
    cc_re, ct_re = c_views(c_re)
    cc_im, ct_im = c_views(c_im)
    return dict(prow=prow, pcol=pcol, pcmp=pcmp, bc_re=bc_re, bc_im=bc_im, bt_re=bt_re, bt_im=bt_im,
                cc_re=cc_re, cc_im=cc_im, ct_re=ct_re, ct_im=ct_im)


def _lam_powers(ldt, ar, ai, k):
    dt = jnp.exp(ldt)
    mag = jnp.exp(k * (dt * ar))
    return mag * jnp.cos(k * (dt * ai)), mag * jnp.sin(k * (dt * ai))


def _bbar_coef(ldt, ar, ai):
    lam_re, lam_im = _lam_powers(ldt, ar, ai, 1.0)
    den = ar * ar + ai * ai
    return ((lam_re - 1.0) * ar + lam_im * ai) / den, (lam_im * ar - (lam_re - 1.0) * ai) / den


def _same_group(shape, rows_per_group, cols_per_group):
    r = lax.broadcasted_iota(jnp.int32, shape, 0) // rows_per_group
    c = lax.broadcasted_iota(jnp.int32, shape, 1) // cols_per_group
    return r == c


def _s5_prep_w_kernel(prow_ref, btr_ref, bti_ref, w_ref, lam_ref):
    tc = S5_CHUNK
    rows, sw = btr_ref.shape[1], btr_ref.shape[2]
    mask = _same_group((rows, sw), rows // S5_OCT, sw // S5_OCT)
    k = lax.broadcasted_iota(jnp.int32, (24, sw), 0).astype(F32)
    for d in range(2):
        ldt, ar, ai = prow_ref[d, 0:1, :], prow_ref[d, 1:2, :], prow_ref[d, 2:3, :]
        pw_re, pw_im = _lam_powers(ldt, ar, ai, k)
        co_re, co_im = _bbar_coef(ldt, ar, ai)
        bb_re = jnp.where(mask, co_re * btr_ref[d] - co_im * bti_ref[d], 0.0)
        bb_im = jnp.where(mask, co_re * bti_ref[d] + co_im * btr_ref[d], 0.0)
        lam_ref[2 * d:2 * d + 1, :] = pw_re[tc:tc + 1]
        lam_ref[2 * d + 1:2 * d + 2, :] = pw_im[tc:tc + 1]
        for j in range(tc):
            kk = tc - 1 - j if d == 0 else j
            pr, pi = pw_re[kk:kk + 1], pw_im[kk:kk + 1]
            rs = slice(j * rows, (j + 1) * rows)
            w_ref[rs, (2 * d) * sw:(2 * d + 1) * sw] = (pr * bb_re - pi * bb_im).astype(w_ref.dtype)
            w_ref[rs, (2 * d + 1) * sw:(2 * d + 2) * sw] = (pr * bb_im + pi * bb_re).astype(w_ref.dtype)


def _s5_prep_z_kernel(pcol_ref, ctr_ref, cti_ref, z_ref):
    tc = S5_CHUNK
    sw, cols = ctr_ref.shape[1], ctr_ref.shape[2]
    mask = _same_group((sw, cols), sw // S5_OCT, cols // S5_OCT)
    k = lax.broadcasted_iota(jnp.int32, (sw, LANES), 1).astype(F32)
    for d in range(2):
        ldt, ar, ai = pcol_ref[d, :, 0:1], pcol_ref[d, :, 1:2], pcol_ref[d, :, 2:3]
        pw_re, pw_im = _lam_powers(ldt, ar, ai, k)
        c_re = jnp.where(mask, ctr_ref[d], 0.0)
        c_im = jnp.where(mask, cti_ref[d], 0.0)
        for t in range(tc):
            kk = t + 1 if d == 0 else tc - t
            pr, pi = pw_re[:, kk:kk + 1], pw_im[:, kk:kk + 1]
            cs = slice(t * cols, (t + 1) * cols)
            z_ref[(2 * d) * sw:(2 * d + 1) * sw, cs] = (c_re * pr - c_im * pi).astype(z_ref.dtype)
            z_ref[(2 * d + 1) * sw:(2 * d + 2) * sw, cs] = (-(c_re * pi + c_im * pr)).astype(z_ref.dtype)


def _s5_prep_m_kernel(pcmp_ref, bcr_ref, bci_ref, ccr_ref, cci_ref, m_ref, xr_ref, xi_ref, taps_ref):
    tc = S5_CHUNK
    rows = bcr_ref.shape[1]
    mask = _same_group((rows, rows), rows // S5_OCT, rows // S5_OCT)
    nt = (((1,), (1,)), ((), ()))
    for d in range(2):
        ldt, ar, ai = pcmp_ref[d, 0], pcmp_ref[d, 1], pcmp_ref[d, 2]
        co_re, co_im = _bbar_coef(ldt, ar, ai)
        bb_re = co_re * bcr_ref[d] - co_im * bci_ref[d]
        bb_im = co_re * bci_ref[d] + co_im * bcr_ref[d]
        for kk in range(tc):
            pr, pi = _lam_powers(ldt, ar, ai, float(kk))
            xr_ref[kk * rows:(kk + 1) * rows, :] = pr * bb_re - pi * bb_im
            xi_ref[kk * rows:(kk + 1) * rows, :] = pr * bb_im + pi * bb_re
        taps = (lax.dot_general(xr_ref[...], ccr_ref[d], nt, precision=HIGHEST, preferred_element_type=F32)
                - lax.dot_general(xi_ref[...], cci_ref[d], nt, precision=HIGHEST, preferred_element_type=F32))
        for kk in range(tc):
            tile = jnp.where(mask, taps[kk * rows:(kk + 1) * rows, :], 0.0)
            col = (tc - 1 + kk) if d == 0 else (tc - 1 - kk)
            cs = slice(col * rows, (col + 1) * rows)
            if d == 1 and kk == 0:
                taps_ref[:, cs] += tile
            else:
                taps_ref[:, cs] = tile
    for j in range(tc):
        m_ref[j * rows:(j + 1) * rows, :] = taps_ref[:, (tc - 1 - j) * rows:(2 * tc - 1 - j) * rows].astype(m_ref.dtype)


def _s5_prep(lay, layer):
    _, _, no, rows, sw = lay["bt_re"].shape
    p_n = lay["bc_re"].shape[-1]
    tc = S5_CHUNK
    kw = tc * rows

    def per_octet(*tail):
        nd = len(tail)
        return pl.BlockSpec((None, 2, None) + tail, lambda o: (layer, 0, o) + (0,) * nd)

    def out(r, c):
        return pl.BlockSpec((None, r, c), lambda o: (o, 0, 0))

    w_o, lam_o = pl.pallas_call(
        _s5_prep_w_kernel, grid=(no,),
        in_specs=[per_octet(8, sw), per_octet(rows, sw), per_octet(rows, sw)],
        out_specs=[out(kw, 4 * sw), out(4, sw)],
        out_shape=[jax.ShapeDtypeStruct((no, kw, 4 * sw), BF16), jax.ShapeDtypeStruct((no, 4, sw), F32)],
        compiler_params=_cparams(1),
    )(lay["prow"], lay["bt_re"], lay["bt_im"])
    z_o = pl.pallas_call(
        _s5_prep_z_kernel, grid=(no,),
        in_specs=[per_octet(sw, 8), per_octet(sw, rows), per_octet(sw, rows)],
        out_specs=out(4 * sw, kw),
        out_shape=jax.ShapeDtypeStruct((no, 4 * sw, kw), BF16),
        compiler_params=_cparams(1),
    )(lay["pcol"], lay["ct_re"], lay["ct_im"])
    m_o = pl.pallas_call(
        _s5_prep_m_kernel, grid=(no,),
        in_specs=[per_octet(3, rows, p_n)] + [per_octet(rows, p_n)] * 4,
        out_specs=out(kw, kw),
        out_shape=jax.ShapeDtypeStruct((no, kw, kw), BF16),
        scratch_shapes=[pltpu.VMEM((kw, p_n), F32), pltpu.VMEM((kw, p_n), F32),
                        pltpu.VMEM((rows, (2 * tc - 1) * rows), F32)],
        compiler_params=_cparams(1),
    )(lay["pcmp"], lay["bc_re"], lay["bc_im"], lay["cc_re"], lay["cc_im"])
    return m_o, w_o, z_o, lam_o


def _s5_state_kernel(u_ref, w_ref, lam_ref, s0_ref, sp_ref, fin_ref, ubf_ref, vs_ref, st_ref,
                     *, batch, nchunk):
    tc = S5_CHUNK
    for t in range(tc):
        ubf_ref[:, t * LANES:(t + 1) * LANES] = u_ref[:, t, :].astype(BF16)
    q = st_ref.shape[-1] // 4
    vs_ref[...] = jnp.dot(ubf_ref[...], w_ref[...],
                          preferred_element_type=F32).reshape(batch, nchunk, 4 * q)
    fa, fb = lam_ref[0:1, :], lam_ref[1:2, :]
    ba, bb = lam_ref[2:3, :], lam_ref[3:4, :]
    s_re, s_im = s0_ref[:, 0:q], s0_ref[:, q:2 * q]
    for c in range(nchunk):
        st_ref[:, c, 0:q] = s_re
        st_ref[:, c, q:2 * q] = s_im
        v_re, v_im = vs_ref[:, c, 0:q], vs_ref[:, c, q:2 * q]
        s_re, s_im = fa * s_re - fb * s_im + v_re, fa * s_im + fb * s_re + v_im
    fin_ref[:, 0:q] = s_re
    fin_ref[:, q:2 * q] = s_im
    s_re, s_im = s0_ref[:, 2 * q:3 * q], s0_ref[:, 3 * q:4 * q]
    for c in range(nchunk - 1, -1, -1):
        st_ref[:, c, 2 * q:3 * q] = s_re
        st_ref[:, c, 3 * q:4 * q] = s_im
        v_re, v_im = vs_ref[:, c, 2 * q:3 * q], vs_ref[:, c, 3 * q:4 * q]
        s_re, s_im = ba * s_re - bb * s_im + v_re, ba * s_im + bb * s_re + v_im
    fin_ref[:, 2 * q:3 * q] = s_re
    fin_ref[:, 3 * q:4 * q] = s_im
    sp_ref[...] = st_ref[...].reshape(batch * nchunk, 4 * q).astype(sp_ref.dtype)


def _s5_out_kernel(u_ref, sp_ref, m_ref, z_ref, y_ref, ubf_ref):
    tc = S5_CHUNK
    for t in range(tc):
        ubf_ref[:, t * LANES:(t + 1) * LANES] = u_ref[:, t, :].astype(BF16)
    y = (jnp.dot(ubf_ref[...], m_ref[...], preferred_element_type=F32)
         + jnp.dot(sp_ref[...], z_ref[...], preferred_element_type=F32))
    for t in range(tc):
        y_ref[:, t, :] = y[:, t * LANES:(t + 1) * LANES]


def _s5_mix(ua, u0, width, m_o, w_o, z_o, lam_o, s0, *, batch, seq):
    t, full = ua.shape
    tc = S5_CHUNK
    no = width // LANES
    assert u0 % LANES == 0
    uc = u0 // LANES
    r = t // tc
    nchunk = seq // tc
    sw = w_o.shape[-1]
    kw = tc * LANES
    u3 = ua.reshape(r, tc, full)
    sp, fin = pl.pallas_call(
        functools.partial(_s5_state_kernel, batch=batch, nchunk=nchunk),
        grid=(no,),
        in_specs=[pl.BlockSpec((r, tc, LANES), lambda o: (0, 0, uc + o)),
                  pl.BlockSpec((None, kw, sw), lambda o: (o, 0, 0)),
                  pl.BlockSpec((None, 4, sw // 4), lambda o: (o, 0, 0)),
                  pl.BlockSpec((None, batch, sw), lambda o: (o, 0, 0))],
        out_specs=[pl.BlockSpec((None, r, sw), lambda o: (o, 0, 0)),
                   pl.BlockSpec((None, batch, sw), lambda o: (o, 0, 0))],
        out_shape=[jax.ShapeDtypeStruct((no, r, sw), BF16),
                   jax.ShapeDtypeStruct((no, batch, sw), F32)],
        scratch_shapes=[pltpu.VMEM((r, kw), BF16),
                        pltpu.VMEM((batch, nchunk, sw), F32),
                        pltpu.VMEM((batch, nchunk, sw), F32)],
        compiler_params=_cparams(1),
    )(u3, w_o, lam_o, s0)
    tr = _tile(r, 256)
    y3 = pl.pallas_call(
        _s5_out_kernel,
        grid=(no, r // tr),
        in_specs=[pl.BlockSpec((tr, tc, LANES), lambda o, i: (i, 0, uc + o)),
                  pl.BlockSpec((None, tr, sw), lambda o, i: (o, i, 0)),
                  pl.BlockSpec((None, kw, kw), lambda o, i: (o, 0, 0)),
                  pl.BlockSpec((None, sw, kw), lambda o, i: (o, 0, 0))],
        out_specs=pl.BlockSpec((tr, tc, LANES), lambda o, i: (i, 0, o)),
        out_shape=jax.ShapeDtypeStruct((r, tc, width), F32),
        scratch_shapes=[pltpu.VMEM((tr, kw), BF16)],
        compiler_params=_cparams(2),
    )(u3, sp, m_o, z_o)
    return y3.reshape(t, width), fin


def _state_to_lanes(s_re, s_im):
    b, _, g_n, p_n = s_re.shape
    no = g_n // S5_OCT
    parts = jnp.stack([s_re[:, 0], s_im[:, 0], s_re[:, 1], s_im[:, 1]], axis=1)
    parts = parts.reshape(b, 4, no, S5_OCT * p_n)
    return jnp.transpose(parts, (2, 0, 1, 3)).reshape(no, b, 4 * S5_OCT * p_n).astype(F32)


def _lanes_to_state(fin, p_n):
    no, b, _ = fin.shape
    parts = jnp.transpose(fin.reshape(no, b, 4, S5_OCT, p_n), (1, 2, 0, 3, 4)).reshape(b, 4, no * S5_OCT, p_n)
    return jnp.stack([parts[:, 0], parts[:, 2]], axis=1), jnp.stack([parts[:, 1], parts[:, 3]], axis=1)


def _glu_kernel(y_ref, u_ref, d_ref, w_ref, o_ref):
    z = jax.nn.gelu(y_ref[...] + d_ref[...] * u_ref[...]).astype(BF16)
    g = jnp.dot(z, w_ref[...], preferred_element_type=F32)
    w = o_ref.shape[-1]
    o_ref[...] = (g[:, :w] * jax.nn.sigmoid(g[:, w:])).astype(o_ref.dtype)


def _glu(y, ua, u0, d, w_glu, layer):
    t, width = y.shape
    assert u0 % width == 0
    uc = u0 // width
    tm = _tile(t, 512)
    return pl.pallas_call(
        _glu_kernel,
        grid=(t // tm,),
        in_specs=[pl.BlockSpec((tm, width), lambda i: (i, 0)),
                  pl.BlockSpec((tm, width), lambda i: (i, uc)),
                  pl.BlockSpec((None, 1, width), lambda i: (layer, 0, 0)),
                  pl.BlockSpec((None, width, 2 * width), lambda i: (layer, 0, 0), pipeline_mode=RESIDENT)],
        out_specs=pl.BlockSpec((tm, width), lambda i: (i, 0)),
        out_shape=jax.ShapeDtypeStruct((t, width), BF16),
        compiler_params=_cparams(1),
    )(y, ua, d, w_glu)


def _merge_kernel(ow_ref, os_ref, og_ref, gw_ref, gs_ref, gg_ref, ww_ref, ws_ref, wg_ref, y_ref):
    y = (jax.nn.sigmoid(gw_ref[...]) * jnp.dot(ow_ref[...], ww_ref[...], preferred_element_type=F32)
         + jax.nn.sigmoid(gs_ref[...]) * jnp.dot(os_ref[...], ws_ref[...], preferred_element_type=F32)
         + jax.nn.sigmoid(gg_ref[...]) * jnp.dot(og_ref[...], wg_ref[...], preferred_element_type=F32))
    y_ref[...] = y.astype(y_ref.dtype)


def _merge(o_win, o_ssm, o_glob, gates, w_win, w_ssm, w_glob, layer):
    t = o_win.shape[0]
    d = w_win.shape[-1]
    tm = _tile(t, 512)

    def rows(w):
        return pl.BlockSpec((tm, w), lambda i: (i, 0))

    def whole(a):
        return pl.BlockSpec((None,) + a.shape[1:], lambda i: (layer, 0, 0), pipeline_mode=RESIDENT)

    return pl.pallas_call(
        _merge_kernel,
        grid=(t // tm,),
        in_specs=[rows(o_win.shape[1]), rows(o_ssm.shape[1]), rows(o_glob.shape[1]),
                  pl.BlockSpec((tm, d), lambda i: (i, 0)),
                  pl.BlockSpec((tm, d), lambda i: (i, 1)),
                  pl.BlockSpec((tm, d), lambda i: (i, 2)),
                  whole(w_win), whole(w_ssm), whole(w_glob)],
        out_specs=rows(d),
        out_shape=jax.ShapeDtypeStruct((t, d), BF16),
        compiler_params=_cparams(1),
    )(o_win, o_ssm, o_glob, gates, gates, gates, w_win, w_ssm, w_glob)


def _outproj_kernel(y_ref, x_ref, w_ref, g1_ref, sc2_ref, sh2_ref, lg_ref, lb_ref, rw_ref,
                    x1_ref, h2_ref, logit_ref, *, alpha):
    o = jnp.dot(y_ref[...], w_ref[...], preferred_element_type=F32)
    x1 = _ln(alpha * x_ref[...] + g1_ref[...] * o) * lg_ref[...] + lb_ref[...]
    x1_ref[...] = x1
    h2 = _ln(x1) * (1.0 + sc2_ref[...]) + sh2_ref[...]
    h2_ref[...] = h2.astype(h2_ref.dtype)
    logit_ref[...] = jnp.dot(h2, rw_ref[...], preferred_element_type=F32, precision=HIGHEST)


def _outproj(y, x, w_out, mod4, mod_row, ln_g, ln_b, router_pad, layer, alpha, tm):
    t, d = x.shape

    def rows(w):
        return pl.BlockSpec((tm, w), lambda i: (i, 0))

    def vec():
        return pl.BlockSpec((None, 1, d), lambda i: (layer, 0, 0))

    def mod(chunk):
        return _mod_spec(d, chunk, mod_row, tm)

    return pl.pallas_call(
        functools.partial(_outproj_kernel, alpha=alpha),
        grid=(t // tm,),
        in_specs=[rows(d), rows(d),
                  pl.BlockSpec((None, d, d), lambda i: (layer, 0, 0), pipeline_mode=RESIDENT),
                  mod(2), mod(4), mod(3), vec(), vec(),
                  pl.BlockSpec((None, d, LANES), lambda i: (layer, 0, 0), pipeline_mode=RESIDENT)],
        out_specs=[rows(d), rows(d), rows(LANES)],
        out_shape=[jax.ShapeDtypeStruct((t, d), F32), jax.ShapeDtypeStruct((t, d), BF16),
                   jax.ShapeDtypeStruct((t, LANES), F32)],
        compiler_params=_cparams(1),
    )(y, x, w_out, mod4, mod4, mod4, ln_g, ln_b, router_pad)


def _route_kernel(logit_ref, h_ref, xsel_ref, vals_ref, rank_ref, *, n_exp, cap):
    n = logit_ref.shape[0]
    lg = logit_ref[...]
    col = lax.broadcasted_iota(jnp.int32, lg.shape, 1)
    lg = jnp.where(col < n_exp, lg, -jnp.inf)
    ex = jnp.exp(lg - jnp.max(lg, axis=-1, keepdims=True))
    aff = ex / jnp.sum(ex, axis=-1, keepdims=True)
    aff_t = aff.T
    jj = lax.broadcasted_iota(jnp.int32, (n, n), 0)
    ii = lax.broadcasted_iota(jnp.int32, (n, n), 1)
    slot = lax.broadcasted_iota(jnp.int32, (cap, n), 0)
    h = h_ref[...]
    for e in range(n_exp):
        a_row = aff_t[e:e + 1, :]
        a_col = aff[:, e:e + 1]
        beats = (a_col > a_row) | ((a_col == a_row) & (jj < ii))
        rank = jnp.sum(beats.astype(jnp.int32), axis=0, keepdims=True)
        pick = slot == rank
        xsel_ref[e] = jnp.dot(pick.astype(BF16), h, preferred_element_type=F32).astype(xsel_ref.dtype)
        vals_ref[e] = jnp.sum(jnp.where(pick, a_row, 0.0), axis=1, keepdims=True)
        rank_ref[e:e + 1, :] = rank


def _route(logits, h2, *, batch, seq, n_exp):
    t, d = h2.shape
    cap = EC_FACTOR * seq // n_exp
    return pl.pallas_call(
        functools.partial(_route_kernel, n_exp=n_exp, cap=cap),
        grid=(batch,),
        in_specs=[pl.BlockSpec((seq, LANES), lambda b: (b, 0)),
                  pl.BlockSpec((seq, d), lambda b: (b, 0))],
        out_specs=[pl.BlockSpec((n_exp, cap, d), lambda b: (0, b, 0)),
                   pl.BlockSpec((n_exp, cap, 1), lambda b: (0, b, 0)),
                   pl.BlockSpec((None, n_exp, seq), lambda b: (b, 0, 0))],
        out_shape=[jax.ShapeDtypeStruct((n_exp, batch * cap, d), BF16),
                   jax.ShapeDtypeStruct((n_exp, batch * cap, 1), F32),
                   jax.ShapeDtypeStruct((batch, n_exp, seq), jnp.int32)],
        compiler_params=_cparams(1),
    )(logits, h2)


def _ffn_hidden_kernel(xp_ref, xs_ref, wg_ref, wu_ref, hp_ref, hs_ref):
    wg = wg_ref[...].astype(BF16)
    wu = wu_ref[...].astype(BF16)
    for x_ref, h_ref in ((xp_ref, hp_ref), (xs_ref, hs_ref)):
        x = x_ref[...]
        h_ref[...] = (jax.nn.silu(jnp.dot(x, wg, preferred_element_type=F32))
                      * jnp.dot(x, wu, preferred_element_type=F32)).astype(h_ref.dtype)


def _ffn_down_kernel(hp_ref, hs_ref, vp_ref, vs_ref, wd_ref, op_ref, os_ref):
    wd = wd_ref[...].astype(BF16)
    for h_ref, v_ref, o_ref in ((hp_ref, vp_ref, op_ref), (hs_ref, vs_ref, os_ref)):
        o_ref[...] = (jnp.dot(h_ref[...], wd, preferred_element_type=F32) * v_ref[...]).astype(o_ref.dtype)


def _ffn(xsel_p, xsel_s, vals_p, vals_s, w_gate, w_up, w_down, layer):
    n_exp, rp, d = xsel_p.shape
    rs = xsel_s.shape[1]
    ff = w_gate.shape[-1]
    tf = _tile(ff, 512)
    tn = _tile(d, 512)

    def per_expert(r, w):
        return pl.BlockSpec((None, r, w), lambda e, j: (e, 0, 0))

    def col_tile(r, w):
        return pl.BlockSpec((None, r, w), lambda e, j: (e, 0, j))

    def w_tile(k, w):
        return pl.BlockSpec((None, None, k, w), lambda e, j: (layer, e, 0, j))

    hid_p, hid_s = pl.pallas_call(
        _ffn_hidden_kernel,
        grid=(n_exp, ff // tf),
        in_specs=[per_expert(rp, d), per_expert(rs, d), w_tile(d, tf), w_tile(d, tf)],
        out_specs=[col_tile(rp, tf), col_tile(rs, tf)],
        out_shape=[jax.ShapeDtypeStruct((n_exp, rp, ff), BF16), jax.ShapeDtypeStruct((n_exp, rs, ff), BF16)],
        compiler_params=_cparams(2),
    )(xsel_p, xsel_s, w_gate, w_up)
    return pl.pallas_call(
        _ffn_down_kernel,
        grid=(n_exp, d // tn),
        in_specs=[per_expert(rp, ff), per_expert(rs, ff), per_expert(rp, 1), per_expert(rs, 1),
                  w_tile(ff, tn)],
        out_specs=[col_tile(rp, tn), col_tile(rs, tn)],
        out_shape=[jax.ShapeDtypeStruct((n_exp, rp, d), BF16), jax.ShapeDtypeStruct((n_exp, rs, d), BF16)],
        compiler_params=_cparams(2),
    )(hid_p, hid_s, vals_p, vals_s, w_down)


def _scatter_kernel(out_ref, rank_ref, x_ref, g2_ref, lg_ref, lb_ref, x2_ref, *, alpha):
    n_exp, cap, d = out_ref.shape
    n = x_ref.shape[0]
    slot = lax.broadcasted_iota(jnp.int32, (cap, n), 0)
    pick = jnp.concatenate([(slot == rank_ref[e:e + 1, :]).astype(F32) for e in range(n_exp)], axis=0)
    f = jnp.dot(pick.T.astype(BF16), out_ref[...].reshape(n_exp * cap, d), preferred_element_type=F32)
    x2_ref[...] = _ln(alpha * x_ref[...] + g2_ref[...] * f) * lg_ref[...] + lb_ref[...]


def _scatter(out, rank, x1, mod4, mod_row, ln_g, ln_b, layer, alpha, *, batch, seq):
    n_exp, _, d = out.shape
    cap = out.shape[1] // batch
    blocks_per_seq = seq // ROW_BLOCK
    return pl.pallas_call(
        functools.partial(_scatter_kernel, alpha=alpha),
        grid=(batch,),
        in_specs=[pl.BlockSpec((n_exp, cap, d), lambda b: (0, b, 0)),
                  pl.BlockSpec((None, n_exp, seq), lambda b: (b, 0, 0)),
                  pl.BlockSpec((seq, d), lambda b: (b, 0)),
                  pl.BlockSpec((None, None, 1, d), lambda b: (mod_row(b * blocks_per_seq), 5, 0, 0)),
                  pl.BlockSpec((None, 1, d), lambda b: (layer, 0, 0)),
                  pl.BlockSpec((None, 1, d), lambda b: (layer, 0, 0))],
        out_specs=pl.BlockSpec((seq, d), lambda b: (b, 0)),
        out_shape=jax.ShapeDtypeStruct((batch * seq, d), F32),
        compiler_params=_cparams(1),
    )(out, rank, x1, mod4, ln_g, ln_b)


def _mixers(h, p, st, layer):
    batch, seq = st["batch"], st["seq"]
    dims = p["dims"]
    hw, kvw, hg, kvg, ssm_w = dims["heads_win"], dims["kv_win"], dims["heads_glob"], dims["kv_glob"], dims["ssm_w"]
    wa = (hw + 2 * kvw) * HEAD_DIM
    wb = (hg + 2 * kvg) * HEAD_DIM
    pa = _matmul(h, p["w_in"], layer, 0, wa)
    pu = _matmul(h, p["w_in"], layer, wa, ssm_w)
    pb = _matmul(h, p["w_in"], layer, wa + ssm_w, wb)
    gates = _matmul(h, p["w_in"], layer, wa + ssm_w + wb, p["w_in"].shape[-1] - wa - ssm_w - wb)
    k_w0, v_w0 = hw * HEAD_DIM, (hw + kvw) * HEAD_DIM
    k_g0, v_g0 = hg * HEAD_DIM, (hg + kvg) * HEAD_DIM
    win = dict(heads=hw, kv=kvw, batch=batch, seq=seq, sink=p["win_sink"])
    glob = dict(heads=hg, kv=kvg, batch=batch, seq=seq, q_norm=p["q_norm"], k_norm=p["k_norm"])
    if st["latent"]:
        o_win = _attention(pa, 0, pa, k_w0, pa, v_w0, k_ctx=st["ck_w"], v_ctx=st["cv_w"],
                           rope=True, band=True, **win)
        o_glob = _attention(pb, 0, pb, k_g0, pb, v_g0, k_ctx=st["ck_g"], v_ctx=st["cv_g"], rope=True, **glob)
        k_g = None
    else:
        o_win = _attention(pa, 0, pa, k_w0, pa, v_w0, **win)
        o_glob, k_g = _attention(pb, 0, pb, k_g0, pb, v_g0, emit_k=True, **glob)
    y_ssm, fin = _s5_mix(pu, 0, ssm_w, *p["s5"], st["s0"], batch=batch, seq=seq)
    o_ssm = _glu(y_ssm, pu, 0, p["ssm_d"], p["ssm_w_glu"], layer)
    y = _merge(o_win, o_ssm, o_glob, gates, p["w_up_win"], p["w_up_ssm"], p["w_up_glob"], layer)
    extras = (pa[:, k_w0:v_w0], pa[:, v_w0:wa], k_g, pb[:, v_g0:v_g0 + kvg * HEAD_DIM], fin)
    return y, extras


def kernel(x_prompt, x_sample, cache_win_k, cache_win_v, cache_glob_k, cache_glob_v, state_ssm_re, state_ssm_im, c, c_ctx, w_mod, b_mod, w_in, win_sink, ssm_a_re, ssm_a_im, ssm_log_dt, ssm_b_re, ssm_b_im, ssm_c_re, ssm_c_im, ssm_d, ssm_w_glu, q_norm, k_norm, w_up_win, w_up_ssm, w_up_glob, w_out, ln1_g, ln1_b, ln2_g, ln2_b, router_w, exp_w_gate, exp_w_up, exp_w_down):
    bp, lp, d = x_prompt.shape
    bs, ls, _ = x_sample.shape
    depth = w_mod.shape[0]
    past = cache_win_k.shape[2]
    kv_win, kv_glob = cache_win_k.shape[3], cache_glob_k.shape[3]
    kvw, kvg = kv_win * HEAD_DIM, kv_glob * HEAD_DIM
    ssm_w = w_up_ssm.shape[1]
    n_exp = router_w.shape[-1]
    p_n = ssm_a_re.shape[-1]
    alpha = (2.0 * depth) ** 0.25
    dims = dict(heads_win=w_up_win.shape[1] // HEAD_DIM, kv_win=kv_win,
                heads_glob=w_up_glob.shape[1] // HEAD_DIM, kv_glob=kv_glob, ssm_w=ssm_w)
    assert lp % ROW_BLOCK == 0 and ls % ROW_BLOCK == 0 and 1 + bs <= 8

    cond = jnp.zeros((8, d), F32).at[0].set(c_ctx).at[1:1 + bs].set(c)
    sample_blocks = ls // ROW_BLOCK
    streams = [
        dict(batch=bp, seq=lp, latent=False, mod_row=lambda i: 0,
             s0=jnp.zeros((ssm_w // LANES, bp, 4 * S5_OCT * p_n), F32)),
        dict(batch=bs, seq=ls, latent=True, mod_row=lambda i: 1 + i // sample_blocks),
    ]
    xs = [x_prompt.reshape(bp * lp, d), x_sample.reshape(bs * ls, d)]
    new = {k: [] for k in ("wk", "wv", "gk", "gv", "sre", "sim")}

    lay = _s5_layouts(ssm_log_dt, ssm_a_re, ssm_a_im, ssm_b_re, ssm_b_im, ssm_c_re, ssm_c_im)
    p = dict(w_in=w_in, dims=dims, ssm_d=ssm_d.reshape(depth, 1, ssm_w),
             ssm_w_glu=ssm_w_glu.astype(BF16), w_up_win=w_up_win.astype(BF16),
             w_up_ssm=w_up_ssm.astype(BF16), w_up_glob=w_up_glob.astype(BF16))
    w_out_bf = w_out.astype(BF16)
    router_pad = jnp.pad(router_w.astype(F32), ((0, 0), (0, 0), (0, LANES - n_exp)))
    b_mod3 = b_mod.reshape(depth, 1, 6 * d)
    ln1_g3, ln1_b3 = ln1_g.reshape(depth, 1, d), ln1_b.reshape(depth, 1, d)
    ln2_g3, ln2_b3 = ln2_g.reshape(depth, 1, d), ln2_b.reshape(depth, 1, d)

    for l in range(depth):
        p.update(win_sink=win_sink[l], q_norm=q_norm[l], k_norm=k_norm[l], s5=_s5_prep(lay, l))
        mod = _matmul(cond, w_mod, l, 0, 6 * d, bias=b_mod3, act="silu", tn=1024)
        mod4 = mod.reshape(8, 6, 1, d)
        streams[1].update(
            ck_w=cache_win_k[:, l].reshape(bs, past, kvw), cv_w=cache_win_v[:, l].reshape(bs, past, kvw),
            ck_g=cache_glob_k[:, l].reshape(bs, past, kvg), cv_g=cache_glob_v[:, l].reshape(bs, past, kvg),
            s0=_state_to_lanes(state_ssm_re[:, l], state_ssm_im[:, l]))

        routed = []
        for si, st in enumerate(streams):
            x = xs[si]
            h = _ln_mod(x, mod4, st["mod_row"])
            y, extras = _mixers(h, p, st, l)
            x1, h2, logits = _outproj(y, x, w_out_bf, mod4, st["mod_row"], ln1_g3, ln1_b3, router_pad,
                                      l, alpha, _tile(st["seq"], 512))
            xsel, vals, rank = _route(logits, h2, batch=st["batch"], seq=st["seq"], n_exp=n_exp)
            routed.append((x1, xsel, vals, rank))
            if not st["latent"]:
                k_w, v_w, k_g, v_g, fin = extras
                new["wk"].append(k_w.reshape(bp, lp, -1, HEAD_DIM))
                new["wv"].append(v_w.reshape(bp, lp, -1, HEAD_DIM))
                new["gk"].append(k_g.reshape(bp, lp, -1, HEAD_DIM))
                new["gv"].append(v_g.reshape(bp, lp, -1, HEAD_DIM))
                s_re, s_im = _lanes_to_state(fin, p_n)
                new["sre"].append(s_re)
                new["sim"].append(s_im)
        outs = _ffn(routed[0][1], routed[1][1], routed[0][2], routed[1][2],
                    exp_w_gate, exp_w_up, exp_w_down, l)
        for si, st in enumerate(streams):
            xs[si] = _scatter(outs[si], routed[si][3], routed[si][0], mod4, st["mod_row"],
                              ln2_g3, ln2_b3, l, alpha, batch=st["batch"], seq=st["seq"])

    return (xs[0].reshape(bp, lp, d), xs[1].reshape(bs, ls, d),
            jnp.stack(new["wk"], axis=1), jnp.stack(new["wv"], axis=1),
            jnp.stack(new["gk"], axis=1), jnp.stack(new["gv"], axis=1),
            jnp.stack(new["sre"], axis=1), jnp.stack(new["sim"], axis=1))
```

```python
import functools

import numpy as np
import jax
import jax.numpy as jnp
from jax import lax
from jax.experimental import pallas as pl
from jax.experimental.pallas import tpu as pltpu

F32 = jnp.float32
BF16 = jnp.bfloat16
HIGHEST = lax.Precision.HIGHEST

HEAD_DIM = 128
LANES = 128
GRID_W = 64
WINDOW = 128
EC_FACTOR = 2
ROPE_THETA = 10000.0
EPS = 1e-6
NEG_INF = -1e30
ATTN_SCALE = HEAD_DIM ** -0.5
S5_CHUNK = 16
S5_OCT = 8
ROW_BLOCK = 256
VMEM_LIMIT = 60 * 1024 * 1024
RESIDENT = pl.Buffered(1)


def _cparams(n_axes):
    return pltpu.CompilerParams(dimension_semantics=("arbitrary",) * n_axes,
                                vmem_limit_bytes=VMEM_LIMIT)


def _tile(dim, pref):
    return pref if dim % pref == 0 else dim


def _ln(x):
    mu = jnp.mean(x, axis=-1, keepdims=True)
    xc = x - mu
    var = jnp.mean(xc * xc, axis=-1, keepdims=True)
    return xc * lax.rsqrt(var + EPS)


def _split_bf16(x):
    hi = x.astype(BF16)
    return hi, (x - hi.astype(F32)).astype(BF16)


def _dot3(a_hi, a_lo, b_hi, b_lo, dims=(((1,), (0,)), ((), ()))):
    def d(x, y):
        return lax.dot_general(x, y, dims, preferred_element_type=F32)
    return d(a_hi, b_hi) + d(a_lo, b_hi) + d(a_hi, b_lo)


def _dot3_nt(a, b):
    return _dot3(*_split_bf16(a), *_split_bf16(b), dims=(((1,), (1,)), ((), ())))


def _mm_kernel(x_ref, w_ref, *rest, act, has_bias):
    if has_bias:
        b_ref, o_ref, wbf_ref = rest
    else:
        o_ref, wbf_ref = rest

    @pl.when(pl.program_id(1) == 0)
    def _():
        wbf_ref[...] = w_ref[...].astype(BF16)

    x = x_ref[...]
    if act == "silu":
        x = jax.nn.silu(x.astype(F32))
    acc = jnp.dot(x.astype(BF16), wbf_ref[...], preferred_element_type=F32)
    if has_bias:
        acc = acc + b_ref[...]
    o_ref[...] = acc.astype(o_ref.dtype)


def _matmul(x, w, layer, col0, ncols, *, bias=None, act=None, out_dtype=F32, tm=1024, tn=512):
    m, k = x.shape
    tm = _tile(m, tm)
    tn = next(t for t in (tn, 512, 256, LANES) if ncols % t == 0 and col0 % t == 0)
    c0 = col0 // tn
    in_specs = [pl.BlockSpec((tm, k), lambda n, i: (i, 0)),
                pl.BlockSpec((None, k, tn), lambda n, i: (layer, 0, c0 + n))]
    args = [x, w]
    if bias is not None:
        in_specs.append(pl.BlockSpec((None, 1, tn), lambda n, i: (layer, 0, c0 + n)))
        args.append(bias)
    return pl.pallas_call(
        functools.partial(_mm_kernel, act=act, has_bias=bias is not None),
        grid=(ncols // tn, m // tm),
        in_specs=in_specs,
        out_specs=pl.BlockSpec((tm, tn), lambda n, i: (i, n)),
        out_shape=jax.ShapeDtypeStruct((m, ncols), out_dtype),
        scratch_shapes=[pltpu.VMEM((k, tn), BF16)],
        compiler_params=_cparams(2),
    )(*args)


def _ln_mod_kernel(x_ref, sc_ref, sh_ref, h_ref):
    h_ref[...] = (_ln(x_ref[...]) * (1.0 + sc_ref[...]) + sh_ref[...]).astype(h_ref.dtype)


def _mod_spec(d, chunk, mod_row, rows_per_step=ROW_BLOCK):
    scale = rows_per_step // ROW_BLOCK
    return pl.BlockSpec((None, None, 1, d), lambda i: (mod_row(i * scale), chunk, 0, 0))


def _ln_mod(x, mod4, mod_row):
    t, d = x.shape
    return pl.pallas_call(
        _ln_mod_kernel,
        grid=(t // ROW_BLOCK,),
        in_specs=[pl.BlockSpec((ROW_BLOCK, d), lambda i: (i, 0)),
                  _mod_spec(d, 1, mod_row), _mod_spec(d, 0, mod_row)],
        out_specs=pl.BlockSpec((ROW_BLOCK, d), lambda i: (i, 0)),
        out_shape=jax.ShapeDtypeStruct((t, d), BF16),
        compiler_params=_cparams(1),
    )(x, mod4, mod4)


def _rope_tables(seq_len):
    half = HEAD_DIM // 4
    inv = ROPE_THETA ** (-np.arange(half, dtype=np.float64) / half)
    tok = np.arange(seq_len)
    ang_r = (tok // GRID_W)[:, None] * inv[None, :]
    ang_c = (tok % GRID_W)[:, None] * inv[None, :]
    cos = np.concatenate([np.cos(ang_r), np.cos(ang_r), np.cos(ang_c), np.cos(ang_c)], axis=-1)
    sin = np.concatenate([-np.sin(ang_r), np.sin(ang_r), -np.sin(ang_c), np.sin(ang_c)], axis=-1)
    return jnp.asarray(cos, F32), jnp.asarray(sin, F32)


def _rope(x, cos, sin_signed):
    lane = lax.broadcasted_iota(jnp.int32, x.shape, 1)
    swapped = jnp.where((lane % 64) < 32, pltpu.roll(x, 96, 1), pltpu.roll(x, 32, 1))
    return x * cos + swapped * sin_signed


def _rms(x, w):
    return x * lax.rsqrt(jnp.mean(x * x, axis=-1, keepdims=True) + EPS) * w


def _attn_kernel(*refs, heads, kv, tq, rope, norm, sink, ctx, band, emit_k):
    it = iter(refs)
    q_ref, k_ref, v_ref = next(it), next(it), next(it)
    kc_ref = vc_ref = sink_ref = qn_ref = kn_ref = cq_ref = sq_ref = ck_ref = sk_ref = None
    if ctx:
        kc_ref, vc_ref = next(it), next(it)
    if sink:
        sink_ref = next(it)
    if norm:
        qn_ref, kn_ref = next(it), next(it)
    if rope:
        cq_ref, sq_ref, ck_ref, sk_ref = next(it), next(it), next(it), next(it)
    o_ref = next(it)
    ko_ref = next(it) if emit_k else None
    kp_ref, vp_ref = next(it), next(it)

    qi = pl.program_id(1)
    rep = heads // kv

    @pl.when(qi == 0)
    def _():
        for g in range(kv):
            sl = slice(g * HEAD_DIM, (g + 1) * HEAD_DIM)
            kg = k_ref[:, sl]
            if norm:
                kg = _rms(kg, kn_ref[...])
            if rope:
                kg = _rope(kg, ck_ref[...], sk_ref[...])
            if emit_k:
                ko_ref[:, sl] = kg
            kp_ref[:, sl] = kg.astype(BF16)
        vp_ref[...] = v_ref[...].astype(BF16)

    nt = (((1,), (1,)), ((), ()))
    seq = kp_ref.shape[0]
    if band:
        span = min(seq, tq + 2 * max(WINDOW, tq))
        k0 = pl.multiple_of(jnp.clip(qi * tq - max(WINDOW, tq), 0, seq - span), tq)
        rows = pl.ds(k0, span)
        qpos = qi * tq + lax.broadcasted_iota(jnp.int32, (tq, span), 0)
        kpos = k0 + lax.broadcasted_iota(jnp.int32, (tq, span), 1)
        in_band = jnp.abs(qpos - kpos) <= WINDOW
    else:
        rows = slice(None)
    for h in range(heads):
        g = h // rep
        gs = slice(g * HEAD_DIM, (g + 1) * HEAD_DIM)
        qh = q_ref[:, h * HEAD_DIM:(h + 1) * HEAD_DIM]
        if norm:
            qh = _rms(qh, qn_ref[...])
        if rope:
            qh = _rope(qh, cq_ref[...], sq_ref[...])
        qh = (qh * ATTN_SCALE).astype(BF16)
        s = lax.dot_general(qh, kp_ref[rows, gs], nt, preferred_element_type=F32)
        if band:
            s = jnp.where(in_band, s, NEG_INF)
        m = jnp.max(s, axis=-1, keepdims=True)
        if ctx:
            sc = lax.dot_general(qh, kc_ref[:, gs].astype(BF16), nt, preferred_element_type=F32)
            m = jnp.maximum(m, jnp.max(sc, axis=-1, keepdims=True))
        if sink:
            m = jnp.maximum(m, sink_ref[h])
        e = jnp.exp(s - m)
        den = jnp.sum(e, axis=-1, keepdims=True)
        o = jnp.dot(e.astype(BF16), vp_ref[rows, gs], preferred_element_type=F32)
        if ctx:
            ec = jnp.exp(sc - m)
            den = den + jnp.sum(ec, axis=-1, keepdims=True)
            o = o + jnp.dot(ec.astype(BF16), vc_ref[:, gs].astype(BF16), preferred_element_type=F32)
        if sink:
            den = den + jnp.exp(sink_ref[h] - m)
        o_ref[:, h * HEAD_DIM:(h + 1) * HEAD_DIM] = (o / den).astype(o_ref.dtype)


def _attention(qa, q0, ka, k0, va, v0, *, heads, kv, batch, seq, k_ctx=None, v_ctx=None, sink=None,
               q_norm=None, k_norm=None, rope=False, band=False, emit_k=False):
    t = qa.shape[0]
    qw, kw = heads * HEAD_DIM, kv * HEAD_DIM
    assert q0 % qw == 0 and k0 % kw == 0 and v0 % kw == 0
    qc, kc, vc = q0 // qw, k0 // kw, v0 // kw
    tq = _tile(seq, 256)
    nq = seq // tq
    ctx, has_sink, norm = k_ctx is not None, sink is not None, q_norm is not None
    in_specs = [pl.BlockSpec((tq, qw), lambda b, i: (b * nq + i, qc)),
                pl.BlockSpec((seq, kw), lambda b, i: (b, kc)),
                pl.BlockSpec((seq, kw), lambda b, i: (b, vc))]
    args = [qa, ka, va]
    if ctx:
        lc = k_ctx.shape[1]
        in_specs += [pl.BlockSpec((None, lc, kw), lambda b, i: (b, 0, 0))] * 2
        args += [k_ctx, v_ctx]
    if has_sink:
        in_specs.append(pl.BlockSpec(memory_space=pltpu.SMEM))
        args.append(sink)
    if norm:
        in_specs += [pl.BlockSpec((1, HEAD_DIM), lambda b, i: (0, 0))] * 2
        args += [q_norm.reshape(1, HEAD_DIM), k_norm.reshape(1, HEAD_DIM)]
    if rope:
        cos, sin = _rope_tables(seq)
        in_specs += [pl.BlockSpec((tq, HEAD_DIM), lambda b, i: (i, 0))] * 2
        in_specs += [pl.BlockSpec((seq, HEAD_DIM), lambda b, i: (0, 0))] * 2
        args += [cos, sin, cos, sin]
    out_specs = [pl.BlockSpec((tq, qw), lambda b, i: (b * nq + i, 0))]
    out_shape = [jax.ShapeDtypeStruct((t, qw), BF16)]
    if emit_k:
        out_specs.append(pl.BlockSpec((seq, kw), lambda b, i: (b, 0)))
        out_shape.append(jax.ShapeDtypeStruct((t, kw), F32))
    res = pl.pallas_call(
        functools.partial(_attn_kernel, heads=heads, kv=kv, tq=tq, rope=rope, norm=norm,
                          sink=has_sink, ctx=ctx, band=band, emit_k=emit_k),
        grid=(batch, nq),
        in_specs=in_specs,
        out_specs=out_specs,
        out_shape=out_shape,
        scratch_shapes=[pltpu.VMEM((seq, kw), BF16), pltpu.VMEM((seq, kw), BF16)],
        compiler_params=_cparams(2),
    )(*args)
    return res if emit_k else res[0]


def _s5_layouts(log_dt, a_re, a_im, b_re, b_im, c_re, c_im):
    dep, _, g_n, p_n = a_re.shape
    gc = b_re.shape[-1]
    no = g_n // S5_OCT
    sw = S5_OCT * p_n
    ldt = jnp.broadcast_to(log_dt[..., None], a_re.shape)
    prm = jnp.stack([ldt, a_re, a_im], axis=2).astype(F32)
    prow = jnp.pad(prm.reshape(dep, 2, 3, no, sw).transpose(0, 1, 3, 2, 4),
                   ((0, 0),) * 3 + ((0, 5), (0, 0)))
    pcmp = jnp.broadcast_to(prm[:, :, :, :, None, :], (dep, 2, 3, g_n, gc, p_n))
    pcmp = pcmp.reshape(dep, 2, 3, no, S5_OCT * gc, p_n).transpose(0, 1, 3, 2, 4, 5)

    def b_views(b):
        bt = jnp.swapaxes(b.astype(F32), -1, -2).reshape(dep, 2, no, S5_OCT * gc, p_n)
        return bt, jnp.tile(bt, (1, 1, 1, 1, S5_OCT))

    def c_views(c):
        cc = c.astype(F32).reshape(dep, 2, no, S5_OCT * gc, p_n)
        ct = jnp.swapaxes(c.astype(F32), -1, -2).reshape(dep, 2, no, sw, gc)
        return cc, jnp.tile(ct, (1, 1, 1, 1, S5_OCT))

    bc_re, bt_re = b_views(b_re)
    bc_im, bt_im = b_views(b_im)
    cc_re, ct_re = c_views(c_re)
    cc_im, ct_im = c_views(c_im)
    return dict(prow=prow, pcmp=pcmp, bc_re=bc_re, bc_im=bc_im, bt_re=bt_re, bt_im=bt_im,
                cc_re=cc_re, cc_im=cc_im, ct_re=ct_re, ct_im=ct_im)


def _lam_powers(ldt, ar, ai, k):
    dt = jnp.exp(ldt)
    mag = jnp.exp(k * (dt * ar))
    return mag * jnp.cos(k * (dt * ai)), mag * jnp.sin(k * (dt * ai))


def _bbar_coef(ldt, ar, ai):
    lam_re, lam_im = _lam_powers(ldt, ar, ai, 1.0)
    den = ar * ar + ai * ai
    return ((lam_re - 1.0) * ar + lam_im * ai) / den, (lam_im * ar - (lam_re - 1.0) * ai) / den


def _same_group(shape, rows_per_group, cols_per_group):
    r = lax.broadcasted_iota(jnp.int32, shape, 0) // rows_per_group
    c = lax.broadcasted_iota(jnp.int32, shape, 1) // cols_per_group
    return r == c


def _s5_prep_w_kernel(prow_ref, btr_ref, bti_ref, w_ref, lam_ref):
    tc = S5_CHUNK
    rows, sw = btr_ref.shape[1], btr_ref.shape[2]
    mask = _same_group((rows, sw), rows // S5_OCT, sw // S5_OCT)
    k = lax.broadcasted_iota(jnp.int32, (24, sw), 0).astype(F32)
    for d in range(2):
        ldt, ar, ai = prow_ref[d, 0:1, :], prow_ref[d, 1:2, :], prow_ref[d, 2:3, :]
        pw_re, pw_im = _lam_powers(ldt, ar, ai, k)
        co_re, co_im = _bbar_coef(ldt, ar, ai)
        bb_re = jnp.where(mask, co_re * btr_ref[d] - co_im * bti_ref[d], 0.0)
        bb_im = jnp.where(mask, co_re * bti_ref[d] + co_im * btr_ref[d], 0.0)
        lam_ref[2 * d:2 * d + 1, :] = pw_re[tc:tc + 1]
        lam_ref[2 * d + 1:2 * d + 2, :] = pw_im[tc:tc + 1]
        for j in range(tc):
            kk = tc - 1 - j if d == 0 else j
            pr, pi = pw_re[kk:kk + 1], pw_im[kk:kk + 1]
            rs = slice(j * rows, (j + 1) * rows)
            w_ref[rs, (2 * d) * sw:(2 * d + 1) * sw] = (pr * bb_re - pi * bb_im).astype(w_ref.dtype)
            w_ref[rs, (2 * d + 1) * sw:(2 * d + 2) * sw] = (pr * bb_im + pi * bb_re).astype(w_ref.dtype)


def _row_to_col(row, eye):
    return jnp.sum(jnp.where(eye, row, 0.0), axis=1, keepdims=True)


def _s5_prep_z_kernel(prow_ref, ctr_ref, cti_ref, z_ref):
    tc = S5_CHUNK
    sw, cols = ctr_ref.shape[1], ctr_ref.shape[2]
    mask = _same_group((sw, cols), sw // S5_OCT, cols // S5_OCT)
    eye = _same_group((sw, sw), 1, 1)
    for d in range(2):
        lam_re, lam_im = _lam_powers(prow_ref[d, 0:1, :], prow_ref[d, 1:2, :], prow_ref[d, 2:3, :], 1.0)
        lr = jnp.broadcast_to(_row_to_col(lam_re, eye), (sw, cols))
        li = jnp.broadcast_to(_row_to_col(lam_im, eye), (sw, cols))
        c_re = jnp.where(mask, ctr_ref[d], 0.0)
        c_im = jnp.where(mask, cti_ref[d], 0.0)
        z_re, z_im = c_re * lr - c_im * li, c_re * li + c_im * lr
        for step in range(tc):
            t = step if d == 0 else tc - 1 - step
            cs = slice(t * cols, (t + 1) * cols)
            z_ref[(2 * d) * sw:(2 * d + 1) * sw, cs] = z_re.astype(z_ref.dtype)
            z_ref[(2 * d + 1) * sw:(2 * d + 2) * sw, cs] = (-z_im).astype(z_ref.dtype)
            z_re, z_im = z_re * lr - z_im * li, z_re * li + z_im * lr


def _s5_prep_m_kernel(pcmp_ref, bcr_ref, bci_ref, ccr_ref, cci_ref, m_ref, xr_ref, xi_ref, taps_ref):
    tc = S5_CHUNK
    rows = bcr_ref.shape[1]
    mask = _same_group((rows, rows), rows // S5_OCT, rows // S5_OCT)
    for d in range(2):
        ldt, ar, ai = pcmp_ref[d, 0], pcmp_ref[d, 1], pcmp_ref[d, 2]
        lam_re, lam_im = _lam_powers(ldt, ar, ai, 1.0)
        co_re, co_im = _bbar_coef(ldt, ar, ai)
        x_re = co_re * bcr_ref[d] - co_im * bci_ref[d]
        x_im = co_re * bci_ref[d] + co_im * bcr_ref[d]
        for kk in range(tc):
            xr_ref[kk * rows:(kk + 1) * rows, :] = x_re
            xi_ref[kk * rows:(kk + 1) * rows, :] = x_im
            x_re, x_im = x_re * lam_re - x_im * lam_im, x_re * lam_im + x_im * lam_re
        taps = _dot3_nt(xr_ref[...], ccr_ref[d]) - _dot3_nt(xi_ref[...], cci_ref[d])
        for kk in range(tc):
            tile = jnp.where(mask, taps[kk * rows:(kk + 1) * rows, :], 0.0)
            col = (tc - 1 + kk) if d == 0 else (tc - 1 - kk)
            cs = slice(col * rows, (col + 1) * rows)
            if d == 1 and kk == 0:
                taps_ref[:, cs] += tile
            else:
                taps_ref[:, cs] = tile
    for j in range(tc):
        m_ref[j * rows:(j + 1) * rows, :] = taps_ref[:, (tc - 1 - j) * rows:(2 * tc - 1 - j) * rows].astype(m_ref.dtype)


def _s5_prep(lay, layer):
    _, _, no, rows, sw = lay["bt_re"].shape
    p_n = lay["bc_re"].shape[-1]
    tc = S5_CHUNK
    kw = tc * rows

    def per_octet(*tail):
        nd = len(tail)
        return pl.BlockSpec((None, 2, None) + tail, lambda o: (layer, 0, o) + (0,) * nd)

    def out(r, c):
        return pl.BlockSpec((None, r, c), lambda o: (o, 0, 0))

    w_o, lam_o = pl.pallas_call(
        _s5_prep_w_kernel, grid=(no,),
        in_specs=[per_octet(8, sw), per_octet(rows, sw), per_octet(rows, sw)],
        out_specs=[out(kw, 4 * sw), out(4, sw)],
        out_shape=[jax.ShapeDtypeStruct((no, kw, 4 * sw), BF16), jax.ShapeDtypeStruct((no, 4, sw), F32)],
        compiler_params=_cparams(1),
    )(lay["prow"], lay["bt_re"], lay["bt_im"])
    z_o = pl.pallas_call(
        _s5_prep_z_kernel, grid=(no,),
        in_specs=[per_octet(8, sw), per_octet(sw, rows), per_octet(sw, rows)],
        out_specs=out(4 * sw, kw),
        out_shape=jax.ShapeDtypeStruct((no, 4 * sw, kw), BF16),
        compiler_params=_cparams(1),
    )(lay["prow"], lay["ct_re"], lay["ct_im"])
    m_o = pl.pallas_call(
        _s5_prep_m_kernel, grid=(no,),
        in_specs=[per_octet(3, rows, p_n)] + [per_octet(rows, p_n)] * 4,
        out_specs=out(kw, kw),
        out_shape=jax.ShapeDtypeStruct((no, kw, kw), BF16),
        scratch_shapes=[pltpu.VMEM((kw, p_n), F32), pltpu.VMEM((kw, p_n), F32),
                        pltpu.VMEM((rows, (2 * tc - 1) * rows), F32)],
        compiler_params=_cparams(1),
    )(lay["pcmp"], lay["bc_re"], lay["bc_im"], lay["cc_re"], lay["cc_im"])
    return m_o, w_o, z_o, lam_o


def _s5_state_kernel(u_ref, w_ref, lam_ref, s0_ref, sp_ref, fin_ref, ubf_ref, vs_ref, st_ref,
                     *, batch, nchunk):
    tc = S5_CHUNK
    for t in range(tc):
        ubf_ref[:, t * LANES:(t + 1) * LANES] = u_ref[:, t, :].astype(BF16)
    q = st_ref.shape[-1] // 4
    vs_ref[...] = jnp.dot(ubf_ref[...], w_ref[...],
                          preferred_element_type=F32).reshape(batch, nchunk, 4 * q)
    fa, fb = lam_ref[0:1, :], lam_ref[1:2, :]
    ba, bb = lam_ref[2:3, :], lam_ref[3:4, :]
    s_re, s_im = s0_ref[:, 0:q], s0_ref[:, q:2 * q]
    for c in range(nchunk):
        st_ref[:, c, 0:q] = s_re
        st_ref[:, c, q:2 * q] = s_im
        v_re, v_im = vs_ref[:, c, 0:q], vs_ref[:, c, q:2 * q]
        s_re, s_im = fa * s_re - fb * s_im + v_re, fa * s_im + fb * s_re + v_im
    fin_ref[:, 0:q] = s_re
    fin_ref[:, q:2 * q] = s_im
    s_re, s_im = s0_ref[:, 2 * q:3 * q], s0_ref[:, 3 * q:4 * q]
    for c in range(nchunk - 1, -1, -1):
        st_ref[:, c, 2 * q:3 * q] = s_re
        st_ref[:, c, 3 * q:4 * q] = s_im
        v_re, v_im = vs_ref[:, c, 2 * q:3 * q], vs_ref[:, c, 3 * q:4 * q]
        s_re, s_im = ba * s_re - bb * s_im + v_re, ba * s_im + bb * s_re + v_im
    fin_ref[:, 2 * q:3 * q] = s_re
    fin_ref[:, 3 * q:4 * q] = s_im
    sp_ref[...] = st_ref[...].reshape(batch * nchunk, 4 * q).astype(sp_ref.dtype)


def _s5_out_kernel(u_ref, sp_ref, m_ref, z_ref, y_ref, ubf_ref):
    tc = S5_CHUNK
    for t in range(tc):
        ubf_ref[:, t * LANES:(t + 1) * LANES] = u_ref[:, t, :].astype(BF16)
    y = (jnp.dot(ubf_ref[...], m_ref[...], preferred_element_type=F32)
         + jnp.dot(sp_ref[...], z_ref[...], preferred_element_type=F32))
    for t in range(tc):
        y_ref[:, t, :] = y[:, t * LANES:(t + 1) * LANES]


def _s5_mix(ua, u0, width, m_o, w_o, z_o, lam_o, s0, *, batch, seq):
    t, full = ua.shape
    tc = S5_CHUNK
    no = width // LANES
    assert u0 % LANES == 0
    uc = u0 // LANES
    r = t // tc
    nchunk = seq // tc
    sw = w_o.shape[-1]
    kw = tc * LANES
    u3 = ua.reshape(r, tc, full)
    sp, fin = pl.pallas_call(
        functools.partial(_s5_state_kernel, batch=batch, nchunk=nchunk),
        grid=(no,),
        in_specs=[pl.BlockSpec((r, tc, LANES), lambda o: (0, 0, uc + o)),
                  pl.BlockSpec((None, kw, sw), lambda o: (o, 0, 0)),
                  pl.BlockSpec((None, 4, sw // 4), lambda o: (o, 0, 0)),
                  pl.BlockSpec((None, batch, sw), lambda o: (o, 0, 0))],
        out_specs=[pl.BlockSpec((None, r, sw), lambda o: (o, 0, 0)),
                   pl.BlockSpec((None, batch, sw), lambda o: (o, 0, 0))],
        out_shape=[jax.ShapeDtypeStruct((no, r, sw), BF16),
                   jax.ShapeDtypeStruct((no, batch, sw), F32)],
        scratch_shapes=[pltpu.VMEM((r, kw), BF16),
                        pltpu.VMEM((batch, nchunk, sw), F32),
                        pltpu.VMEM((batch, nchunk, sw), F32)],
        compiler_params=_cparams(1),
    )(u3, w_o, lam_o, s0)
    tr = _tile(r, 256)
    y3 = pl.pallas_call(
        _s5_out_kernel,
        grid=(no, r // tr),
        in_specs=[pl.BlockSpec((tr, tc, LANES), lambda o, i: (i, 0, uc + o)),
                  pl.BlockSpec((None, tr, sw), lambda o, i: (o, i, 0)),
                  pl.BlockSpec((None, kw, kw), lambda o, i: (o, 0, 0)),
                  pl.BlockSpec((None, sw, kw), lambda o, i: (o, 0, 0))],
        out_specs=pl.BlockSpec((tr, tc, LANES), lambda o, i: (i, 0, o)),
        out_shape=jax.ShapeDtypeStruct((r, tc, width), F32),
        scratch_shapes=[pltpu.VMEM((tr, kw), BF16)],
        compiler_params=_cparams(2),
    )(u3, sp, m_o, z_o)
    return y3.reshape(t, width), fin


def _state_to_lanes(s_re, s_im):
    b, _, g_n, p_n = s_re.shape
    no = g_n // S5_OCT
    parts = jnp.stack([s_re[:, 0], s_im[:, 0], s_re[:, 1], s_im[:, 1]], axis=1)
    parts = parts.reshape(b, 4, no, S5_OCT * p_n)
    return jnp.transpose(parts, (2, 0, 1, 3)).reshape(no, b, 4 * S5_OCT * p_n).astype(F32)


def _lanes_to_state(fin, p_n):
    no, b, _ = fin.shape
    parts = jnp.transpose(fin.reshape(no, b, 4, S5_OCT, p_n), (1, 2, 0, 3, 4)).reshape(b, 4, no * S5_OCT, p_n)
    return jnp.stack([parts[:, 0], parts[:, 2]], axis=1), jnp.stack([parts[:, 1], parts[:, 3]], axis=1)


def _glu_kernel(y_ref, u_ref, d_ref, w_ref, o_ref):
    z = jax.nn.gelu(y_ref[...] + d_ref[...] * u_ref[...]).astype(BF16)
    g = jnp.dot(z, w_ref[...], preferred_element_type=F32)
    w = o_ref.shape[-1]
    o_ref[...] = (g[:, :w] * jax.nn.sigmoid(g[:, w:])).astype(o_ref.dtype)


def _glu(y, ua, u0, d, w_glu, layer):
    t, width = y.shape
    assert u0 % width == 0
    uc = u0 // width
    tm = _tile(t, 512)
    return pl.pallas_call(
        _glu_kernel,
        grid=(t // tm,),
        in_specs=[pl.BlockSpec((tm, width), lambda i: (i, 0)),
                  pl.BlockSpec((tm, width), lambda i: (i, uc)),
                  pl.BlockSpec((None, 1, width), lambda i: (layer, 0, 0)),
                  pl.BlockSpec((None, width, 2 * width), lambda i: (layer, 0, 0), pipeline_mode=RESIDENT)],
        out_specs=pl.BlockSpec((tm, width), lambda i: (i, 0)),
        out_shape=jax.ShapeDtypeStruct((t, width), BF16),
        compiler_params=_cparams(1),
    )(y, ua, d, w_glu)


def _merge_kernel(ow_ref, os_ref, og_ref, gw_ref, gs_ref, gg_ref, x_ref, g1_ref,
                  ww_ref, ws_ref, wg_ref, wo_ref, r_ref, *, alpha):
    y = (jax.nn.sigmoid(gw_ref[...]) * jnp.dot(ow_ref[...], ww_ref[...], preferred_element_type=F32)
         + jax.nn.sigmoid(gs_ref[...]) * jnp.dot(os_ref[...], ws_ref[...], preferred_element_type=F32)
         + jax.nn.sigmoid(gg_ref[...]) * jnp.dot(og_ref[...], wg_ref[...], preferred_element_type=F32))
    o = jnp.dot(y.astype(BF16), wo_ref[...], preferred_element_type=F32)
    r_ref[...] = alpha * x_ref[...] + g1_ref[...] * o


def _merge(o_win, o_ssm, o_glob, gates, x, mod4, mod_row, w_win, w_ssm, w_glob, w_out, layer, alpha):
    t, d = x.shape
    tm = ROW_BLOCK

    def rows(w):
        return pl.BlockSpec((tm, w), lambda i: (i, 0))

    def whole(a):
        return pl.BlockSpec((None,) + a.shape[1:], lambda i: (layer, 0, 0), pipeline_mode=RESIDENT)

    return pl.pallas_call(
        functools.partial(_merge_kernel, alpha=alpha),
        grid=(t // tm,),
        in_specs=[rows(o_win.shape[1]), rows(o_ssm.shape[1]), rows(o_glob.shape[1]),
                  pl.BlockSpec((tm, d), lambda i: (i, 0)),
                  pl.BlockSpec((tm, d), lambda i: (i, 1)),
                  pl.BlockSpec((tm, d), lambda i: (i, 2)),
                  rows(d), _mod_spec(d, 2, mod_row),
                  whole(w_win), whole(w_ssm), whole(w_glob), whole(w_out)],
        out_specs=rows(d),
        out_shape=jax.ShapeDtypeStruct((t, d), F32),
        compiler_params=_cparams(1),
    )(o_win, o_ssm, o_glob, gates, gates, gates, x, mod4, w_win, w_ssm, w_glob, w_out)


def _post_ln_kernel(r_ref, sc2_ref, sh2_ref, lg_ref, lb_ref, rwh_ref, rwl_ref, x1_ref, h2_ref, logit_ref):
    x1 = _ln(r_ref[...]) * lg_ref[...] + lb_ref[...]
    x1_ref[...] = x1
    h2 = _ln(x1) * (1.0 + sc2_ref[...]) + sh2_ref[...]
    hi, lo = _split_bf16(h2)
    h2_ref[...] = hi
    logit_ref[...] = _dot3(hi, lo, rwh_ref[...], rwl_ref[...])


def _post_ln(r, mod4, mod_row, ln_g, ln_b, router_hi, router_lo, layer, tm):
    t, d = r.shape

    def rows(w):
        return pl.BlockSpec((tm, w), lambda i: (i, 0))

    def vec():
        return pl.BlockSpec((None, 1, d), lambda i: (layer, 0, 0))

    def router():
        return pl.BlockSpec((None, d, LANES), lambda i: (layer, 0, 0), pipeline_mode=RESIDENT)

    return pl.pallas_call(
        _post_ln_kernel,
        grid=(t // tm,),
        in_specs=[rows(d), _mod_spec(d, 4, mod_row, tm), _mod_spec(d, 3, mod_row, tm), vec(), vec(),
                  router(), router()],
        out_specs=[rows(d), rows(d), rows(LANES)],
        out_shape=[jax.ShapeDtypeStruct((t, d), F32), jax.ShapeDtypeStruct((t, d), BF16),
                   jax.ShapeDtypeStruct((t, LANES), F32)],
        compiler_params=_cparams(1),
    )(r, mod4, mod4, ln_g, ln_b, router_hi, router_lo)


def _route_kernel(logit_ref, h_ref, xsel_ref, vals_ref, rank_ref, *, n_exp, cap):
    n = logit_ref.shape[0]
    lg = logit_ref[...]
    col = lax.broadcasted_iota(jnp.int32, lg.shape, 1)
    lg = jnp.where(col < n_exp, lg, -jnp.inf)
    ex = jnp.exp(lg - jnp.max(lg, axis=-1, keepdims=True))
    aff = ex / jnp.sum(ex, axis=-1, keepdims=True)
    aff_t = aff.T
    jj = lax.broadcasted_iota(jnp.int32, (n, n), 0)
    ii = lax.broadcasted_iota(jnp.int32, (n, n), 1)
    slot = lax.broadcasted_iota(jnp.int32, (cap, n), 0)
    h = h_ref[...]
    for e in range(n_exp):
        a_row = aff_t[e:e + 1, :]
        a_col = aff[:, e:e + 1]
        beats = (a_col > a_row) | ((a_col == a_row) & (jj < ii))
        rank = jnp.sum(beats.astype(jnp.int32), axis=0, keepdims=True)
        pick = slot == rank
        xsel_ref[e] = jnp.dot(pick.astype(BF16), h, preferred_element_type=F32).astype(xsel_ref.dtype)
        vals_ref[e] = jnp.sum(jnp.where(pick, a_row, 0.0), axis=1, keepdims=True)
        rank_ref[e:e + 1, :] = rank


def _route(logits, h2, *, batch, seq, n_exp):
    t, d = h2.shape
    cap = EC_FACTOR * seq // n_exp
    return pl.pallas_call(
        functools.partial(_route_kernel, n_exp=n_exp, cap=cap),
        grid=(batch,),
        in_specs=[pl.BlockSpec((seq, LANES), lambda b: (b, 0)),
                  pl.BlockSpec((seq, d), lambda b: (b, 0))],
        out_specs=[pl.BlockSpec((n_exp, cap, d), lambda b: (0, b, 0)),
                   pl.BlockSpec((n_exp, cap, 1), lambda b: (0, b, 0)),
                   pl.BlockSpec((None, n_exp, seq), lambda b: (b, 0, 0))],
        out_shape=[jax.ShapeDtypeStruct((n_exp, batch * cap, d), BF16),
                   jax.ShapeDtypeStruct((n_exp, batch * cap, 1), F32),
                   jax.ShapeDtypeStruct((batch, n_exp, seq), jnp.int32)],
        compiler_params=_cparams(1),
    )(logits, h2)


def _ffn_hidden_kernel(xp_ref, xs_ref, wg_ref, wu_ref, hp_ref, hs_ref):
    wg = wg_ref[...].astype(BF16)
    wu = wu_ref[...].astype(BF16)
    for x_ref, h_ref in ((xp_ref, hp_ref), (xs_ref, hs_ref)):
        x = x_ref[...]
        h_ref[...] = (jax.nn.silu(jnp.dot(x, wg, preferred_element_type=F32))
                      * jnp.dot(x, wu, preferred_element_type=F32)).astype(h_ref.dtype)


def _ffn_down_kernel(hp_ref, hs_ref, vp_ref, vs_ref, wd_ref, op_ref, os_ref):
    wd = wd_ref[...].astype(BF16)
    for h_ref, v_ref, o_ref in ((hp_ref, vp_ref, op_ref), (hs_ref, vs_ref, os_ref)):
        o_ref[...] = (jnp.dot(h_ref[...], wd, preferred_element_type=F32) * v_ref[...]).astype(o_ref.dtype)


def _ffn(xsel_p, xsel_s, vals_p, vals_s, w_gate, w_up, w_down, layer):
    n_exp, rp, d = xsel_p.shape
    rs = xsel_s.shape[1]
    ff = w_gate.shape[-1]
    tf = _tile(ff, 512)
    tn = _tile(d, 512)

    def per_expert(r, w):
        return pl.BlockSpec((None, r, w), lambda e, j: (e, 0, 0))

    def col_tile(r, w):
        return pl.BlockSpec((None, r, w), lambda e, j: (e, 0, j))

    def w_tile(k, w):
        return pl.BlockSpec((None, None, k, w), lambda e, j: (layer, e, 0, j))

    hid_p, hid_s = pl.pallas_call(
        _ffn_hidden_kernel,
        grid=(n_exp, ff // tf),
        in_specs=[per_expert(rp, d), per_expert(rs, d), w_tile(d, tf), w_tile(d, tf)],
        out_specs=[col_tile(rp, tf), col_tile(rs, tf)],
        out_shape=[jax.ShapeDtypeStruct((n_exp, rp, ff), BF16), jax.ShapeDtypeStruct((n_exp, rs, ff), BF16)],
        compiler_params=_cparams(2),
    )(xsel_p, xsel_s, w_gate, w_up)
    return pl.pallas_call(
        _ffn_down_kernel,
        grid=(n_exp, d // tn),
        in_specs=[per_expert(rp, ff), per_expert(rs, ff), per_expert(rp, 1), per_expert(rs, 1),
                  w_tile(ff, tn)],
        out_specs=[col_tile(rp, tn), col_tile(rs, tn)],
        out_shape=[jax.ShapeDtypeStruct((n_exp, rp, d), BF16), jax.ShapeDtypeStruct((n_exp, rs, d), BF16)],
        compiler_params=_cparams(2),
    )(hid_p, hid_s, vals_p, vals_s, w_down)


def _scatter_kernel(out_ref, rank_ref, x_ref, g2_ref, lg_ref, lb_ref, x2_ref, *, alpha):
    n_exp, cap, d = out_ref.shape
    n = x_ref.shape[0]
    slot = lax.broadcasted_iota(jnp.int32, (cap, n), 0)
    pick = jnp.concatenate([(slot == rank_ref[e:e + 1, :]).astype(F32) for e in range(n_exp)], axis=0)
    f = jnp.dot(pick.T.astype(BF16), out_ref[...].reshape(n_exp * cap, d), preferred_element_type=F32)
    x2_ref[...] = _ln(alpha * x_ref[...] + g2_ref[...] * f) * lg_ref[...] + lb_ref[...]


def _scatter(out, rank, x1, mod4, mod_row, ln_g, ln_b, layer, alpha, *, batch, seq):
    n_exp, _, d = out.shape
    cap = out.shape[1] // batch
    blocks_per_seq = seq // ROW_BLOCK
    return pl.pallas_call(
        functools.partial(_scatter_kernel, alpha=alpha),
        grid=(batch,),
        in_specs=[pl.BlockSpec((n_exp, cap, d), lambda b: (0, b, 0)),
                  pl.BlockSpec((None, n_exp, seq), lambda b: (b, 0, 0)),
                  pl.BlockSpec((seq, d), lambda b: (b, 0)),
                  pl.BlockSpec((None, None, 1, d), lambda b: (mod_row(b * blocks_per_seq), 5, 0, 0)),
                  pl.BlockSpec((None, 1, d), lambda b: (layer, 0, 0)),
                  pl.BlockSpec((None, 1, d), lambda b: (layer, 0, 0))],
        out_specs=pl.BlockSpec((seq, d), lambda b: (b, 0)),
        out_shape=jax.ShapeDtypeStruct((batch * seq, d), F32),
        compiler_params=_cparams(1),
    )(out, rank, x1, mod4, ln_g, ln_b)


def _mixers(x, h, mod4, p, st, layer):
    batch, seq = st["batch"], st["seq"]
    dims = p["dims"]
    hw, kvw, hg, kvg, ssm_w = dims["heads_win"], dims["kv_win"], dims["heads_glob"], dims["kv_glob"], dims["ssm_w"]
    wa = (hw + 2 * kvw) * HEAD_DIM
    wb = (hg + 2 * kvg) * HEAD_DIM
    pa = _matmul(h, p["w_in"], layer, 0, wa)
    pu = _matmul(h, p["w_in"], layer, wa, ssm_w)
    pb = _matmul(h, p["w_in"], layer, wa + ssm_w, wb)
    gates = _matmul(h, p["w_in"], layer, wa + ssm_w + wb, p["w_in"].shape[-1] - wa - ssm_w - wb, tn=1024)
    k_w0, v_w0 = hw * HEAD_DIM, (hw + kvw) * HEAD_DIM
    k_g0, v_g0 = hg * HEAD_DIM, (hg + kvg) * HEAD_DIM
    win = dict(heads=hw, kv=kvw, batch=batch, seq=seq, sink=p["win_sink"])
    glob = dict(heads=hg, kv=kvg, batch=batch, seq=seq, q_norm=p["q_norm"], k_norm=p["k_norm"])
    if st["latent"]:
        o_win = _attention(pa, 0, pa, k_w0, pa, v_w0, k_ctx=st["ck_w"], v_ctx=st["cv_w"],
                           rope=True, band=True, **win)
        o_glob = _attention(pb, 0, pb, k_g0, pb, v_g0, k_ctx=st["ck_g"], v_ctx=st["cv_g"], rope=True, **glob)
        k_g = None
    else:
        o_win = _attention(pa, 0, pa, k_w0, pa, v_w0, **win)
        o_glob, k_g = _attention(pb, 0, pb, k_g0, pb, v_g0, emit_k=True, **glob)
    y_ssm, fin = _s5_mix(pu, 0, ssm_w, *p["s5"], st["s0"], batch=batch, seq=seq)
    o_ssm = _glu(y_ssm, pu, 0, p["ssm_d"], p["ssm_w_glu"], layer)
    r = _merge(o_win, o_ssm, o_glob, gates, x, mod4, st["mod_row"], p["w_up_win"], p["w_up_ssm"],
               p["w_up_glob"], p["w_out"], layer, p["alpha"])
    extras = (pa[:, k_w0:v_w0], pa[:, v_w0:wa], k_g, pb[:, v_g0:v_g0 + kvg * HEAD_DIM], fin)
    return r, extras


def kernel(x_prompt, x_sample, cache_win_k, cache_win_v, cache_glob_k, cache_glob_v, state_ssm_re, state_ssm_im, c, c_ctx, w_mod, b_mod, w_in, win_sink, ssm_a_re, ssm_a_im, ssm_log_dt, ssm_b_re, ssm_b_im, ssm_c_re, ssm_c_im, ssm_d, ssm_w_glu, q_norm, k_norm, w_up_win, w_up_ssm, w_up_glob, w_out, ln1_g, ln1_b, ln2_g, ln2_b, router_w, exp_w_gate, exp_w_up, exp_w_down):
    bp, lp, d = x_prompt.shape
    bs, ls, _ = x_sample.shape
    depth = w_mod.shape[0]
    past = cache_win_k.shape[2]
    kv_win, kv_glob = cache_win_k.shape[3], cache_glob_k.shape[3]
    kvw, kvg = kv_win * HEAD_DIM, kv_glob * HEAD_DIM
    ssm_w = w_up_ssm.shape[1]
    n_exp = router_w.shape[-1]
    p_n = ssm_a_re.shape[-1]
    alpha = (2.0 * depth) ** 0.25
    dims = dict(heads_win=w_up_win.shape[1] // HEAD_DIM, kv_win=kv_win,
                heads_glob=w_up_glob.shape[1] // HEAD_DIM, kv_glob=kv_glob, ssm_w=ssm_w)
    assert lp % ROW_BLOCK == 0 and ls % ROW_BLOCK == 0 and 1 + bs <= 8

    cond = jnp.zeros((8, d), F32).at[0].set(c_ctx).at[1:1 + bs].set(c)
    sample_blocks = ls // ROW_BLOCK
    streams = [
        dict(batch=bp, seq=lp, latent=False, mod_row=lambda i: 0,
             s0=jnp.zeros((ssm_w // LANES, bp, 4 * S5_OCT * p_n), F32)),
        dict(batch=bs, seq=ls, latent=True, mod_row=lambda i: 1 + i // sample_blocks),
    ]
    xs = [x_prompt.reshape(bp * lp, d), x_sample.reshape(bs * ls, d)]
    new = {k: [] for k in ("wk", "wv", "gk", "gv", "sre", "sim")}

    lay = _s5_layouts(ssm_log_dt, ssm_a_re, ssm_a_im, ssm_b_re, ssm_b_im, ssm_c_re, ssm_c_im)
    p = dict(w_in=w_in, dims=dims, ssm_d=ssm_d.reshape(depth, 1, ssm_w),
             ssm_w_glu=ssm_w_glu.astype(BF16), w_up_win=w_up_win.astype(BF16),
             w_up_ssm=w_up_ssm.astype(BF16), w_up_glob=w_up_glob.astype(BF16))
    p.update(w_out=w_out.astype(BF16), alpha=alpha)
    router_pad = jnp.pad(router_w.astype(F32), ((0, 0), (0, 0), (0, LANES - n_exp)))
    router_hi = router_pad.astype(BF16)
    router_lo = (router_pad - router_hi.astype(F32)).astype(BF16)
    b_mod3 = b_mod.reshape(depth, 1, 6 * d)
    ln1_g3, ln1_b3 = ln1_g.reshape(depth, 1, d), ln1_b.reshape(depth, 1, d)
    ln2_g3, ln2_b3 = ln2_g.reshape(depth, 1, d), ln2_b.reshape(depth, 1, d)

    for l in range(depth):
        p.update(win_sink=win_sink[l], q_norm=q_norm[l], k_norm=k_norm[l], s5=_s5_prep(lay, l))
        mod = _matmul(cond, w_mod, l, 0, 6 * d, bias=b_mod3, act="silu", tn=1024)
        mod4 = mod.reshape(8, 6, 1, d)
        streams[1].update(
            ck_w=cache_win_k[:, l].reshape(bs, past, kvw), cv_w=cache_win_v[:, l].reshape(bs, past, kvw),
            ck_g=cache_glob_k[:, l].reshape(bs, past, kvg), cv_g=cache_glob_v[:, l].reshape(bs, past, kvg),
            s0=_state_to_lanes(state_ssm_re[:, l], state_ssm_im[:, l]))

        routed = []
        for si, st in enumerate(streams):
            x = xs[si]
            h = _ln_mod(x, mod4, st["mod_row"])
            r, extras = _mixers(x, h, mod4, p, st, l)
            tm = _tile(st["seq"], 512) if st["latent"] else _tile(x.shape[0], 512)
            x1, h2, logits = _post_ln(r, mod4, st["mod_row"], ln1_g3, ln1_b3, router_hi, router_lo, l, tm)
            xsel, vals, rank = _route(logits, h2, batch=st["batch"], seq=st["seq"], n_exp=n_exp)
            routed.append((x1, xsel, vals, rank))
            if not st["latent"]:
                k_w, v_w, k_g, v_g, fin = extras
                new["wk"].append(k_w.reshape(bp, lp, -1, HEAD_DIM))
                new["wv"].append(v_w.reshape(bp, lp, -1, HEAD_DIM))
                new["gk"].append(k_g.reshape(bp, lp, -1, HEAD_DIM))
                new["gv"].append(v_g.reshape(bp, lp, -1, HEAD_DIM))
                s_re, s_im = _lanes_to_state(fin, p_n)
                new["sre"].append(s_re)
                new["sim"].append(s_im)
        outs = _ffn(routed[0][1], routed[1][1], routed[0][2], routed[1][2],
                    exp_w_gate, exp_w_up, exp_w_down, l)
        for si, st in enumerate(streams):
            xs[si] = _scatter(outs[si], routed[si][3], routed[si][0], mod4, st["mod_row"],
                              ln2_g3, ln2_b3, l, alpha, batch=st["batch"], seq=st["seq"])

    return (xs[0].reshape(bp, lp, d), xs[1].reshape(bs, ls, d),
            jnp.stack(new["wk"], axis=1), jnp.stack(new["wv"], axis=1),
            jnp.stack(new["gk"], axis=1), jnp.stack(new["gv"], axis=1),
            jnp.stack(new["sre"], axis=1), jnp.stack(new["sim"], axis=1))
```

```python
import functools

import numpy as np
import jax
import jax.numpy as jnp
from jax import lax
from jax.experimental import pallas as pl
from jax.experimental.pallas import tpu as pltpu

F32 = jnp.float32
BF16 = jnp.bfloat16
HIGHEST = lax.Precision.HIGHEST

HEAD_DIM = 128
LANES = 128
GRID_W = 64
WINDOW = 128
EC_FACTOR = 2
ROPE_THETA = 10000.0
EPS = 1e-6
NEG_INF = -1e30
ATTN_SCALE = HEAD_DIM ** -0.5
S5_CHUNK = 16
S5_OCT = 8
ROW_BLOCK = 256
GATHER_ROWS = 512
VMEM_LIMIT = 60 * 1024 * 1024
RESIDENT = pl.Buffered(1)


def _cparams(n_axes):
    return pltpu.CompilerParams(dimension_semantics=("arbitrary",) * n_axes,
                                vmem_limit_bytes=VMEM_LIMIT)


def _tile(dim, pref):
    return pref if dim % pref == 0 else dim


def _ln(x):
    mu = jnp.mean(x, axis=-1, keepdims=True)
    xc = x - mu
    var = jnp.mean(xc * xc, axis=-1, keepdims=True)
    return xc * lax.rsqrt(var + EPS)


def _split_bf16(x):
    hi = x.astype(BF16)
    return hi, (x - hi.astype(F32)).astype(BF16)


def _dot3(a_hi, a_lo, b_hi, b_lo, dims=(((1,), (0,)), ((), ()))):
    def d(x, y):
        return lax.dot_general(x, y, dims, preferred_element_type=F32)
    return d(a_hi, b_hi) + d(a_lo, b_hi) + d(a_hi, b_lo)


def _dot3_nt(a, b):
    return _dot3(*_split_bf16(a), *_split_bf16(b), dims=(((1,), (1,)), ((), ())))


def _mm_kernel(x_ref, w_ref, *rest, act, has_bias):
    if has_bias:
        b_ref, o_ref, wbf_ref = rest
    else:
        o_ref, wbf_ref = rest

    @pl.when(pl.program_id(1) == 0)
    def _():
        wbf_ref[...] = w_ref[...].astype(BF16)

    x = x_ref[...]
    if act == "silu":
        x = jax.nn.silu(x.astype(F32))
    acc = jnp.dot(x.astype(BF16), wbf_ref[...], preferred_element_type=F32)
    if has_bias:
        acc = acc + b_ref[...]
    o_ref[...] = acc.astype(o_ref.dtype)


def _matmul(x, w, layer, col0, ncols, *, bias=None, act=None, out_dtype=F32, tm=1024, tn=512):
    m, k = x.shape
    tm = _tile(m, tm)
    tn = next(t for t in (tn, 512, 256, LANES) if ncols % t == 0 and col0 % t == 0)
    c0 = col0 // tn
    in_specs = [pl.BlockSpec((tm, k), lambda n, i: (i, 0)),
                pl.BlockSpec((None, k, tn), lambda n, i: (layer, 0, c0 + n))]
    args = [x, w]
    if bias is not None:
        in_specs.append(pl.BlockSpec((None, 1, tn), lambda n, i: (layer, 0, c0 + n)))
        args.append(bias)
    return pl.pallas_call(
        functools.partial(_mm_kernel, act=act, has_bias=bias is not None),
        grid=(ncols // tn, m // tm),
        in_specs=in_specs,
        out_specs=pl.BlockSpec((tm, tn), lambda n, i: (i, n)),
        out_shape=jax.ShapeDtypeStruct((m, ncols), out_dtype),
        scratch_shapes=[pltpu.VMEM((k, tn), BF16)],
        compiler_params=_cparams(2),
    )(*args)


def _ln_mod_kernel(x_ref, sc_ref, sh_ref, h_ref):
    h_ref[...] = (_ln(x_ref[...]) * (1.0 + sc_ref[...]) + sh_ref[...]).astype(h_ref.dtype)


def _mod_spec(d, chunk, mod_row, rows_per_step=ROW_BLOCK):
    scale = rows_per_step // ROW_BLOCK
    return pl.BlockSpec((None, None, 1, d), lambda i: (mod_row(i * scale), chunk, 0, 0))


def _ln_mod(x, mod4, mod_row):
    t, d = x.shape
    return pl.pallas_call(
        _ln_mod_kernel,
        grid=(t // ROW_BLOCK,),
        in_specs=[pl.BlockSpec((ROW_BLOCK, d), lambda i: (i, 0)),
                  _mod_spec(d, 1, mod_row), _mod_spec(d, 0, mod_row)],
        out_specs=pl.BlockSpec((ROW_BLOCK, d), lambda i: (i, 0)),
        out_shape=jax.ShapeDtypeStruct((t, d), BF16),
        compiler_params=_cparams(1),
    )(x, mod4, mod4)


def _rope_tables(seq_len):
    half = HEAD_DIM // 4
    inv = ROPE_THETA ** (-np.arange(half, dtype=np.float64) / half)
    tok = np.arange(seq_len)
    ang_r = (tok // GRID_W)[:, None] * inv[None, :]
    ang_c = (tok % GRID_W)[:, None] * inv[None, :]
    cos = np.concatenate([np.cos(ang_r), np.cos(ang_r), np.cos(ang_c), np.cos(ang_c)], axis=-1)
    sin = np.concatenate([-np.sin(ang_r), np.sin(ang_r), -np.sin(ang_c), np.sin(ang_c)], axis=-1)
    return jnp.asarray(cos, F32), jnp.asarray(sin, F32)


def _rope(x, cos, sin_signed):
    lane = lax.broadcasted_iota(jnp.int32, x.shape, 1)
    swapped = jnp.where((lane % 64) < 32, pltpu.roll(x, 96, 1), pltpu.roll(x, 32, 1))
    return x * cos + swapped * sin_signed


def _rms(x, w):
    return x * lax.rsqrt(jnp.mean(x * x, axis=-1, keepdims=True) + EPS) * w


def _attn_kernel(*refs, heads, kv, tq, rope, norm, sink, ctx, band, emit_k):
    it = iter(refs)
    q_ref, k_ref, v_ref = next(it), next(it), next(it)
    kc_ref = vc_ref = sink_ref = qn_ref = kn_ref = cq_ref = sq_ref = ck_ref = sk_ref = None
    if ctx:
        kc_ref, vc_ref = next(it), next(it)
    if sink:
        sink_ref = next(it)
    if norm:
        qn_ref, kn_ref = next(it), next(it)
    if rope:
        cq_ref, sq_ref, ck_ref, sk_ref = next(it), next(it), next(it), next(it)
    o_ref = next(it)
    ko_ref = next(it) if emit_k else None
    kp_ref, vp_ref = next(it), next(it)

    qi = pl.program_id(1)
    rep = heads // kv

    @pl.when(qi == 0)
    def _():
        for g in range(kv):
            sl = slice(g * HEAD_DIM, (g + 1) * HEAD_DIM)
            kg = k_ref[:, sl]
            if norm:
                kg = _rms(kg, kn_ref[...])
            if rope:
                kg = _rope(kg, ck_ref[...], sk_ref[...])
            if emit_k:
                ko_ref[:, sl] = kg
            kp_ref[:, sl] = kg.astype(BF16)
        vp_ref[...] = v_ref[...].astype(BF16)

    nt = (((1,), (1,)), ((), ()))
    seq = kp_ref.shape[0]
    if band:
        span = min(seq, tq + 2 * max(WINDOW, tq))
        k0 = pl.multiple_of(jnp.clip(qi * tq - max(WINDOW, tq), 0, seq - span), tq)
        rows = pl.ds(k0, span)
        qpos = qi * tq + lax.broadcasted_iota(jnp.int32, (tq, span), 0)
        kpos = k0 + lax.broadcasted_iota(jnp.int32, (tq, span), 1)
        in_band = jnp.abs(qpos - kpos) <= WINDOW
    else:
        rows = slice(None)
    for h in range(heads):
        g = h // rep
        gs = slice(g * HEAD_DIM, (g + 1) * HEAD_DIM)
        qh = q_ref[:, h * HEAD_DIM:(h + 1) * HEAD_DIM]
        if norm:
            qh = _rms(qh, qn_ref[...])
        if rope:
            qh = _rope(qh, cq_ref[...], sq_ref[...])
        qh = (qh * ATTN_SCALE).astype(BF16)
        s = lax.dot_general(qh, kp_ref[rows, gs], nt, preferred_element_type=F32)
        if band:
            s = jnp.where(in_band, s, NEG_INF)
        m = jnp.max(s, axis=-1, keepdims=True)
        if ctx:
            sc = lax.dot_general(qh, kc_ref[:, gs].astype(BF16), nt, preferred_element_type=F32)
            m = jnp.maximum(m, jnp.max(sc, axis=-1, keepdims=True))
        if sink:
            m = jnp.maximum(m, sink_ref[h])
        e = jnp.exp(s - m)
        den = jnp.sum(e, axis=-1, keepdims=True)
        o = jnp.dot(e.astype(BF16), vp_ref[rows, gs], preferred_element_type=F32)
        if ctx:
            ec = jnp.exp(sc - m)
            den = den + jnp.sum(ec, axis=-1, keepdims=True)
            o = o + jnp.dot(ec.astype(BF16), vc_ref[:, gs].astype(BF16), preferred_element_type=F32)
        if sink:
            den = den + jnp.exp(sink_ref[h] - m)
        o_ref[:, h * HEAD_DIM:(h + 1) * HEAD_DIM] = (o / den).astype(o_ref.dtype)


def _attention(qa, q0, ka, k0, va, v0, *, heads, kv, batch, seq, k_ctx=None, v_ctx=None, sink=None,
               q_norm=None, k_norm=None, rope=False, band=False, emit_k=False):
    t = qa.shape[0]
    qw, kw = heads * HEAD_DIM, kv * HEAD_DIM
    assert q0 % qw == 0 and k0 % kw == 0 and v0 % kw == 0
    qc, kc, vc = q0 // qw, k0 // kw, v0 // kw
    tq = _tile(seq, 256)
    nq = seq // tq
    ctx, has_sink, norm = k_ctx is not None, sink is not None, q_norm is not None
    in_specs = [pl.BlockSpec((tq, qw), lambda b, i: (b * nq + i, qc)),
                pl.BlockSpec((seq, kw), lambda b, i: (b, kc)),
                pl.BlockSpec((seq, kw), lambda b, i: (b, vc))]
    args = [qa, ka, va]
    if ctx:
        lc = k_ctx.shape[1]
        in_specs += [pl.BlockSpec((None, lc, kw), lambda b, i: (b, 0, 0))] * 2
        args += [k_ctx, v_ctx]
    if has_sink:
        in_specs.append(pl.BlockSpec(memory_space=pltpu.SMEM))
        args.append(sink)
    if norm:
        in_specs += [pl.BlockSpec((1, HEAD_DIM), lambda b, i: (0, 0))] * 2
        args += [q_norm.reshape(1, HEAD_DIM), k_norm.reshape(1, HEAD_DIM)]
    if rope:
        cos, sin = _rope_tables(seq)
        in_specs += [pl.BlockSpec((tq, HEAD_DIM), lambda b, i: (i, 0))] * 2
        in_specs += [pl.BlockSpec((seq, HEAD_DIM), lambda b, i: (0, 0))] * 2
        args += [cos, sin, cos, sin]
    out_specs = [pl.BlockSpec((tq, qw), lambda b, i: (b * nq + i, 0))]
    out_shape = [jax.ShapeDtypeStruct((t, qw), BF16)]
    if emit_k:
        out_specs.append(pl.BlockSpec((seq, kw), lambda b, i: (b, 0)))
        out_shape.append(jax.ShapeDtypeStruct((t, kw), F32))
    res = pl.pallas_call(
        functools.partial(_attn_kernel, heads=heads, kv=kv, tq=tq, rope=rope, norm=norm,
                          sink=has_sink, ctx=ctx, band=band, emit_k=emit_k),
        grid=(batch, nq),
        in_specs=in_specs,
        out_specs=out_specs,
        out_shape=out_shape,
        scratch_shapes=[pltpu.VMEM((seq, kw), BF16), pltpu.VMEM((seq, kw), BF16)],
        compiler_params=_cparams(2),
    )(*args)
    return res if emit_k else res[0]


def _s5_layouts(log_dt, a_re, a_im, b_re, b_im, c_re, c_im):
    dep, _, g_n, p_n = a_re.shape
    gc = b_re.shape[-1]
    no = g_n // S5_OCT
    sw = S5_OCT * p_n
    ldt = jnp.broadcast_to(log_dt[..., None], a_re.shape)
    prm = jnp.stack([ldt, a_re, a_im], axis=2).astype(F32)
    prow = jnp.pad(prm.reshape(dep, 2, 3, no, sw).transpose(0, 1, 3, 2, 4),
                   ((0, 0),) * 3 + ((0, 5), (0, 0)))
    pcmp = jnp.broadcast_to(prm[:, :, :, :, None, :], (dep, 2, 3, g_n, gc, p_n))
    pcmp = pcmp.reshape(dep, 2, 3, no, S5_OCT * gc, p_n).transpose(0, 1, 3, 2, 4, 5)

    def b_views(b):
        bt = jnp.swapaxes(b.astype(F32), -1, -2).reshape(dep, 2, no, S5_OCT * gc, p_n)
        return bt, jnp.tile(bt, (1, 1, 1, 1, S5_OCT))

    def c_views(c):
        cc = c.astype(F32).reshape(dep, 2, no, S5_OCT * gc, p_n)
        ct = jnp.swapaxes(c.astype(F32), -1, -2).reshape(dep, 2, no, sw, gc)
        return cc, jnp.tile(ct, (1, 1, 1, 1, S5_OCT))

    bc_re, bt_re = b_views(b_re)
    bc_im, bt_im = b_views(b_im)
    cc_re, ct_re = c_views(c_re)
    cc_im, ct_im = c_views(c_im)
    return dict(prow=prow, pcmp=pcmp, bc_re=bc_re, bc_im=bc_im, bt_re=bt_re, bt_im=bt_im,
                cc_re=cc_re, cc_im=cc_im, ct_re=ct_re, ct_im=ct_im)


def _lam_powers(ldt, ar, ai, k):
    dt = jnp.exp(ldt)
    mag = jnp.exp(k * (dt * ar))
    return mag * jnp.cos(k * (dt * ai)), mag * jnp.sin(k * (dt * ai))


def _bbar_coef(ldt, ar, ai):
    lam_re, lam_im = _lam_powers(ldt, ar, ai, 1.0)
    den = ar * ar + ai * ai
    return ((lam_re - 1.0) * ar + lam_im * ai) / den, (lam_im * ar - (lam_re - 1.0) * ai) / den


def _same_group(shape, rows_per_group, cols_per_group):
    r = lax.broadcasted_iota(jnp.int32, shape, 0) // rows_per_group
    c = lax.broadcasted_iota(jnp.int32, shape, 1) // cols_per_group
    return r == c


def _s5_prep_w_kernel(prow_ref, btr_ref, bti_ref, w_ref, lam_ref):
    tc = S5_CHUNK
    rows, sw = btr_ref.shape[1], btr_ref.shape[2]
    mask = _same_group((rows, sw), rows // S5_OCT, sw // S5_OCT)
    k = lax.broadcasted_iota(jnp.int32, (24, sw), 0).astype(F32)
    for d in range(2):
        ldt, ar, ai = prow_ref[d, 0:1, :], prow_ref[d, 1:2, :], prow_ref[d, 2:3, :]
        pw_re, pw_im = _lam_powers(ldt, ar, ai, k)
        co_re, co_im = _bbar_coef(ldt, ar, ai)
        bb_re = jnp.where(mask, co_re * btr_ref[d] - co_im * bti_ref[d], 0.0)
        bb_im = jnp.where(mask, co_re * bti_ref[d] + co_im * btr_ref[d], 0.0)
        lam_ref[2 * d:2 * d + 1, :] = pw_re[tc:tc + 1]
        lam_ref[2 * d + 1:2 * d + 2, :] = pw_im[tc:tc + 1]
        for j in range(tc):
            kk = tc - 1 - j if d == 0 else j
            pr, pi = pw_re[kk:kk + 1], pw_im[kk:kk + 1]
            rs = slice(j * rows, (j + 1) * rows)
            w_ref[rs, (2 * d) * sw:(2 * d + 1) * sw] = (pr * bb_re - pi * bb_im).astype(w_ref.dtype)
            w_ref[rs, (2 * d + 1) * sw:(2 * d + 2) * sw] = (pr * bb_im + pi * bb_re).astype(w_ref.dtype)


def _row_to_col(row, eye):
    return jnp.sum(jnp.where(eye, row, 0.0), axis=1, keepdims=True)


def _s5_prep_z_kernel(prow_ref, ctr_ref, cti_ref, z_ref):
    tc = S5_CHUNK
    sw, cols = ctr_ref.shape[1], ctr_ref.shape[2]
    mask = _same_group((sw, cols), sw // S5_OCT, cols // S5_OCT)
    eye = _same_group((sw, sw), 1, 1)
    for d in range(2):
        lam_re, lam_im = _lam_powers(prow_ref[d, 0:1, :], prow_ref[d, 1:2, :], prow_ref[d, 2:3, :], 1.0)
        lr = jnp.broadcast_to(_row_to_col(lam_re, eye), (sw, cols))
        li = jnp.broadcast_to(_row_to_col(lam_im, eye), (sw, cols))
        c_re = jnp.where(mask, ctr_ref[d], 0.0)
        c_im = jnp.where(mask, cti_ref[d], 0.0)
        z_re, z_im = c_re * lr - c_im * li, c_re * li + c_im * lr
        for step in range(tc):
            t = step if d == 0 else tc - 1 - step
            cs = slice(t * cols, (t + 1) * cols)
            z_ref[(2 * d) * sw:(2 * d + 1) * sw, cs] = z_re.astype(z_ref.dtype)
            z_ref[(2 * d + 1) * sw:(2 * d + 2) * sw, cs] = (-z_im).astype(z_ref.dtype)
            z_re, z_im = z_re * lr - z_im * li, z_re * li + z_im * lr


def _s5_prep_m_kernel(pcmp_ref, bcr_ref, bci_ref, ccr_ref, cci_ref, m_ref, xr_ref, xi_ref, taps_ref):
    tc = S5_CHUNK
    rows = bcr_ref.shape[1]
    mask = _same_group((rows, rows), rows // S5_OCT, rows // S5_OCT)
    for d in range(2):
        ldt, ar, ai = pcmp_ref[d, 0], pcmp_ref[d, 1], pcmp_ref[d, 2]
        lam_re, lam_im = _lam_powers(ldt, ar, ai, 1.0)
        co_re, co_im = _bbar_coef(ldt, ar, ai)
        x_re = co_re * bcr_ref[d] - co_im * bci_ref[d]
        x_im = co_re * bci_ref[d] + co_im * bcr_ref[d]
        for kk in range(tc):
            xr_ref[kk * rows:(kk + 1) * rows, :] = x_re
            xi_ref[kk * rows:(kk + 1) * rows, :] = x_im
            x_re, x_im = x_re * lam_re - x_im * lam_im, x_re * lam_im + x_im * lam_re
        taps = _dot3_nt(xr_ref[...], ccr_ref[d]) - _dot3_nt(xi_ref[...], cci_ref[d])
        for kk in range(tc):
            tile = jnp.where(mask, taps[kk * rows:(kk + 1) * rows, :], 0.0)
            col = (tc - 1 + kk) if d == 0 else (tc - 1 - kk)
            cs = slice(col * rows, (col + 1) * rows)
            if d == 1 and kk == 0:
                taps_ref[:, cs] += tile
            else:
                taps_ref[:, cs] = tile
    for j in range(tc):
        m_ref[j * rows:(j + 1) * rows, :] = taps_ref[:, (tc - 1 - j) * rows:(2 * tc - 1 - j) * rows].astype(m_ref.dtype)


def _s5_prep(lay, layer):
    _, _, no, rows, sw = lay["bt_re"].shape
    p_n = lay["bc_re"].shape[-1]
    tc = S5_CHUNK
    kw = tc * rows

    def per_octet(*tail):
        nd = len(tail)
        return pl.BlockSpec((None, 2, None) + tail, lambda o: (layer, 0, o) + (0,) * nd)

    def out(r, c):
        return pl.BlockSpec((None, r, c), lambda o: (o, 0, 0))

    w_o, lam_o = pl.pallas_call(
        _s5_prep_w_kernel, grid=(no,),
        in_specs=[per_octet(8, sw), per_octet(rows, sw), per_octet(rows, sw)],
        out_specs=[out(kw, 4 * sw), out(4, sw)],
        out_shape=[jax.ShapeDtypeStruct((no, kw, 4 * sw), BF16), jax.ShapeDtypeStruct((no, 4, sw), F32)],
        compiler_params=_cparams(1),
    )(lay["prow"], lay["bt_re"], lay["bt_im"])
    z_o = pl.pallas_call(
        _s5_prep_z_kernel, grid=(no,),
        in_specs=[per_octet(8, sw), per_octet(sw, rows), per_octet(sw, rows)],
        out_specs=out(4 * sw, kw),
        out_shape=jax.ShapeDtypeStruct((no, 4 * sw, kw), BF16),
        compiler_params=_cparams(1),
    )(lay["prow"], lay["ct_re"], lay["ct_im"])
    m_o = pl.pallas_call(
        _s5_prep_m_kernel, grid=(no,),
        in_specs=[per_octet(3, rows, p_n)] + [per_octet(rows, p_n)] * 4,
        out_specs=out(kw, kw),
        out_shape=jax.ShapeDtypeStruct((no, kw, kw), BF16),
        scratch_shapes=[pltpu.VMEM((kw, p_n), F32), pltpu.VMEM((kw, p_n), F32),
                        pltpu.VMEM((rows, (2 * tc - 1) * rows), F32)],
        compiler_params=_cparams(1),
    )(lay["pcmp"], lay["bc_re"], lay["bc_im"], lay["cc_re"], lay["cc_im"])
    return m_o, w_o, z_o, lam_o


def _s5_state_kernel(u_ref, w_ref, lam_ref, s0_ref, sp_ref, fin_ref, ubf_ref, vs_ref, st_ref,
                     *, batch, nchunk):
    tc = S5_CHUNK
    for t in range(tc):
        ubf_ref[:, t * LANES:(t + 1) * LANES] = u_ref[:, t, :].astype(BF16)
    q = st_ref.shape[-1] // 4
    vs_ref[...] = jnp.dot(ubf_ref[...], w_ref[...],
                          preferred_element_type=F32).reshape(batch, nchunk, 4 * q)
    fa, fb = lam_ref[0:1, :], lam_ref[1:2, :]
    ba, bb = lam_ref[2:3, :], lam_ref[3:4, :]
    s_re, s_im = s0_ref[:, 0:q], s0_ref[:, q:2 * q]
    for c in range(nchunk):
        st_ref[:, c, 0:q] = s_re
        st_ref[:, c, q:2 * q] = s_im
        v_re, v_im = vs_ref[:, c, 0:q], vs_ref[:, c, q:2 * q]
        s_re, s_im = fa * s_re - fb * s_im + v_re, fa * s_im + fb * s_re + v_im
    fin_ref[:, 0:q] = s_re
    fin_ref[:, q:2 * q] = s_im
    s_re, s_im = s0_ref[:, 2 * q:3 * q], s0_ref[:, 3 * q:4 * q]
    for c in range(nchunk - 1, -1, -1):
        st_ref[:, c, 2 * q:3 * q] = s_re
        st_ref[:, c, 3 * q:4 * q] = s_im
        v_re, v_im = vs_ref[:, c, 2 * q:3 * q], vs_ref[:, c, 3 * q:4 * q]
        s_re, s_im = ba * s_re - bb * s_im + v_re, ba * s_im + bb * s_re + v_im
    fin_ref[:, 2 * q:3 * q] = s_re
    fin_ref[:, 3 * q:4 * q] = s_im
    sp_ref[...] = st_ref[...].reshape(batch * nchunk, 4 * q).astype(sp_ref.dtype)


def _s5_out_kernel(u_ref, sp_ref, m_ref, z_ref, y_ref, ubf_ref):
    tc = S5_CHUNK
    for t in range(tc):
        ubf_ref[:, t * LANES:(t + 1) * LANES] = u_ref[:, t, :].astype(BF16)
    y = (jnp.dot(ubf_ref[...], m_ref[...], preferred_element_type=F32)
         + jnp.dot(sp_ref[...], z_ref[...], preferred_element_type=F32))
    for t in range(tc):
        y_ref[:, t, :] = y[:, t * LANES:(t + 1) * LANES]


def _s5_mix(ua, u0, width, m_o, w_o, z_o, lam_o, s0, *, batch, seq):
    t, full = ua.shape
    tc = S5_CHUNK
    no = width // LANES
    assert u0 % LANES == 0
    uc = u0 // LANES
    r = t // tc
    nchunk = seq // tc
    sw = w_o.shape[-1]
    kw = tc * LANES
    u3 = ua.reshape(r, tc, full)
    sp, fin = pl.pallas_call(
        functools.partial(_s5_state_kernel, batch=batch, nchunk=nchunk),
        grid=(no,),
        in_specs=[pl.BlockSpec((r, tc, LANES), lambda o: (0, 0, uc + o)),
                  pl.BlockSpec((None, kw, sw), lambda o: (o, 0, 0)),
                  pl.BlockSpec((None, 4, sw // 4), lambda o: (o, 0, 0)),
                  pl.BlockSpec((None, batch, sw), lambda o: (o, 0, 0))],
        out_specs=[pl.BlockSpec((None, r, sw), lambda o: (o, 0, 0)),
                   pl.BlockSpec((None, batch, sw), lambda o: (o, 0, 0))],
        out_shape=[jax.ShapeDtypeStruct((no, r, sw), BF16),
                   jax.ShapeDtypeStruct((no, batch, sw), F32)],
        scratch_shapes=[pltpu.VMEM((r, kw), BF16),
                        pltpu.VMEM((batch, nchunk, sw), F32),
                        pltpu.VMEM((batch, nchunk, sw), F32)],
        compiler_params=_cparams(1),
    )(u3, w_o, lam_o, s0)
    tr = _tile(r, 256)
    y3 = pl.pallas_call(
        _s5_out_kernel,
        grid=(no, r // tr),
        in_specs=[pl.BlockSpec((tr, tc, LANES), lambda o, i: (i, 0, uc + o)),
                  pl.BlockSpec((None, tr, sw), lambda o, i: (o, i, 0)),
                  pl.BlockSpec((None, kw, kw), lambda o, i: (o, 0, 0)),
                  pl.BlockSpec((None, sw, kw), lambda o, i: (o, 0, 0))],
        out_specs=pl.BlockSpec((tr, tc, LANES), lambda o, i: (i, 0, o)),
        out_shape=jax.ShapeDtypeStruct((r, tc, width), F32),
        scratch_shapes=[pltpu.VMEM((tr, kw), BF16)],
        compiler_params=_cparams(2),
    )(u3, sp, m_o, z_o)
    return y3.reshape(t, width), fin


def _state_to_lanes(s_re, s_im):
    b, _, g_n, p_n = s_re.shape
    no = g_n // S5_OCT
    parts = jnp.stack([s_re[:, 0], s_im[:, 0], s_re[:, 1], s_im[:, 1]], axis=1)
    parts = parts.reshape(b, 4, no, S5_OCT * p_n)
    return jnp.transpose(parts, (2, 0, 1, 3)).reshape(no, b, 4 * S5_OCT * p_n).astype(F32)


def _lanes_to_state(fin, p_n):
    no, b, _ = fin.shape
    parts = jnp.transpose(fin.reshape(no, b, 4, S5_OCT, p_n), (1, 2, 0, 3, 4)).reshape(b, 4, no * S5_OCT, p_n)
    return jnp.stack([parts[:, 0], parts[:, 2]], axis=1), jnp.stack([parts[:, 1], parts[:, 3]], axis=1)


def _glu_kernel(y_ref, ua_ref, ub_ref, d_ref, w_ref, o_ref):
    u = jnp.concatenate([ua_ref[...], ub_ref[...]], axis=1)
    z = jax.nn.gelu(y_ref[...] + d_ref[...] * u).astype(BF16)
    g = jnp.dot(z, w_ref[...], preferred_element_type=F32)
    w = o_ref.shape[-1]
    o_ref[...] = (g[:, :w] * jax.nn.sigmoid(g[:, w:])).astype(o_ref.dtype)


def _glu(y, ua, u0, d, w_glu, layer):
    t, width = y.shape
    half = width // 2
    assert u0 % half == 0
    uc = u0 // half
    tm = _tile(t, 512)
    return pl.pallas_call(
        _glu_kernel,
        grid=(t // tm,),
        in_specs=[pl.BlockSpec((tm, width), lambda i: (i, 0)),
                  pl.BlockSpec((tm, half), lambda i: (i, uc)),
                  pl.BlockSpec((tm, half), lambda i: (i, uc + 1)),
                  pl.BlockSpec((None, 1, width), lambda i: (layer, 0, 0)),
                  pl.BlockSpec((None, width, 2 * width), lambda i: (layer, 0, 0), pipeline_mode=RESIDENT)],
        out_specs=pl.BlockSpec((tm, width), lambda i: (i, 0)),
        out_shape=jax.ShapeDtypeStruct((t, width), BF16),
        compiler_params=_cparams(1),
    )(y, ua, ua, d, w_glu)


def _merge_kernel(ow_ref, os_ref, og_ref, gw_ref, gs_ref, gg_ref, x_ref, g1_ref,
                  ww_ref, ws_ref, wg_ref, wo_ref, r_ref, *, alpha):
    def gate(g_ref):
        return jax.nn.sigmoid(g_ref[...].astype(F32))

    y = (gate(gw_ref) * jnp.dot(ow_ref[...], ww_ref[...], preferred_element_type=F32)
         + gate(gs_ref) * jnp.dot(os_ref[...], ws_ref[...], preferred_element_type=F32)
         + gate(gg_ref) * jnp.dot(og_ref[...], wg_ref[...], preferred_element_type=F32))
    o = jnp.dot(y.astype(BF16), wo_ref[...], preferred_element_type=F32)
    r_ref[...] = alpha * x_ref[...] + g1_ref[...] * o


def _merge(o_win, o_ssm, o_glob, gates, x, mod4, mod_row, w_win, w_ssm, w_glob, w_out, layer, alpha):
    t, d = x.shape
    tm = ROW_BLOCK

    def rows(w):
        return pl.BlockSpec((tm, w), lambda i: (i, 0))

    def whole(a):
        return pl.BlockSpec((None,) + a.shape[1:], lambda i: (layer, 0, 0), pipeline_mode=RESIDENT)

    return pl.pallas_call(
        functools.partial(_merge_kernel, alpha=alpha),
        grid=(t // tm,),
        in_specs=[rows(o_win.shape[1]), rows(o_ssm.shape[1]), rows(o_glob.shape[1]),
                  pl.BlockSpec((tm, d), lambda i: (i, 0)),
                  pl.BlockSpec((tm, d), lambda i: (i, 1)),
                  pl.BlockSpec((tm, d), lambda i: (i, 2)),
                  rows(d), _mod_spec(d, 2, mod_row),
                  whole(w_win), whole(w_ssm), whole(w_glob), whole(w_out)],
        out_specs=rows(d),
        out_shape=jax.ShapeDtypeStruct((t, d), F32),
        compiler_params=_cparams(1),
    )(o_win, o_ssm, o_glob, gates, gates, gates, x, mod4, w_win, w_ssm, w_glob, w_out)


def _post_ln_kernel(r_ref, sc2_ref, sh2_ref, lg_ref, lb_ref, rwh_ref, rwl_ref, x1_ref, h2_ref, logit_ref):
    x1 = _ln(r_ref[...]) * lg_ref[...] + lb_ref[...]
    x1_ref[...] = x1
    h2 = _ln(x1) * (1.0 + sc2_ref[...]) + sh2_ref[...]
    hi, lo = _split_bf16(h2)
    h2_ref[...] = hi
    logit_ref[...] = _dot3(hi, lo, rwh_ref[...], rwl_ref[...])


def _post_ln(r, mod4, mod_row, ln_g, ln_b, router_hi, router_lo, layer, tm):
    t, d = r.shape

    def rows(w):
        return pl.BlockSpec((tm, w), lambda i: (i, 0))

    def vec():
        return pl.BlockSpec((None, 1, d), lambda i: (layer, 0, 0))

    def router():
        return pl.BlockSpec((None, d, LANES), lambda i: (layer, 0, 0), pipeline_mode=RESIDENT)

    return pl.pallas_call(
        _post_ln_kernel,
        grid=(t // tm,),
        in_specs=[rows(d), _mod_spec(d, 4, mod_row, tm), _mod_spec(d, 3, mod_row, tm), vec(), vec(),
                  router(), router()],
        out_specs=[rows(d), rows(d), rows(LANES)],
        out_shape=[jax.ShapeDtypeStruct((t, d), F32), jax.ShapeDtypeStruct((t, d), BF16),
                   jax.ShapeDtypeStruct((t, LANES), F32)],
        compiler_params=_cparams(1),
    )(r, mod4, mod4, ln_g, ln_b, router_hi, router_lo)


def _route_kernel(logit_ref, h_ref, xsel_ref, vals_ref, rank_ref, *, n_exp, cap):
    n = logit_ref.shape[0]
    lg = logit_ref[...]
    col = lax.broadcasted_iota(jnp.int32, lg.shape, 1)
    lg = jnp.where(col < n_exp, lg, -jnp.inf)
    ex = jnp.exp(lg - jnp.max(lg, axis=-1, keepdims=True))
    aff = ex / jnp.sum(ex, axis=-1, keepdims=True)
    aff_t = aff.T
    jj = lax.broadcasted_iota(jnp.int32, (n, n), 0)
    ii = lax.broadcasted_iota(jnp.int32, (n, n), 1)
    slot = lax.broadcasted_iota(jnp.int32, (cap, n), 0)
    picks = []
    for e in range(n_exp):
        a_row = aff_t[e:e + 1, :]
        a_col = aff[:, e:e + 1]
        beats = (a_col > a_row) | ((a_col == a_row) & (jj < ii))
        rank = jnp.sum(beats.astype(jnp.int32), axis=0, keepdims=True)
        pick = slot == rank
        picks.append(pick.astype(BF16))
        vals_ref[e] = jnp.sum(jnp.where(pick, a_row, 0.0), axis=1, keepdims=True)
        rank_ref[e:e + 1, :] = rank
    group = min(n_exp, max(1, GATHER_ROWS // cap))
    for e0 in range(0, n_exp, group):
        sel = jnp.dot(jnp.concatenate(picks[e0:e0 + group], axis=0), h_ref[...], preferred_element_type=F32)
        for e in range(e0, min(e0 + group, n_exp)):
            xsel_ref[e] = sel[(e - e0) * cap:(e - e0 + 1) * cap, :].astype(xsel_ref.dtype)


def _route(logits, h2, *, batch, seq, n_exp):
    t, d = h2.shape
    cap = EC_FACTOR * seq // n_exp
    return pl.pallas_call(
        functools.partial(_route_kernel, n_exp=n_exp, cap=cap),
        grid=(batch,),
        in_specs=[pl.BlockSpec((seq, LANES), lambda b: (b, 0)),
                  pl.BlockSpec((seq, d), lambda b: (b, 0))],
        out_specs=[pl.BlockSpec((n_exp, cap, d), lambda b: (0, b, 0)),
                   pl.BlockSpec((n_exp, cap, 1), lambda b: (0, b, 0)),
                   pl.BlockSpec((None, n_exp, seq), lambda b: (b, 0, 0))],
        out_shape=[jax.ShapeDtypeStruct((n_exp, batch * cap, d), BF16),
                   jax.ShapeDtypeStruct((n_exp, batch * cap, 1), F32),
                   jax.ShapeDtypeStruct((batch, n_exp, seq), jnp.int32)],
        compiler_params=_cparams(1),
    )(logits, h2)


def _ffn_hidden_kernel(xp_ref, xs_ref, wg_ref, wu_ref, hp_ref, hs_ref):
    wg = wg_ref[...].astype(BF16)
    wu = wu_ref[...].astype(BF16)
    for x_ref, h_ref in ((xp_ref, hp_ref), (xs_ref, hs_ref)):
        x = x_ref[...]
        h_ref[...] = (jax.nn.silu(jnp.dot(x, wg, preferred_element_type=F32))
                      * jnp.dot(x, wu, preferred_element_type=F32)).astype(h_ref.dtype)


def _ffn_down_kernel(hp_ref, hs_ref, vp_ref, vs_ref, wd_ref, op_ref, os_ref):
    wd = wd_ref[...].astype(BF16)
    for h_ref, v_ref, o_ref in ((hp_ref, vp_ref, op_ref), (hs_ref, vs_ref, os_ref)):
        o_ref[...] = (jnp.dot(h_ref[...], wd, preferred_element_type=F32) * v_ref[...]).astype(o_ref.dtype)


def _ffn(xsel_p, xsel_s, vals_p, vals_s, w_gate, w_up, w_down, layer):
    n_exp, rp, d = xsel_p.shape
    rs = xsel_s.shape[1]
    ff = w_gate.shape[-1]
    tf = _tile(ff, 512)
    tn = _tile(d, 512)

    def per_expert(r, w):
        return pl.BlockSpec((None, r, w), lambda e, j: (e, 0, 0))

    def col_tile(r, w):
        return pl.BlockSpec((None, r, w), lambda e, j: (e, 0, j))

    def w_tile(k, w):
        return pl.BlockSpec((None, None, k, w), lambda e, j: (layer, e, 0, j))

    hid_p, hid_s = pl.pallas_call(
        _ffn_hidden_kernel,
        grid=(n_exp, ff // tf),
        in_specs=[per_expert(rp, d), per_expert(rs, d), w_tile(d, tf), w_tile(d, tf)],
        out_specs=[col_tile(rp, tf), col_tile(rs, tf)],
        out_shape=[jax.ShapeDtypeStruct((n_exp, rp, ff), BF16), jax.ShapeDtypeStruct((n_exp, rs, ff), BF16)],
        compiler_params=_cparams(2),
    )(xsel_p, xsel_s, w_gate, w_up)
    return pl.pallas_call(
        _ffn_down_kernel,
        grid=(n_exp, d // tn),
        in_specs=[per_expert(rp, ff), per_expert(rs, ff), per_expert(rp, 1), per_expert(rs, 1),
                  w_tile(ff, tn)],
        out_specs=[col_tile(rp, tn), col_tile(rs, tn)],
        out_shape=[jax.ShapeDtypeStruct((n_exp, rp, d), BF16), jax.ShapeDtypeStruct((n_exp, rs, d), BF16)],
        compiler_params=_cparams(2),
    )(hid_p, hid_s, vals_p, vals_s, w_down)


def _scatter_kernel(out_ref, rank_ref, x_ref, g2_ref, lg_ref, lb_ref, *rest, alpha, emit_h):
    n_exp, cap, d = out_ref.shape
    n = x_ref.shape[0]
    slot = lax.broadcasted_iota(jnp.int32, (cap, n), 0)
    pick = jnp.concatenate([(slot == rank_ref[e:e + 1, :]).astype(F32) for e in range(n_exp)], axis=0)
    f = jnp.dot(pick.T.astype(BF16), out_ref[...].reshape(n_exp * cap, d), preferred_element_type=F32)
    x2 = _ln(alpha * x_ref[...] + g2_ref[...] * f) * lg_ref[...] + lb_ref[...]
    if emit_h:
        sc_ref, sh_ref, x2_ref, h_ref = rest
        h_ref[...] = (_ln(x2) * (1.0 + sc_ref[...]) + sh_ref[...]).astype(h_ref.dtype)
    else:
        x2_ref, = rest
    x2_ref[...] = x2


def _scatter(out, rank, x1, mod4, mod_row, ln_g, ln_b, layer, alpha, next_mod4, *, batch, seq):
    n_exp, _, d = out.shape
    cap = out.shape[1] // batch
    blocks_per_seq = seq // ROW_BLOCK
    emit_h = next_mod4 is not None

    ts = _tile(seq, 512)
    nts = seq // ts

    def mod(chunk):
        return pl.BlockSpec((None, None, 1, d), lambda b, j: (mod_row(b * blocks_per_seq), chunk, 0, 0))

    rows = pl.BlockSpec((ts, d), lambda b, j: (b * nts + j, 0))
    in_specs = [pl.BlockSpec((n_exp, cap, d), lambda b, j: (0, b, 0)),
                pl.BlockSpec((None, n_exp, ts), lambda b, j: (b, 0, j)),
                rows, mod(5),
                pl.BlockSpec((None, 1, d), lambda b, j: (layer, 0, 0)),
                pl.BlockSpec((None, 1, d), lambda b, j: (layer, 0, 0))]
    args = [out, rank, x1, mod4, ln_g, ln_b]
    out_specs = [rows]
    out_shape = [jax.ShapeDtypeStruct((batch * seq, d), F32)]
    if emit_h:
        in_specs += [mod(1), mod(0)]
        args += [next_mod4, next_mod4]
        out_specs.append(rows)
        out_shape.append(jax.ShapeDtypeStruct((batch * seq, d), BF16))
    res = pl.pallas_call(
        functools.partial(_scatter_kernel, alpha=alpha, emit_h=emit_h),
        grid=(batch, nts),
        in_specs=in_specs,
        out_specs=out_specs,
        out_shape=out_shape,
        compiler_params=_cparams(2),
    )(*args)
    return (res[0], res[1]) if emit_h else (res[0], None)


def _mixers(x, h, mod4, p, st, layer):
    batch, seq = st["batch"], st["seq"]
    dims = p["dims"]
    hw, kvw, hg, kvg, ssm_w = dims["heads_win"], dims["kv_win"], dims["heads_glob"], dims["kv_glob"], dims["ssm_w"]
    wa = (hw + 2 * kvw) * HEAD_DIM
    wb = (hg + 2 * kvg) * HEAD_DIM
    pa = _matmul(h, p["w_in"], layer, 0, wa + ssm_w, tn=(wa + ssm_w) // 2)
    pb = _matmul(h, p["w_in"], layer, wa + ssm_w, wb)
    gates = _matmul(h, p["w_in"], layer, wa + ssm_w + wb, p["w_in"].shape[-1] - wa - ssm_w - wb,
                    out_dtype=BF16, tn=1024)
    k_w0, v_w0 = hw * HEAD_DIM, (hw + kvw) * HEAD_DIM
    k_g0, v_g0 = hg * HEAD_DIM, (hg + kvg) * HEAD_DIM
    win = dict(heads=hw, kv=kvw, batch=batch, seq=seq, sink=p["win_sink"])
    glob = dict(heads=hg, kv=kvg, batch=batch, seq=seq, q_norm=p["q_norm"], k_norm=p["k_norm"])
    if st["latent"]:
        o_win = _attention(pa, 0, pa, k_w0, pa, v_w0, k_ctx=st["ck_w"], v_ctx=st["cv_w"],
                           rope=True, band=True, **win)
        o_glob = _attention(pb, 0, pb, k_g0, pb, v_g0, k_ctx=st["ck_g"], v_ctx=st["cv_g"], rope=True, **glob)
        k_g = None
    else:
        o_win = _attention(pa, 0, pa, k_w0, pa, v_w0, **win)
        o_glob, k_g = _attention(pb, 0, pb, k_g0, pb, v_g0, emit_k=True, **glob)
    y_ssm, fin = _s5_mix(pa, wa, ssm_w, *p["s5"], st["s0"], batch=batch, seq=seq)
    o_ssm = _glu(y_ssm, pa, wa, p["ssm_d"], p["ssm_w_glu"], layer)
    r = _merge(o_win, o_ssm, o_glob, gates, x, mod4, st["mod_row"], p["w_up_win"], p["w_up_ssm"],
               p["w_up_glob"], p["w_out"], layer, p["alpha"])
    extras = (pa[:, k_w0:v_w0], pa[:, v_w0:wa], k_g, pb[:, v_g0:v_g0 + kvg * HEAD_DIM], fin)
    return r, extras


def kernel(x_prompt, x_sample, cache_win_k, cache_win_v, cache_glob_k, cache_glob_v, state_ssm_re, state_ssm_im, c, c_ctx, w_mod, b_mod, w_in, win_sink, ssm_a_re, ssm_a_im, ssm_log_dt, ssm_b_re, ssm_b_im, ssm_c_re, ssm_c_im, ssm_d, ssm_w_glu, q_norm, k_norm, w_up_win, w_up_ssm, w_up_glob, w_out, ln1_g, ln1_b, ln2_g, ln2_b, router_w, exp_w_gate, exp_w_up, exp_w_down):
    bp, lp, d = x_prompt.shape
    bs, ls, _ = x_sample.shape
    depth = w_mod.shape[0]
    past = cache_win_k.shape[2]
    kv_win, kv_glob = cache_win_k.shape[3], cache_glob_k.shape[3]
    kvw, kvg = kv_win * HEAD_DIM, kv_glob * HEAD_DIM
    ssm_w = w_up_ssm.shape[1]
    n_exp = router_w.shape[-1]
    p_n = ssm_a_re.shape[-1]
    alpha = (2.0 * depth) ** 0.25
    dims = dict(heads_win=w_up_win.shape[1] // HEAD_DIM, kv_win=kv_win,
                heads_glob=w_up_glob.shape[1] // HEAD_DIM, kv_glob=kv_glob, ssm_w=ssm_w)
    assert lp % ROW_BLOCK == 0 and ls % ROW_BLOCK == 0 and 1 + bs <= 8

    cond = jnp.zeros((8, d), F32).at[0].set(c_ctx).at[1:1 + bs].set(c)
    sample_blocks = ls // ROW_BLOCK
    streams = [
        dict(batch=bp, seq=lp, latent=False, mod_row=lambda i: 0,
             s0=jnp.zeros((ssm_w // LANES, bp, 4 * S5_OCT * p_n), F32)),
        dict(batch=bs, seq=ls, latent=True, mod_row=lambda i: 1 + i // sample_blocks),
    ]
    xs = [x_prompt.reshape(bp * lp, d), x_sample.reshape(bs * ls, d)]
    new = {k: [] for k in ("wk", "wv", "gk", "gv", "sre", "sim")}

    lay = _s5_layouts(ssm_log_dt, ssm_a_re, ssm_a_im, ssm_b_re, ssm_b_im, ssm_c_re, ssm_c_im)
    p = dict(w_in=w_in, dims=dims, ssm_d=ssm_d.reshape(depth, 1, ssm_w),
             ssm_w_glu=ssm_w_glu.astype(BF16), w_up_win=w_up_win.astype(BF16),
             w_up_ssm=w_up_ssm.astype(BF16), w_up_glob=w_up_glob.astype(BF16))
    p.update(w_out=w_out.astype(BF16), alpha=alpha)
    router_pad = jnp.pad(router_w.astype(F32), ((0, 0), (0, 0), (0, LANES - n_exp)))
    router_hi = router_pad.astype(BF16)
    router_lo = (router_pad - router_hi.astype(F32)).astype(BF16)
    b_mod3 = b_mod.reshape(depth, 1, 6 * d)
    ln1_g3, ln1_b3 = ln1_g.reshape(depth, 1, d), ln1_b.reshape(depth, 1, d)
    ln2_g3, ln2_b3 = ln2_g.reshape(depth, 1, d), ln2_b.reshape(depth, 1, d)

    mods = [_matmul(cond, w_mod, l, 0, 6 * d, bias=b_mod3, act="silu", tn=1024).reshape(8, 6, 1, d)
            for l in range(depth)]
    hs = [_ln_mod(x, mods[0], st["mod_row"]) for x, st in zip(xs, streams)]

    for l in range(depth):
        p.update(win_sink=win_sink[l], q_norm=q_norm[l], k_norm=k_norm[l], s5=_s5_prep(lay, l))
        mod4 = mods[l]
        next_mod4 = mods[l + 1] if l + 1 < depth else None
        streams[1].update(
            ck_w=cache_win_k[:, l].reshape(bs, past, kvw), cv_w=cache_win_v[:, l].reshape(bs, past, kvw),
            ck_g=cache_glob_k[:, l].reshape(bs, past, kvg), cv_g=cache_glob_v[:, l].reshape(bs, past, kvg),
            s0=_state_to_lanes(state_ssm_re[:, l], state_ssm_im[:, l]))

        routed = []
        for si, st in enumerate(streams):
            x = xs[si]
            r, extras = _mixers(x, hs[si], mod4, p, st, l)
            tm = _tile(st["seq"], 512) if st["latent"] else _tile(x.shape[0], 512)
            x1, h2, logits = _post_ln(r, mod4, st["mod_row"], ln1_g3, ln1_b3, router_hi, router_lo, l, tm)
            xsel, vals, rank = _route(logits, h2, batch=st["batch"], seq=st["seq"], n_exp=n_exp)
            routed.append((x1, xsel, vals, rank))
            if not st["latent"]:
                k_w, v_w, k_g, v_g, fin = extras
                new["wk"].append(k_w.reshape(bp, lp, -1, HEAD_DIM))
                new["wv"].append(v_w.reshape(bp, lp, -1, HEAD_DIM))
                new["gk"].append(k_g.reshape(bp, lp, -1, HEAD_DIM))
                new["gv"].append(v_g.reshape(bp, lp, -1, HEAD_DIM))
                s_re, s_im = _lanes_to_state(fin, p_n)
                new["sre"].append(s_re)
                new["sim"].append(s_im)
        outs = _ffn(routed[0][1], routed[1][1], routed[0][2], routed[1][2],
                    exp_w_gate, exp_w_up, exp_w_down, l)
        for si, st in enumerate(streams):
            xs[si], hs[si] = _scatter(outs[si], routed[si][3], routed[si][0], mod4, st["mod_row"],
                                      ln2_g3, ln2_b3, l, alpha, next_mod4, batch=st["batch"], seq=st["seq"])

    return (xs[0].reshape(bp, lp, d), xs[1].reshape(bs, ls, d),
            jnp.stack(new["wk"], axis=1), jnp.stack(new["wv"], axis=1),
            jnp.stack(new["gk"], axis=1), jnp.stack(new["gv"], axis=1),
            jnp.stack(new["sre"], axis=1), jnp.stack(new["sim"], axis=1))
```

```python
import functools

import numpy as np
import jax
import jax.numpy as jnp
from jax import lax
from jax.experimental import pallas as pl
from jax.experimental.pallas import tpu as pltpu

F32 = jnp.float32
BF16 = jnp.bfloat16
HIGHEST = lax.Precision.HIGHEST

HEAD_DIM = 128
LANES = 128
GRID_W = 64
WINDOW = 128
EC_FACTOR = 2
ROPE_THETA = 10000.0
EPS = 1e-6
NEG_INF = -1e30
ATTN_SCALE = HEAD_DIM ** -0.5
LOG2E = 1.4426950408889634
S5_CHUNK = 16
S5_OCT = 8
ROW_BLOCK = 256
GATHER_ROWS = 512
VMEM_LIMIT = 60 * 1024 * 1024
RESIDENT = pl.Buffered(1)


def _cparams(n_axes):
    return pltpu.CompilerParams(dimension_semantics=("arbitrary",) * n_axes,
                                vmem_limit_bytes=VMEM_LIMIT)


def _tile(dim, pref):
    return pref if dim % pref == 0 else dim


def _ln(x):
    mu = jnp.mean(x, axis=-1, keepdims=True)
    xc = x - mu
    var = jnp.mean(xc * xc, axis=-1, keepdims=True)
    return xc * lax.rsqrt(var + EPS)


def _split_bf16(x):
    hi = x.astype(BF16)
    return hi, (x - hi.astype(F32)).astype(BF16)


def _dot3(a_hi, a_lo, b_hi, b_lo, dims=(((1,), (0,)), ((), ()))):
    def d(x, y):
        return lax.dot_general(x, y, dims, preferred_element_type=F32)
    return d(a_hi, b_hi) + d(a_lo, b_hi) + d(a_hi, b_lo)


def _dot3_nt(a, b):
    return _dot3(*_split_bf16(a), *_split_bf16(b), dims=(((1,), (1,)), ((), ())))


def _mm_kernel(x_ref, w_ref, *rest, act, has_bias):
    if has_bias:
        b_ref, o_ref, wbf_ref = rest
    else:
        o_ref, wbf_ref = rest

    @pl.when(pl.program_id(1) == 0)
    def _():
        wbf_ref[...] = w_ref[...].astype(BF16)

    x = x_ref[...]
    if act == "silu":
        x = jax.nn.silu(x.astype(F32))
    acc = jnp.dot(x.astype(BF16), wbf_ref[...], preferred_element_type=F32)
    if has_bias:
        acc = acc + b_ref[...]
    o_ref[...] = acc.astype(o_ref.dtype)


def _matmul(x, w, layer, col0, ncols, *, bias=None, act=None, out_dtype=F32, tm=1024, tn=512):
    m, k = x.shape
    tm = _tile(m, tm)
    tn = next(t for t in (tn, 512, 256, LANES) if ncols % t == 0 and col0 % t == 0)
    c0 = col0 // tn
    in_specs = [pl.BlockSpec((tm, k), lambda n, i: (i, 0)),
                pl.BlockSpec((None, k, tn), lambda n, i: (layer, 0, c0 + n))]
    args = [x, w]
    if bias is not None:
        in_specs.append(pl.BlockSpec((None, 1, tn), lambda n, i: (layer, 0, c0 + n)))
        args.append(bias)
    return pl.pallas_call(
        functools.partial(_mm_kernel, act=act, has_bias=bias is not None),
        grid=(ncols // tn, m // tm),
        in_specs=in_specs,
        out_specs=pl.BlockSpec((tm, tn), lambda n, i: (i, n)),
        out_shape=jax.ShapeDtypeStruct((m, ncols), out_dtype),
        scratch_shapes=[pltpu.VMEM((k, tn), BF16)],
        compiler_params=_cparams(2),
    )(*args)


def _ln_mod_kernel(x_ref, sc_ref, sh_ref, h_ref):
    h_ref[...] = (_ln(x_ref[...]) * (1.0 + sc_ref[...]) + sh_ref[...]).astype(h_ref.dtype)


def _mod_spec(d, chunk, mod_row, rows_per_step=ROW_BLOCK):
    scale = rows_per_step // ROW_BLOCK
    return pl.BlockSpec((None, None, 1, d), lambda i: (mod_row(i * scale), chunk, 0, 0))


def _ln_mod(x, mod4, mod_row):
    t, d = x.shape
    return pl.pallas_call(
        _ln_mod_kernel,
        grid=(t // ROW_BLOCK,),
        in_specs=[pl.BlockSpec((ROW_BLOCK, d), lambda i: (i, 0)),
                  _mod_spec(d, 1, mod_row), _mod_spec(d, 0, mod_row)],
        out_specs=pl.BlockSpec((ROW_BLOCK, d), lambda i: (i, 0)),
        out_shape=jax.ShapeDtypeStruct((t, d), BF16),
        compiler_params=_cparams(1),
    )(x, mod4, mod4)


def _rope_tables(seq_len):
    half = HEAD_DIM // 4
    inv = ROPE_THETA ** (-np.arange(half, dtype=np.float64) / half)
    tok = np.arange(seq_len)
    ang_r = (tok // GRID_W)[:, None] * inv[None, :]
    ang_c = (tok % GRID_W)[:, None] * inv[None, :]
    cos = np.concatenate([np.cos(ang_r), np.cos(ang_r), np.cos(ang_c), np.cos(ang_c)], axis=-1)
    sin = np.concatenate([-np.sin(ang_r), np.sin(ang_r), -np.sin(ang_c), np.sin(ang_c)], axis=-1)
    return jnp.asarray(cos, F32), jnp.asarray(sin, F32)


def _rope(x, cos, sin_signed):
    lane = lax.broadcasted_iota(jnp.int32, x.shape, 1)
    swapped = jnp.where((lane % 64) < 32, pltpu.roll(x, 96, 1), pltpu.roll(x, 32, 1))
    return x * cos + swapped * sin_signed


def _rms(x, w):
    return x * lax.rsqrt(jnp.mean(x * x, axis=-1, keepdims=True) + EPS) * w


def _attn_kernel(*refs, heads, kv, tq, rope, norm, sink, ctx, band, emit_k):
    it = iter(refs)
    q_ref, k_ref, v_ref = next(it), next(it), next(it)
    kc_ref = vc_ref = sink_ref = qn_ref = kn_ref = cq_ref = sq_ref = ck_ref = sk_ref = None
    if ctx:
        kc_ref, vc_ref = next(it), next(it)
    if sink:
        sink_ref = next(it)
    if norm:
        qn_ref, kn_ref = next(it), next(it)
    if rope:
        cq_ref, sq_ref, ck_ref, sk_ref = next(it), next(it), next(it), next(it)
    o_ref = next(it)
    ko_ref = next(it) if emit_k else None
    kp_ref, vp_ref = next(it), next(it)

    qi = pl.program_id(1)
    rep = heads // kv

    @pl.when(qi == 0)
    def _():
        for g in range(kv):
            sl = slice(g * HEAD_DIM, (g + 1) * HEAD_DIM)
            kg = k_ref[:, sl]
            if norm:
                kg = _rms(kg, kn_ref[...])
            if rope:
                kg = _rope(kg, ck_ref[...], sk_ref[...])
            if emit_k:
                ko_ref[:, sl] = kg
            kp_ref[:, sl] = kg.astype(BF16)
        vp_ref[...] = v_ref[...].astype(BF16)

    nt = (((1,), (1,)), ((), ()))
    seq = kp_ref.shape[0]
    if band:
        span = min(seq, tq + 2 * max(WINDOW, tq))
        k0 = pl.multiple_of(jnp.clip(qi * tq - max(WINDOW, tq), 0, seq - span), tq)
        rows = pl.ds(k0, span)
        qpos = qi * tq + lax.broadcasted_iota(jnp.int32, (tq, span), 0)
        kpos = k0 + lax.broadcasted_iota(jnp.int32, (tq, span), 1)
        in_band = jnp.abs(qpos - kpos) <= WINDOW
    else:
        rows = slice(None)
    for h in range(heads):
        g = h // rep
        gs = slice(g * HEAD_DIM, (g + 1) * HEAD_DIM)
        qh = q_ref[:, h * HEAD_DIM:(h + 1) * HEAD_DIM]
        if norm:
            qh = _rms(qh, qn_ref[...])
        if rope:
            qh = _rope(qh, cq_ref[...], sq_ref[...])
        qh = (qh * (ATTN_SCALE * LOG2E)).astype(BF16)
        s = lax.dot_general(qh, kp_ref[rows, gs], nt, preferred_element_type=F32)
        if band:
            s = jnp.where(in_band, s, NEG_INF)
        m = jnp.max(s, axis=-1, keepdims=True)
        if ctx:
            sc = lax.dot_general(qh, kc_ref[:, gs].astype(BF16), nt, preferred_element_type=F32)
            m = jnp.maximum(m, jnp.max(sc, axis=-1, keepdims=True))
        if sink:
            m = jnp.maximum(m, sink_ref[h] * LOG2E)
        e = jnp.exp2(s - m)
        den = jnp.sum(e, axis=-1, keepdims=True)
        o = jnp.dot(e.astype(BF16), vp_ref[rows, gs], preferred_element_type=F32)
        if ctx:
            ec = jnp.exp2(sc - m)
            den = den + jnp.sum(ec, axis=-1, keepdims=True)
            o = o + jnp.dot(ec.astype(BF16), vc_ref[:, gs].astype(BF16), preferred_element_type=F32)
        if sink:
            den = den + jnp.exp2(sink_ref[h] * LOG2E - m)
        o_ref[:, h * HEAD_DIM:(h + 1) * HEAD_DIM] = (o / den).astype(o_ref.dtype)


def _attention(qa, q0, ka, k0, va, v0, *, heads, kv, batch, seq, k_ctx=None, v_ctx=None, sink=None,
               q_norm=None, k_norm=None, rope=False, band=False, emit_k=False):
    t = qa.shape[0]
    qw, kw = heads * HEAD_DIM, kv * HEAD_DIM
    assert q0 % qw == 0 and k0 % kw == 0 and v0 % kw == 0
    qc, kc, vc = q0 // qw, k0 // kw, v0 // kw
    tq = _tile(seq, 256)
    nq = seq // tq
    ctx, has_sink, norm = k_ctx is not None, sink is not None, q_norm is not None
    in_specs = [pl.BlockSpec((tq, qw), lambda b, i: (b * nq + i, qc)),
                pl.BlockSpec((seq, kw), lambda b, i: (b, kc)),
                pl.BlockSpec((seq, kw), lambda b, i: (b, vc))]
    args = [qa, ka, va]
    if ctx:
        lc = k_ctx.shape[1]
        in_specs += [pl.BlockSpec((None, lc, kw), lambda b, i: (b, 0, 0))] * 2
        args += [k_ctx, v_ctx]
    if has_sink:
        in_specs.append(pl.BlockSpec(memory_space=pltpu.SMEM))
        args.append(sink)
    if norm:
        in_specs += [pl.BlockSpec((1, HEAD_DIM), lambda b, i: (0, 0))] * 2
        args += [q_norm.reshape(1, HEAD_DIM), k_norm.reshape(1, HEAD_DIM)]
    if rope:
        cos, sin = _rope_tables(seq)
        in_specs += [pl.BlockSpec((tq, HEAD_DIM), lambda b, i: (i, 0))] * 2
        in_specs += [pl.BlockSpec((seq, HEAD_DIM), lambda b, i: (0, 0))] * 2
        args += [cos, sin, cos, sin]
    out_specs = [pl.BlockSpec((tq, qw), lambda b, i: (b * nq + i, 0))]
    out_shape = [jax.ShapeDtypeStruct((t, qw), BF16)]
    if emit_k:
        out_specs.append(pl.BlockSpec((seq, kw), lambda b, i: (b, 0)))
        out_shape.append(jax.ShapeDtypeStruct((t, kw), F32))
    res = pl.pallas_call(
        functools.partial(_attn_kernel, heads=heads, kv=kv, tq=tq, rope=rope, norm=norm,
                          sink=has_sink, ctx=ctx, band=band, emit_k=emit_k),
        grid=(batch, nq),
        in_specs=in_specs,
        out_specs=out_specs,
        out_shape=out_shape,
        scratch_shapes=[pltpu.VMEM((seq, kw), BF16), pltpu.VMEM((seq, kw), BF16)],
        compiler_params=_cparams(2),
    )(*args)
    return res if emit_k else res[0]


def _s5_layouts(log_dt, a_re, a_im, b_re, b_im, c_re, c_im):
    dep, _, g_n, p_n = a_re.shape
    gc = b_re.shape[-1]
    no = g_n // S5_OCT
    sw = S5_OCT * p_n
    ldt = jnp.broadcast_to(log_dt[..., None], a_re.shape)
    prm = jnp.stack([ldt, a_re, a_im], axis=2).astype(F32)
    prow = jnp.pad(prm.reshape(dep, 2, 3, no, sw).transpose(0, 1, 3, 2, 4),
                   ((0, 0),) * 3 + ((0, 5), (0, 0)))
    pcmp = jnp.broadcast_to(prm[:, :, :, :, None, :], (dep, 2, 3, g_n, gc, p_n))
    pcmp = pcmp.reshape(dep, 2, 3, no, S5_OCT * gc, p_n).transpose(0, 1, 3, 2, 4, 5)

    def b_views(b):
        bt = jnp.swapaxes(b.astype(F32), -1, -2).reshape(dep, 2, no, S5_OCT * gc, p_n)
        return bt, jnp.tile(bt, (1, 1, 1, 1, S5_OCT))

    def c_views(c):
        cc = c.astype(F32).reshape(dep, 2, no, S5_OCT * gc, p_n)
        ct = jnp.swapaxes(c.astype(F32), -1, -2).reshape(dep, 2, no, sw, gc)
        return cc, jnp.tile(ct, (1, 1, 1, 1, S5_OCT))

    bc_re, bt_re = b_views(b_re)
    bc_im, bt_im = b_views(b_im)
    cc_re, ct_re = c_views(c_re)
    cc_im, ct_im = c_views(c_im)
    return dict(prow=prow, pcmp=pcmp, bc_re=bc_re, bc_im=bc_im, bt_re=bt_re, bt_im=bt_im,
                cc_re=cc_re, cc_im=cc_im, ct_re=ct_re, ct_im=ct_im)


def _lam_powers(ldt, ar, ai, k):
    dt = jnp.exp(ldt)
    mag = jnp.exp(k * (dt * ar))
    return mag * jnp.cos(k * (dt * ai)), mag * jnp.sin(k * (dt * ai))


def _bbar_coef(ldt, ar, ai):
    lam_re, lam_im = _lam_powers(ldt, ar, ai, 1.0)
    den = ar * ar + ai * ai
    return ((lam_re - 1.0) * ar + lam_im * ai) / den, (lam_im * ar - (lam_re - 1.0) * ai) / den


def _same_group(shape, rows_per_group, cols_per_group):
    r = lax.broadcasted_iota(jnp.int32, shape, 0) // rows_per_group
    c = lax.broadcasted_iota(jnp.int32, shape, 1) // cols_per_group
    return r == c


def _s5_prep_w_kernel(prow_ref, btr_ref, bti_ref, w_ref, lam_ref):
    tc = S5_CHUNK
    rows, sw = btr_ref.shape[1], btr_ref.shape[2]
    mask = _same_group((rows, sw), rows // S5_OCT, sw // S5_OCT)
    k = lax.broadcasted_iota(jnp.int32, (24, sw), 0).astype(F32)
    for d in range(2):
        ldt, ar, ai = prow_ref[d, 0:1, :], prow_ref[d, 1:2, :], prow_ref[d, 2:3, :]
        pw_re, pw_im = _lam_powers(ldt, ar, ai, k)
        co_re, co_im = _bbar_coef(ldt, ar, ai)
        bb_re = jnp.where(mask, co_re * btr_ref[d] - co_im * bti_ref[d], 0.0)
        bb_im = jnp.where(mask, co_re * bti_ref[d] + co_im * btr_ref[d], 0.0)
        lam_ref[2 * d:2 * d + 1, :] = pw_re[tc:tc + 1]
        lam_ref[2 * d + 1:2 * d + 2, :] = pw_im[tc:tc + 1]
        for j in range(tc):
            kk = tc - 1 - j if d == 0 else j
            pr, pi = pw_re[kk:kk + 1], pw_im[kk:kk + 1]
            rs = slice(j * rows, (j + 1) * rows)
            w_ref[rs, (2 * d) * sw:(2 * d + 1) * sw] = (pr * bb_re - pi * bb_im).astype(w_ref.dtype)
            w_ref[rs, (2 * d + 1) * sw:(2 * d + 2) * sw] = (pr * bb_im + pi * bb_re).astype(w_ref.dtype)


def _row_to_col(row, eye):
    return jnp.sum(jnp.where(eye, row, 0.0), axis=1, keepdims=True)


def _s5_prep_z_kernel(prow_ref, ctr_ref, cti_ref, z_ref):
    tc = S5_CHUNK
    sw, cols = ctr_ref.shape[1], ctr_ref.shape[2]
    mask = _same_group((sw, cols), sw // S5_OCT, cols // S5_OCT)
    eye = _same_group((sw, sw), 1, 1)
    for d in range(2):
        lam_re, lam_im = _lam_powers(prow_ref[d, 0:1, :], prow_ref[d, 1:2, :], prow_ref[d, 2:3, :], 1.0)
        lr = jnp.broadcast_to(_row_to_col(lam_re, eye), (sw, cols))
        li = jnp.broadcast_to(_row_to_col(lam_im, eye), (sw, cols))
        c_re = jnp.where(mask, ctr_ref[d], 0.0)
        c_im = jnp.where(mask, cti_ref[d], 0.0)
        z_re, z_im = c_re * lr - c_im * li, c_re * li + c_im * lr
        for step in range(tc):
            t = step if d == 0 else tc - 1 - step
            cs = slice(t * cols, (t + 1) * cols)
            z_ref[(2 * d) * sw:(2 * d + 1) * sw, cs] = z_re.astype(z_ref.dtype)
            z_ref[(2 * d + 1) * sw:(2 * d + 2) * sw, cs] = (-z_im).astype(z_ref.dtype)
            z_re, z_im = z_re * lr - z_im * li, z_re * li + z_im * lr


def _s5_prep_m_kernel(pcmp_ref, bcr_ref, bci_ref, ccr_ref, cci_ref, m_ref, xr_ref, xi_ref, taps_ref):
    tc = S5_CHUNK
    rows = bcr_ref.shape[1]
    mask = _same_group((rows, rows), rows // S5_OCT, rows // S5_OCT)
    for d in range(2):
        ldt, ar, ai = pcmp_ref[d, 0], pcmp_ref[d, 1], pcmp_ref[d, 2]
        lam_re, lam_im = _lam_powers(ldt, ar, ai, 1.0)
        co_re, co_im = _bbar_coef(ldt, ar, ai)
        x_re = co_re * bcr_ref[d] - co_im * bci_ref[d]
        x_im = co_re * bci_ref[d] + co_im * bcr_ref[d]
        for kk in range(tc):
            xr_ref[kk * rows:(kk + 1) * rows, :] = x_re
            xi_ref[kk * rows:(kk + 1) * rows, :] = x_im
            x_re, x_im = x_re * lam_re - x_im * lam_im, x_re * lam_im + x_im * lam_re
        taps = _dot3_nt(xr_ref[...], ccr_ref[d]) - _dot3_nt(xi_ref[...], cci_ref[d])
        for kk in range(tc):
            tile = jnp.where(mask, taps[kk * rows:(kk + 1) * rows, :], 0.0)
            col = (tc - 1 + kk) if d == 0 else (tc - 1 - kk)
            cs = slice(col * rows, (col + 1) * rows)
            if d == 1 and kk == 0:
                taps_ref[:, cs] += tile
            else:
                taps_ref[:, cs] = tile
    for j in range(tc):
        m_ref[j * rows:(j + 1) * rows, :] = taps_ref[:, (tc - 1 - j) * rows:(2 * tc - 1 - j) * rows].astype(m_ref.dtype)


def _s5_prep(lay, layer):
    _, _, no, rows, sw = lay["bt_re"].shape
    p_n = lay["bc_re"].shape[-1]
    tc = S5_CHUNK
    kw = tc * rows

    def per_octet(*tail):
        nd = len(tail)
        return pl.BlockSpec((None, 2, None) + tail, lambda o: (layer, 0, o) + (0,) * nd)

    def out(r, c):
        return pl.BlockSpec((None, r, c), lambda o: (o, 0, 0))

    w_o, lam_o = pl.pallas_call(
        _s5_prep_w_kernel, grid=(no,),
        in_specs=[per_octet(8, sw), per_octet(rows, sw), per_octet(rows, sw)],
        out_specs=[out(kw, 4 * sw), out(4, sw)],
        out_shape=[jax.ShapeDtypeStruct((no, kw, 4 * sw), BF16), jax.ShapeDtypeStruct((no, 4, sw), F32)],
        compiler_params=_cparams(1),
    )(lay["prow"], lay["bt_re"], lay["bt_im"])
    z_o = pl.pallas_call(
        _s5_prep_z_kernel, grid=(no,),
        in_specs=[per_octet(8, sw), per_octet(sw, rows), per_octet(sw, rows)],
        out_specs=out(4 * sw, kw),
        out_shape=jax.ShapeDtypeStruct((no, 4 * sw, kw), BF16),
        compiler_params=_cparams(1),
    )(lay["prow"], lay["ct_re"], lay["ct_im"])
    m_o = pl.pallas_call(
        _s5_prep_m_kernel, grid=(no,),
        in_specs=[per_octet(3, rows, p_n)] + [per_octet(rows, p_n)] * 4,
        out_specs=out(kw, kw),
        out_shape=jax.ShapeDtypeStruct((no, kw, kw), BF16),
        scratch_shapes=[pltpu.VMEM((kw, p_n), F32), pltpu.VMEM((kw, p_n), F32),
                        pltpu.VMEM((rows, (2 * tc - 1) * rows), F32)],
        compiler_params=_cparams(1),
    )(lay["pcmp"], lay["bc_re"], lay["bc_im"], lay["cc_re"], lay["cc_im"])
    return m_o, w_o, z_o, lam_o


def _s5_state_kernel(u_ref, w_ref, lam_ref, s0_ref, sp_ref, fin_ref, ubf_ref, vs_ref, *, batch, nchunk):
    tc = S5_CHUNK
    r = batch * nchunk
    for t in range(tc):
        ubf_ref[:, t * LANES:(t + 1) * LANES] = u_ref[pl.ds(t, r, stride=tc), :].astype(BF16)
    q = vs_ref.shape[-1] // 4
    vs_ref[...] = jnp.dot(ubf_ref[...], w_ref[...],
                          preferred_element_type=F32).reshape(batch, nchunk, 4 * q)

    def scan(part, order):
        a, b = lam_ref[part:part + 1, :], lam_ref[part + 1:part + 2, :]
        re, im = slice(part * q, (part + 1) * q), slice((part + 1) * q, (part + 2) * q)
        s_re, s_im = s0_ref[:, re], s0_ref[:, im]
        for c in order:
            v_re, v_im = vs_ref[:, c, re], vs_ref[:, c, im]
            vs_ref[:, c, re] = s_re
            vs_ref[:, c, im] = s_im
            s_re, s_im = a * s_re - b * s_im + v_re, a * s_im + b * s_re + v_im
        fin_ref[:, re] = s_re
        fin_ref[:, im] = s_im

    scan(0, range(nchunk))
    scan(2, range(nchunk - 1, -1, -1))
    sp_ref[...] = vs_ref[...].reshape(r, 4 * q).astype(sp_ref.dtype)


def _s5_out_kernel(ubf_ref, sp_ref, m_ref, z_ref, y_ref):
    tc = S5_CHUNK
    tr = ubf_ref.shape[0]
    y = (jnp.dot(ubf_ref[...], m_ref[...], preferred_element_type=F32)
         + jnp.dot(sp_ref[...], z_ref[...], preferred_element_type=F32))
    for t in range(tc):
        y_ref[pl.ds(t, tr, stride=tc), :] = y[:, t * LANES:(t + 1) * LANES]


def _s5_mix(ua, u0, width, m_o, w_o, z_o, lam_o, s0, *, batch, seq):
    t = ua.shape[0]
    tc = S5_CHUNK
    no = width // LANES
    assert u0 % LANES == 0
    uc = u0 // LANES
    r = t // tc
    nchunk = seq // tc
    sw = w_o.shape[-1]
    kw = tc * LANES
    sp, fin, ubf = pl.pallas_call(
        functools.partial(_s5_state_kernel, batch=batch, nchunk=nchunk),
        grid=(no,),
        in_specs=[pl.BlockSpec((t, LANES), lambda o: (0, uc + o)),
                  pl.BlockSpec((None, kw, sw), lambda o: (o, 0, 0)),
                  pl.BlockSpec((None, 4, sw // 4), lambda o: (o, 0, 0)),
                  pl.BlockSpec((None, batch, sw), lambda o: (o, 0, 0))],
        out_specs=[pl.BlockSpec((None, r, sw), lambda o: (o, 0, 0)),
                   pl.BlockSpec((None, batch, sw), lambda o: (o, 0, 0)),
                   pl.BlockSpec((None, r, kw), lambda o: (o, 0, 0))],
        out_shape=[jax.ShapeDtypeStruct((no, r, sw), BF16),
                   jax.ShapeDtypeStruct((no, batch, sw), F32),
                   jax.ShapeDtypeStruct((no, r, kw), BF16)],
        scratch_shapes=[pltpu.VMEM((batch, nchunk, sw), F32)],
        compiler_params=_cparams(1),
    )(ua, w_o, lam_o, s0)
    tr = _tile(r, 512)
    y = pl.pallas_call(
        _s5_out_kernel,
        grid=(no, r // tr),
        in_specs=[pl.BlockSpec((None, tr, kw), lambda o, i: (o, i, 0)),
                  pl.BlockSpec((None, tr, sw), lambda o, i: (o, i, 0)),
                  pl.BlockSpec((None, kw, kw), lambda o, i: (o, 0, 0)),
                  pl.BlockSpec((None, sw, kw), lambda o, i: (o, 0, 0))],
        out_specs=pl.BlockSpec((tr * tc, LANES), lambda o, i: (i, o)),
        out_shape=jax.ShapeDtypeStruct((t, width), F32),
        compiler_params=_cparams(2),
    )(ubf, sp, m_o, z_o)
    return y, fin


def _state_to_lanes(s_re, s_im):
    b, _, g_n, p_n = s_re.shape
    no = g_n // S5_OCT
    parts = jnp.stack([s_re[:, 0], s_im[:, 0], s_re[:, 1], s_im[:, 1]], axis=1)
    parts = parts.reshape(b, 4, no, S5_OCT * p_n)
    return jnp.transpose(parts, (2, 0, 1, 3)).reshape(no, b, 4 * S5_OCT * p_n).astype(F32)


def _lanes_to_state(fin, p_n):
    no, b, _ = fin.shape
    parts = jnp.transpose(fin.reshape(no, b, 4, S5_OCT, p_n), (1, 2, 0, 3, 4)).reshape(b, 4, no * S5_OCT, p_n)
    return jnp.stack([parts[:, 0], parts[:, 2]], axis=1), jnp.stack([parts[:, 1], parts[:, 3]], axis=1)


def _glu_kernel(y_ref, ua_ref, ub_ref, d_ref, w_ref, o_ref):
    u = jnp.concatenate([ua_ref[...], ub_ref[...]], axis=1)
    z = jax.nn.gelu(y_ref[...] + d_ref[...] * u).astype(BF16)
    g = jnp.dot(z, w_ref[...], preferred_element_type=F32)
    w = o_ref.shape[-1]
    o_ref[...] = (g[:, :w] * jax.nn.sigmoid(g[:, w:])).astype(o_ref.dtype)


def _glu(y, ua, u0, d, w_glu, layer):
    t, width = y.shape
    half = width // 2
    assert u0 % half == 0
    uc = u0 // half
    tm = _tile(t, 512)
    return pl.pallas_call(
        _glu_kernel,
        grid=(t // tm,),
        in_specs=[pl.BlockSpec((tm, width), lambda i: (i, 0)),
                  pl.BlockSpec((tm, half), lambda i: (i, uc)),
                  pl.BlockSpec((tm, half), lambda i: (i, uc + 1)),
                  pl.BlockSpec((None, 1, width), lambda i: (layer, 0, 0)),
                  pl.BlockSpec((None, width, 2 * width), lambda i: (layer, 0, 0), pipeline_mode=RESIDENT)],
        out_specs=pl.BlockSpec((tm, width), lambda i: (i, 0)),
        out_shape=jax.ShapeDtypeStruct((t, width), BF16),
        compiler_params=_cparams(1),
    )(y, ua, ua, d, w_glu)


def _merge_kernel(ow_ref, os_ref, og_ref, gw_ref, gs_ref, gg_ref, x_ref, g1_ref,
                  ww_ref, ws_ref, wg_ref, wo_ref, r_ref, *, alpha):
    def gate(g_ref):
        return jax.nn.sigmoid(g_ref[...].astype(F32))

    y = (gate(gw_ref) * jnp.dot(ow_ref[...], ww_ref[...], preferred_element_type=F32)
         + gate(gs_ref) * jnp.dot(os_ref[...], ws_ref[...], preferred_element_type=F32)
         + gate(gg_ref) * jnp.dot(og_ref[...], wg_ref[...], preferred_element_type=F32))
    o = jnp.dot(y.astype(BF16), wo_ref[...], preferred_element_type=F32)
    r_ref[...] = alpha * x_ref[...] + g1_ref[...] * o


def _merge(o_win, o_ssm, o_glob, gates, x, mod4, mod_row, w_win, w_ssm, w_glob, w_out, layer, alpha, tm):
    t, d = x.shape

    def rows(w):
        return pl.BlockSpec((tm, w), lambda i: (i, 0))

    def whole(a):
        return pl.BlockSpec((None,) + a.shape[1:], lambda i: (layer, 0, 0), pipeline_mode=RESIDENT)

    return pl.pallas_call(
        functools.partial(_merge_kernel, alpha=alpha),
        grid=(t // tm,),
        in_specs=[rows(o_win.shape[1]), rows(o_ssm.shape[1]), rows(o_glob.shape[1]),
                  pl.BlockSpec((tm, d), lambda i: (i, 0)),
                  pl.BlockSpec((tm, d), lambda i: (i, 1)),
                  pl.BlockSpec((tm, d), lambda i: (i, 2)),
                  rows(d), _mod_spec(d, 2, mod_row, tm),
                  whole(w_win), whole(w_ssm), whole(w_glob), whole(w_out)],
        out_specs=rows(d),
        out_shape=jax.ShapeDtypeStruct((t, d), F32),
        compiler_params=_cparams(1),
    )(o_win, o_ssm, o_glob, gates, gates, gates, x, mod4, w_win, w_ssm, w_glob, w_out)


def _post_ln_kernel(r_ref, sc2_ref, sh2_ref, lg_ref, lb_ref, rwh_ref, rwl_ref, x1_ref, h2_ref, logit_ref):
    x1 = _ln(r_ref[...]) * lg_ref[...] + lb_ref[...]
    x1_ref[...] = x1
    h2 = _ln(x1) * (1.0 + sc2_ref[...]) + sh2_ref[...]
    hi, lo = _split_bf16(h2)
    h2_ref[...] = hi
    logit_ref[...] = _dot3(hi, lo, rwh_ref[...], rwl_ref[...])


def _post_ln(r, mod4, mod_row, ln_g, ln_b, router_hi, router_lo, layer, tm):
    t, d = r.shape

    def rows(w):
        return pl.BlockSpec((tm, w), lambda i: (i, 0))

    def vec():
        return pl.BlockSpec((None, 1, d), lambda i: (layer, 0, 0))

    def router():
        return pl.BlockSpec((None, d, LANES), lambda i: (layer, 0, 0), pipeline_mode=RESIDENT)

    return pl.pallas_call(
        _post_ln_kernel,
        grid=(t // tm,),
        in_specs=[rows(d), _mod_spec(d, 4, mod_row, tm), _mod_spec(d, 3, mod_row, tm), vec(), vec(),
                  router(), router()],
        out_specs=[rows(d), rows(d), rows(LANES)],
        out_shape=[jax.ShapeDtypeStruct((t, d), F32), jax.ShapeDtypeStruct((t, d), BF16),
                   jax.ShapeDtypeStruct((t, LANES), F32)],
        compiler_params=_cparams(1),
    )(r, mod4, mod4, ln_g, ln_b, router_hi, router_lo)


def _route_kernel(logit_ref, h_ref, xsel_ref, vals_ref, rank_ref, *, n_exp, cap):
    n = logit_ref.shape[0]
    lg = logit_ref[...]
    col = lax.broadcasted_iota(jnp.int32, lg.shape, 1)
    lg = jnp.where(col < n_exp, lg, -jnp.inf)
    ex = jnp.exp(lg - jnp.max(lg, axis=-1, keepdims=True))
    aff = ex / jnp.sum(ex, axis=-1, keepdims=True)
    aff_t = aff.T
    jj = lax.broadcasted_iota(jnp.int32, (n, n), 0)
    ii = lax.broadcasted_iota(jnp.int32, (n, n), 1)
    slot = lax.broadcasted_iota(jnp.int32, (cap, n), 0)
    picks = []
    for e in range(n_exp):
        a_row = aff_t[e:e + 1, :]
        a_col = aff[:, e:e + 1]
        beats = (a_col > a_row) | ((a_col == a_row) & (jj < ii))
        rank = jnp.sum(beats.astype(jnp.int32), axis=0, keepdims=True)
        pick = slot == rank
        picks.append(pick.astype(BF16))
        vals_ref[e] = jnp.sum(jnp.where(pick, a_row, 0.0), axis=1, keepdims=True)
        rank_ref[e:e + 1, :] = rank
    group = min(n_exp, max(1, GATHER_ROWS // cap))
    for e0 in range(0, n_exp, group):
        sel = jnp.dot(jnp.concatenate(picks[e0:e0 + group], axis=0), h_ref[...], preferred_element_type=F32)
        for e in range(e0, min(e0 + group, n_exp)):
            xsel_ref[e] = sel[(e - e0) * cap:(e - e0 + 1) * cap, :].astype(xsel_ref.dtype)


def _route(logits, h2, *, batch, seq, n_exp):
    t, d = h2.shape
    cap = EC_FACTOR * seq // n_exp
    return pl.pallas_call(
        functools.partial(_route_kernel, n_exp=n_exp, cap=cap),
        grid=(batch,),
        in_specs=[pl.BlockSpec((seq, LANES), lambda b: (b, 0)),
                  pl.BlockSpec((seq, d), lambda b: (b, 0))],
        out_specs=[pl.BlockSpec((n_exp, cap, d), lambda b: (0, b, 0)),
                   pl.BlockSpec((n_exp, cap, 1), lambda b: (0, b, 0)),
                   pl.BlockSpec((None, n_exp, seq), lambda b: (b, 0, 0))],
        out_shape=[jax.ShapeDtypeStruct((n_exp, batch * cap, d), BF16),
                   jax.ShapeDtypeStruct((n_exp, batch * cap, 1), F32),
                   jax.ShapeDtypeStruct((batch, n_exp, seq), jnp.int32)],
        compiler_params=_cparams(1),
    )(logits, h2)


def _ffn_hidden_kernel(xp_ref, xs_ref, wg_ref, wu_ref, hp_ref, hs_ref):
    wg = wg_ref[...].astype(BF16)
    wu = wu_ref[...].astype(BF16)
    for x_ref, h_ref in ((xp_ref, hp_ref), (xs_ref, hs_ref)):
        x = x_ref[...]
        h_ref[...] = (jax.nn.silu(jnp.dot(x, wg, preferred_element_type=F32))
                      * jnp.dot(x, wu, preferred_element_type=F32)).astype(h_ref.dtype)


def _ffn_down_kernel(hp_ref, hs_ref, vp_ref, vs_ref, wd_ref, op_ref, os_ref):
    wd = wd_ref[...].astype(BF16)
    for h_ref, v_ref, o_ref in ((hp_ref, vp_ref, op_ref), (hs_ref, vs_ref, os_ref)):
        o_ref[...] = (jnp.dot(h_ref[...], wd, preferred_element_type=F32) * v_ref[...]).astype(o_ref.dtype)


def _ffn(xsel_p, xsel_s, vals_p, vals_s, w_gate, w_up, w_down, layer):
    n_exp, rp, d = xsel_p.shape
    rs = xsel_s.shape[1]
    ff = w_gate.shape[-1]
    tf = _tile(ff, 512)
    tn = _tile(d, 512)

    def per_expert(r, w):
        return pl.BlockSpec((None, r, w), lambda e, j: (e, 0, 0))

    def col_tile(r, w):
        return pl.BlockSpec((None, r, w), lambda e, j: (e, 0, j))

    def w_tile(k, w):
        return pl.BlockSpec((None, None, k, w), lambda e, j: (layer, e, 0, j))

    hid_p, hid_s = pl.pallas_call(
        _ffn_hidden_kernel,
        grid=(n_exp, ff // tf),
        in_specs=[per_expert(rp, d), per_expert(rs, d), w_tile(d, tf), w_tile(d, tf)],
        out_specs=[col_tile(rp, tf), col_tile(rs, tf)],
        out_shape=[jax.ShapeDtypeStruct((n_exp, rp, ff), BF16), jax.ShapeDtypeStruct((n_exp, rs, ff), BF16)],
        compiler_params=_cparams(2),
    )(xsel_p, xsel_s, w_gate, w_up)
    return pl.pallas_call(
        _ffn_down_kernel,
        grid=(n_exp, d // tn),
        in_specs=[per_expert(rp, ff), per_expert(rs, ff), per_expert(rp, 1), per_expert(rs, 1),
                  w_tile(ff, tn)],
        out_specs=[col_tile(rp, tn), col_tile(rs, tn)],
        out_shape=[jax.ShapeDtypeStruct((n_exp, rp, d), BF16), jax.ShapeDtypeStruct((n_exp, rs, d), BF16)],
        compiler_params=_cparams(2),
    )(hid_p, hid_s, vals_p, vals_s, w_down)


def _scatter_kernel(out_ref, rank_ref, x_ref, g2_ref, lg_ref, lb_ref, *rest, alpha, emit_h):
    n_exp, cap, d = out_ref.shape
    n = x_ref.shape[0]
    slot = lax.broadcasted_iota(jnp.int32, (cap, n), 0)
    pick = jnp.concatenate([(slot == rank_ref[e:e + 1, :]).astype(F32) for e in range(n_exp)], axis=0)
    f = jnp.dot(pick.T.astype(BF16), out_ref[...].reshape(n_exp * cap, d), preferred_element_type=F32)
    x2 = _ln(alpha * x_ref[...] + g2_ref[...] * f) * lg_ref[...] + lb_ref[...]
    if emit_h:
        sc_ref, sh_ref, x2_ref, h_ref = rest
        h_ref[...] = (_ln(x2) * (1.0 + sc_ref[...]) + sh_ref[...]).astype(h_ref.dtype)
    else:
        x2_ref, = rest
    x2_ref[...] = x2


def _scatter(out, rank, x1, mod4, mod_row, ln_g, ln_b, layer, alpha, next_mod4, *, batch, seq):
    n_exp, _, d = out.shape
    cap = out.shape[1] // batch
    blocks_per_seq = seq // ROW_BLOCK
    emit_h = next_mod4 is not None

    ts = _tile(seq, 512)
    nts = seq // ts

    def mod(chunk):
        return pl.BlockSpec((None, None, 1, d), lambda b, j: (mod_row(b * blocks_per_seq), chunk, 0, 0))

    rows = pl.BlockSpec((ts, d), lambda b, j: (b * nts + j, 0))
    in_specs = [pl.BlockSpec((n_exp, cap, d), lambda b, j: (0, b, 0)),
                pl.BlockSpec((None, n_exp, ts), lambda b, j: (b, 0, j)),
                rows, mod(5),
                pl.BlockSpec((None, 1, d), lambda b, j: (layer, 0, 0)),
                pl.BlockSpec((None, 1, d), lambda b, j: (layer, 0, 0))]
    args = [out, rank, x1, mod4, ln_g, ln_b]
    out_specs = [rows]
    out_shape = [jax.ShapeDtypeStruct((batch * seq, d), F32)]
    if emit_h:
        in_specs += [mod(1), mod(0)]
        args += [next_mod4, next_mod4]
        out_specs.append(rows)
        out_shape.append(jax.ShapeDtypeStruct((batch * seq, d), BF16))
    res = pl.pallas_call(
        functools.partial(_scatter_kernel, alpha=alpha, emit_h=emit_h),
        grid=(batch, nts),
        in_specs=in_specs,
        out_specs=out_specs,
        out_shape=out_shape,
        compiler_params=_cparams(2),
    )(*args)
    return (res[0], res[1]) if emit_h else (res[0], None)


def _mixers(x, h, mod4, p, st, layer):
    batch, seq = st["batch"], st["seq"]
    dims = p["dims"]
    hw, kvw, hg, kvg, ssm_w = dims["heads_win"], dims["kv_win"], dims["heads_glob"], dims["kv_glob"], dims["ssm_w"]
    wa = (hw + 2 * kvw) * HEAD_DIM
    wb = (hg + 2 * kvg) * HEAD_DIM
    pa = _matmul(h, p["w_in"], layer, 0, wa + ssm_w, tn=(wa + ssm_w) // 2)
    pb = _matmul(h, p["w_in"], layer, wa + ssm_w, wb)
    gates = _matmul(h, p["w_in"], layer, wa + ssm_w + wb, p["w_in"].shape[-1] - wa - ssm_w - wb,
                    out_dtype=BF16, tn=1024)
    k_w0, v_w0 = hw * HEAD_DIM, (hw + kvw) * HEAD_DIM
    k_g0, v_g0 = hg * HEAD_DIM, (hg + kvg) * HEAD_DIM
    win = dict(heads=hw, kv=kvw, batch=batch, seq=seq, sink=p["win_sink"])
    glob = dict(heads=hg, kv=kvg, batch=batch, seq=seq, q_norm=p["q_norm"], k_norm=p["k_norm"])
    if st["latent"]:
        o_win = _attention(pa, 0, pa, k_w0, pa, v_w0, k_ctx=st["ck_w"], v_ctx=st["cv_w"],
                           rope=True, band=True, **win)
        o_glob = _attention(pb, 0, pb, k_g0, pb, v_g0, k_ctx=st["ck_g"], v_ctx=st["cv_g"], rope=True, **glob)
        k_g = None
    else:
        o_win = _attention(pa, 0, pa, k_w0, pa, v_w0, **win)
        o_glob, k_g = _attention(pb, 0, pb, k_g0, pb, v_g0, emit_k=True, **glob)
    y_ssm, fin = _s5_mix(pa, wa, ssm_w, *p["s5"], st["s0"], batch=batch, seq=seq)
    o_ssm = _glu(y_ssm, pa, wa, p["ssm_d"], p["ssm_w_glu"], layer)
    r = _merge(o_win, o_ssm, o_glob, gates, x, mod4, st["mod_row"], p["w_up_win"], p["w_up_ssm"],
               p["w_up_glob"], p["w_out"], layer, p["alpha"], st["tm"])
    extras = (pa[:, k_w0:v_w0], pa[:, v_w0:wa], k_g, pb[:, v_g0:v_g0 + kvg * HEAD_DIM], fin)
    return r, extras


def kernel(x_prompt, x_sample, cache_win_k, cache_win_v, cache_glob_k, cache_glob_v, state_ssm_re, state_ssm_im, c, c_ctx, w_mod, b_mod, w_in, win_sink, ssm_a_re, ssm_a_im, ssm_log_dt, ssm_b_re, ssm_b_im, ssm_c_re, ssm_c_im, ssm_d, ssm_w_glu, q_norm, k_norm, w_up_win, w_up_ssm, w_up_glob, w_out, ln1_g, ln1_b, ln2_g, ln2_b, router_w, exp_w_gate, exp_w_up, exp_w_down):
    bp, lp, d = x_prompt.shape
    bs, ls, _ = x_sample.shape
    depth = w_mod.shape[0]
    past = cache_win_k.shape[2]
    kv_win, kv_glob = cache_win_k.shape[3], cache_glob_k.shape[3]
    kvw, kvg = kv_win * HEAD_DIM, kv_glob * HEAD_DIM
    ssm_w = w_up_ssm.shape[1]
    n_exp = router_w.shape[-1]
    p_n = ssm_a_re.shape[-1]
    alpha = (2.0 * depth) ** 0.25
    dims = dict(heads_win=w_up_win.shape[1] // HEAD_DIM, kv_win=kv_win,
                heads_glob=w_up_glob.shape[1] // HEAD_DIM, kv_glob=kv_glob, ssm_w=ssm_w)
    assert lp % ROW_BLOCK == 0 and ls % ROW_BLOCK == 0 and 1 + bs <= 8

    cond = jnp.zeros((8, d), F32).at[0].set(c_ctx).at[1:1 + bs].set(c)
    sample_blocks = ls // ROW_BLOCK
    streams = [
        dict(batch=bp, seq=lp, latent=False, mod_row=lambda i: 0, tm=_tile(bp * lp, 512),
             s0=jnp.zeros((ssm_w // LANES, bp, 4 * S5_OCT * p_n), F32)),
        dict(batch=bs, seq=ls, latent=True, mod_row=lambda i: 1 + i // sample_blocks, tm=_tile(ls, 512)),
    ]
    xs = [x_prompt.reshape(bp * lp, d), x_sample.reshape(bs * ls, d)]
    new = {k: [] for k in ("wk", "wv", "gk", "gv", "sre", "sim")}

    lay = _s5_layouts(ssm_log_dt, ssm_a_re, ssm_a_im, ssm_b_re, ssm_b_im, ssm_c_re, ssm_c_im)
    p = dict(w_in=w_in, dims=dims, ssm_d=ssm_d.reshape(depth, 1, ssm_w),
             ssm_w_glu=ssm_w_glu.astype(BF16), w_up_win=w_up_win.astype(BF16),
             w_up_ssm=w_up_ssm.astype(BF16), w_up_glob=w_up_glob.astype(BF16))
    p.update(w_out=w_out.astype(BF16), alpha=alpha)
    router_pad = jnp.pad(router_w.astype(F32), ((0, 0), (0, 0), (0, LANES - n_exp)))
    router_hi = router_pad.astype(BF16)
    router_lo = (router_pad - router_hi.astype(F32)).astype(BF16)
    b_mod3 = b_mod.reshape(depth, 1, 6 * d)
    ln1_g3, ln1_b3 = ln1_g.reshape(depth, 1, d), ln1_b.reshape(depth, 1, d)
    ln2_g3, ln2_b3 = ln2_g.reshape(depth, 1, d), ln2_b.reshape(depth, 1, d)

    mods = [_matmul(cond, w_mod, l, 0, 6 * d, bias=b_mod3, act="silu", tn=1024).reshape(8, 6, 1, d)
            for l in range(depth)]
    hs = [_ln_mod(x, mods[0], st["mod_row"]) for x, st in zip(xs, streams)]

    for l in range(depth):
        p.update(win_sink=win_sink[l], q_norm=q_norm[l], k_norm=k_norm[l], s5=_s5_prep(lay, l))
        mod4 = mods[l]
        next_mod4 = mods[l + 1] if l + 1 < depth else None
        streams[1].update(
            ck_w=cache_win_k[:, l].reshape(bs, past, kvw), cv_w=cache_win_v[:, l].reshape(bs, past, kvw),
            ck_g=cache_glob_k[:, l].reshape(bs, past, kvg), cv_g=cache_glob_v[:, l].reshape(bs, past, kvg),
            s0=_state_to_lanes(state_ssm_re[:, l], state_ssm_im[:, l]))

        routed = []
        for si, st in enumerate(streams):
            x = xs[si]
            r, extras = _mixers(x, hs[si], mod4, p, st, l)
            x1, h2, logits = _post_ln(r, mod4, st["mod_row"], ln1_g3, ln1_b3, router_hi, router_lo, l, st["tm"])
            xsel, vals, rank = _route(logits, h2, batch=st["batch"], seq=st["seq"], n_exp=n_exp)
            routed.append((x1, xsel, vals, rank))
            if not st["latent"]:
                k_w, v_w, k_g, v_g, fin = extras
                new["wk"].append(k_w.reshape(bp, lp, -1, HEAD_DIM))
                new["wv"].append(v_w.reshape(bp, lp, -1, HEAD_DIM))
                new["gk"].append(k_g.reshape(bp, lp, -1, HEAD_DIM))
                new["gv"].append(v_g.reshape(bp, lp, -1, HEAD_DIM))
                s_re, s_im = _lanes_to_state(fin, p_n)
                new["sre"].append(s_re)
                new["sim"].append(s_im)
        outs = _ffn(routed[0][1], routed[1][1], routed[0][2], routed[1][2],
                    exp_w_gate, exp_w_up, exp_w_down, l)
        for si, st in enumerate(streams):
            xs[si], hs[si] = _scatter(outs[si], routed[si][3], routed[si][0], mod4, st["mod_row"],
                                      ln2_g3, ln2_b3, l, alpha, next_mod4, batch=st["batch"], seq=st["seq"])

    return (xs[0].reshape(bp, lp, d), xs[1].reshape(bs, ls, d),
            jnp.stack(new["wk"], axis=1), jnp.stack(new["wv"], axis=1),
            jnp.stack(new["gk"], axis=1), jnp.stack(new["gv"], axis=1),
            jnp.stack(new["sre"], axis=1), jnp.stack(new["sim"], axis=1))
```

```python
import functools

import numpy as np
import jax
import jax.numpy as jnp
from jax import lax
from jax.experimental import pallas as pl
from jax.experimental.pallas import tpu as pltpu

F32 = jnp.float32
BF16 = jnp.bfloat16
HIGHEST = lax.Precision.HIGHEST

HEAD_DIM = 128
LANES = 128
GRID_W = 64
WINDOW = 128
EC_FACTOR = 2
ROPE_THETA = 10000.0
EPS = 1e-6
NEG_INF = -1e30
ATTN_SCALE = HEAD_DIM ** -0.5
LOG2E = 1.4426950408889634
S5_CHUNK = 16
S5_OCT = 8
ROW_BLOCK = 256
GATHER_ROWS = 512
ROUTE_SEQS_PER_STEP = 2
VMEM_LIMIT = 60 * 1024 * 1024
RESIDENT = pl.Buffered(1)


def _cparams(n_axes):
    return pltpu.CompilerParams(dimension_semantics=("arbitrary",) * n_axes,
                                vmem_limit_bytes=VMEM_LIMIT)


def _tile(dim, pref):
    return pref if dim % pref == 0 else dim


def _ln(x):
    mu = jnp.mean(x, axis=-1, keepdims=True)
    xc = x - mu
    var = jnp.mean(xc * xc, axis=-1, keepdims=True)
    return xc * lax.rsqrt(var + EPS)


def _split_bf16(x):
    hi = x.astype(BF16)
    return hi, (x - hi.astype(F32)).astype(BF16)


def _dot3(a_hi, a_lo, b_hi, b_lo, dims=(((1,), (0,)), ((), ()))):
    def d(x, y):
        return lax.dot_general(x, y, dims, preferred_element_type=F32)
    return d(a_hi, b_hi) + d(a_lo, b_hi) + d(a_hi, b_lo)


def _dot3_nt(a, b):
    return _dot3(*_split_bf16(a), *_split_bf16(b), dims=(((1,), (1,)), ((), ())))


def _mm_kernel(x_ref, w_ref, *rest, act, has_bias):
    if has_bias:
        b_ref, o_ref, wbf_ref = rest
    else:
        o_ref, wbf_ref = rest

    @pl.when(pl.program_id(1) == 0)
    def _():
        wbf_ref[...] = w_ref[...].astype(BF16)

    x = x_ref[...]
    if act == "silu":
        x = jax.nn.silu(x.astype(F32))
    acc = jnp.dot(x.astype(BF16), wbf_ref[...], preferred_element_type=F32)
    if has_bias:
        acc = acc + b_ref[...]
    o_ref[...] = acc.astype(o_ref.dtype)


def _matmul(x, w, layer, col0, ncols, *, bias=None, act=None, out_dtype=F32, tm=1024, tn=512):
    m, k = x.shape
    tm = _tile(m, tm)
    tn = next(t for t in (tn, 512, 256, LANES) if ncols % t == 0 and col0 % t == 0)
    c0 = col0 // tn
    in_specs = [pl.BlockSpec((tm, k), lambda n, i: (i, 0)),
                pl.BlockSpec((None, k, tn), lambda n, i: (layer, 0, c0 + n))]
    args = [x, w]
    if bias is not None:
        in_specs.append(pl.BlockSpec((None, 1, tn), lambda n, i: (layer, 0, c0 + n)))
        args.append(bias)
    return pl.pallas_call(
        functools.partial(_mm_kernel, act=act, has_bias=bias is not None),
        grid=(ncols // tn, m // tm),
        in_specs=in_specs,
        out_specs=pl.BlockSpec((tm, tn), lambda n, i: (i, n)),
        out_shape=jax.ShapeDtypeStruct((m, ncols), out_dtype),
        scratch_shapes=[pltpu.VMEM((k, tn), BF16)],
        compiler_params=_cparams(2),
    )(*args)


def _ln_mod_kernel(x_ref, sc_ref, sh_ref, h_ref):
    h_ref[...] = (_ln(x_ref[...]) * (1.0 + sc_ref[...]) + sh_ref[...]).astype(h_ref.dtype)


def _mod_spec(d, chunk, mod_row, rows_per_step=ROW_BLOCK):
    scale = rows_per_step // ROW_BLOCK
    return pl.BlockSpec((None, None, 1, d), lambda i: (mod_row(i * scale), chunk, 0, 0))


def _ln_mod(x, mod4, mod_row):
    t, d = x.shape
    return pl.pallas_call(
        _ln_mod_kernel,
        grid=(t // ROW_BLOCK,),
        in_specs=[pl.BlockSpec((ROW_BLOCK, d), lambda i: (i, 0)),
                  _mod_spec(d, 1, mod_row), _mod_spec(d, 0, mod_row)],
        out_specs=pl.BlockSpec((ROW_BLOCK, d), lambda i: (i, 0)),
        out_shape=jax.ShapeDtypeStruct((t, d), BF16),
        compiler_params=_cparams(1),
    )(x, mod4, mod4)


def _rope_tables(seq_len):
    half = HEAD_DIM // 4
    inv = ROPE_THETA ** (-np.arange(half, dtype=np.float64) / half)
    tok = np.arange(seq_len)
    ang_r = (tok // GRID_W)[:, None] * inv[None, :]
    ang_c = (tok % GRID_W)[:, None] * inv[None, :]
    cos = np.concatenate([np.cos(ang_r), np.cos(ang_r), np.cos(ang_c), np.cos(ang_c)], axis=-1)
    sin = np.concatenate([-np.sin(ang_r), np.sin(ang_r), -np.sin(ang_c), np.sin(ang_c)], axis=-1)
    return jnp.asarray(cos, F32), jnp.asarray(sin, F32)


def _rope(x, cos, sin_signed):
    lane = lax.broadcasted_iota(jnp.int32, x.shape, 1)
    swapped = jnp.where((lane % 64) < 32, pltpu.roll(x, 96, 1), pltpu.roll(x, 32, 1))
    return x * cos + swapped * sin_signed


def _rms(x, w):
    return x * lax.rsqrt(jnp.mean(x * x, axis=-1, keepdims=True) + EPS) * w


def _attn_kernel(*refs, heads, kv, tq, rope, norm, sink, ctx, band, emit_k):
    it = iter(refs)
    q_ref, k_ref, v_ref = next(it), next(it), next(it)
    kc_ref = vc_ref = sink_ref = qn_ref = kn_ref = cq_ref = sq_ref = ck_ref = sk_ref = None
    if ctx:
        kc_ref, vc_ref = next(it), next(it)
    if sink:
        sink_ref = next(it)
    if norm:
        qn_ref, kn_ref = next(it), next(it)
    if rope:
        cq_ref, sq_ref, ck_ref, sk_ref = next(it), next(it), next(it), next(it)
    o_ref = next(it)
    ko_ref = next(it) if emit_k else None
    kp_ref, vp_ref = next(it), next(it)

    qi = pl.program_id(1)
    rep = heads // kv

    @pl.when(qi == 0)
    def _():
        for g in range(kv):
            sl = slice(g * HEAD_DIM, (g + 1) * HEAD_DIM)
            kg = k_ref[:, sl]
            if norm:
                kg = _rms(kg, kn_ref[...])
            if rope:
                kg = _rope(kg, ck_ref[...], sk_ref[...])
            if emit_k:
                ko_ref[:, sl] = kg
            kp_ref[:, sl] = kg.astype(BF16)
        vp_ref[...] = v_ref[...].astype(BF16)

    nt = (((1,), (1,)), ((), ()))
    seq = kp_ref.shape[0]
    if band:
        span = min(seq, tq + 2 * max(WINDOW, tq))
        k0 = pl.multiple_of(jnp.clip(qi * tq - max(WINDOW, tq), 0, seq - span), tq)
        rows = pl.ds(k0, span)
        qpos = qi * tq + lax.broadcasted_iota(jnp.int32, (tq, span), 0)
        kpos = k0 + lax.broadcasted_iota(jnp.int32, (tq, span), 1)
        in_band = jnp.abs(qpos - kpos) <= WINDOW
    else:
        rows = slice(None)
    for h in range(heads):
        g = h // rep
        gs = slice(g * HEAD_DIM, (g + 1) * HEAD_DIM)
        qh = q_ref[:, h * HEAD_DIM:(h + 1) * HEAD_DIM]
        if norm:
            qh = _rms(qh, qn_ref[...])
        if rope:
            qh = _rope(qh, cq_ref[...], sq_ref[...])
        qh = (qh * (ATTN_SCALE * LOG2E)).astype(BF16)
        s = lax.dot_general(qh, kp_ref[rows, gs], nt, preferred_element_type=F32)
        if band:
            s = jnp.where(in_band, s, NEG_INF)
        m = jnp.max(s, axis=-1, keepdims=True)
        if ctx:
            sc = lax.dot_general(qh, kc_ref[:, gs].astype(BF16), nt, preferred_element_type=F32)
            m = jnp.maximum(m, jnp.max(sc, axis=-1, keepdims=True))
        if sink:
            m = jnp.maximum(m, sink_ref[h] * LOG2E)
        e = jnp.exp2(s - m)
        den = jnp.sum(e, axis=-1, keepdims=True)
        o = jnp.dot(e.astype(BF16), vp_ref[rows, gs], preferred_element_type=F32)
        if ctx:
            ec = jnp.exp2(sc - m)
            den = den + jnp.sum(ec, axis=-1, keepdims=True)
            o = o + jnp.dot(ec.astype(BF16), vc_ref[:, gs].astype(BF16), preferred_element_type=F32)
        if sink:
            den = den + jnp.exp2(sink_ref[h] * LOG2E - m)
        o_ref[:, h * HEAD_DIM:(h + 1) * HEAD_DIM] = (o / den).astype(o_ref.dtype)


def _attention(qa, q0, ka, k0, va, v0, *, heads, kv, batch, seq, k_ctx=None, v_ctx=None, sink=None,
               q_norm=None, k_norm=None, rope=False, band=False, emit_k=False):
    t = qa.shape[0]
    qw, kw = heads * HEAD_DIM, kv * HEAD_DIM
    assert q0 % qw == 0 and k0 % kw == 0 and v0 % kw == 0
    qc, kc, vc = q0 // qw, k0 // kw, v0 // kw
    tq = _tile(seq, 256)
    nq = seq // tq
    ctx, has_sink, norm = k_ctx is not None, sink is not None, q_norm is not None
    in_specs = [pl.BlockSpec((tq, qw), lambda b, i: (b * nq + i, qc)),
                pl.BlockSpec((seq, kw), lambda b, i: (b, kc)),
                pl.BlockSpec((seq, kw), lambda b, i: (b, vc))]
    args = [qa, ka, va]
    if ctx:
        lc = k_ctx.shape[1]
        in_specs += [pl.BlockSpec((None, lc, kw), lambda b, i: (b, 0, 0))] * 2
        args += [k_ctx, v_ctx]
    if has_sink:
        in_specs.append(pl.BlockSpec(memory_space=pltpu.SMEM))
        args.append(sink)
    if norm:
        in_specs += [pl.BlockSpec((1, HEAD_DIM), lambda b, i: (0, 0))] * 2
        args += [q_norm.reshape(1, HEAD_DIM), k_norm.reshape(1, HEAD_DIM)]
    if rope:
        cos, sin = _rope_tables(seq)
        in_specs += [pl.BlockSpec((tq, HEAD_DIM), lambda b, i: (i, 0))] * 2
        in_specs += [pl.BlockSpec((seq, HEAD_DIM), lambda b, i: (0, 0))] * 2
        args += [cos, sin, cos, sin]
    out_specs = [pl.BlockSpec((tq, qw), lambda b, i: (b * nq + i, 0))]
    out_shape = [jax.ShapeDtypeStruct((t, qw), BF16)]
    if emit_k:
        out_specs.append(pl.BlockSpec((seq, kw), lambda b, i: (b, 0)))
        out_shape.append(jax.ShapeDtypeStruct((t, kw), F32))
    res = pl.pallas_call(
        functools.partial(_attn_kernel, heads=heads, kv=kv, tq=tq, rope=rope, norm=norm,
                          sink=has_sink, ctx=ctx, band=band, emit_k=emit_k),
        grid=(batch, nq),
        in_specs=in_specs,
        out_specs=out_specs,
        out_shape=out_shape,
        scratch_shapes=[pltpu.VMEM((seq, kw), BF16), pltpu.VMEM((seq, kw), BF16)],
        compiler_params=_cparams(2),
    )(*args)
    return res if emit_k else res[0]


def _s5_layouts(log_dt, a_re, a_im, b_re, b_im, c_re, c_im):
    dep, _, g_n, p_n = a_re.shape
    gc = b_re.shape[-1]
    no = g_n // S5_OCT
    sw = S5_OCT * p_n
    ldt = jnp.broadcast_to(log_dt[..., None], a_re.shape)
    prm = jnp.stack([ldt, a_re, a_im], axis=2).astype(F32)
    prow = jnp.pad(prm.reshape(dep, 2, 3, no, sw).transpose(0, 1, 3, 2, 4),
                   ((0, 0),) * 3 + ((0, 5), (0, 0)))
    pcmp = jnp.broadcast_to(prm[:, :, :, :, None, :], (dep, 2, 3, g_n, gc, p_n))
    pcmp = pcmp.reshape(dep, 2, 3, no, S5_OCT * gc, p_n).transpose(0, 1, 3, 2, 4, 5)

    def b_views(b):
        bt = jnp.swapaxes(b.astype(F32), -1, -2).reshape(dep, 2, no, S5_OCT * gc, p_n)
        return bt, jnp.tile(bt, (1, 1, 1, 1, S5_OCT))

    def c_views(c):
        cc = c.astype(F32).reshape(dep, 2, no, S5_OCT * gc, p_n)
        ct = jnp.swapaxes(c.astype(F32), -1, -2).reshape(dep, 2, no, sw, gc)
        return cc, jnp.tile(ct, (1, 1, 1, 1, S5_OCT))

    bc_re, bt_re = b_views(b_re)
    bc_im, bt_im = b_views(b_im)
    cc_re, ct_re = c_views(c_re)
    cc_im, ct_im = c_views(c_im)
    return dict(prow=prow, pcmp=pcmp, bc_re=bc_re, bc_im=bc_im, bt_re=bt_re, bt_im=bt_im,
                cc_re=cc_re, cc_im=cc_im, ct_re=ct_re, ct_im=ct_im)


def _lam_powers(ldt, ar, ai, k):
    dt = jnp.exp(ldt)
    mag = jnp.exp(k * (dt * ar))
    return mag * jnp.cos(k * (dt * ai)), mag * jnp.sin(k * (dt * ai))


def _bbar_coef(ldt, ar, ai):
    lam_re, lam_im = _lam_powers(ldt, ar, ai, 1.0)
    den = ar * ar + ai * ai
    return ((lam_re - 1.0) * ar + lam_im * ai) / den, (lam_im * ar - (lam_re - 1.0) * ai) / den


def _same_group(shape, rows_per_group, cols_per_group):
    r = lax.broadcasted_iota(jnp.int32, shape, 0) // rows_per_group
    c = lax.broadcasted_iota(jnp.int32, shape, 1) // cols_per_group
    return r == c


def _s5_prep_w_kernel(prow_ref, btr_ref, bti_ref, w_ref, lam_ref):
    tc = S5_CHUNK
    rows, sw = btr_ref.shape[1], btr_ref.shape[2]
    mask = _same_group((rows, sw), rows // S5_OCT, sw // S5_OCT)
    k = lax.broadcasted_iota(jnp.int32, (24, sw), 0).astype(F32)
    for d in range(2):
        ldt, ar, ai = prow_ref[d, 0:1, :], prow_ref[d, 1:2, :], prow_ref[d, 2:3, :]
        pw_re, pw_im = _lam_powers(ldt, ar, ai, k)
        co_re, co_im = _bbar_coef(ldt, ar, ai)
        bb_re = jnp.where(mask, co_re * btr_ref[d] - co_im * bti_ref[d], 0.0)
        bb_im = jnp.where(mask, co_re * bti_ref[d] + co_im * btr_ref[d], 0.0)
        lam_ref[2 * d:2 * d + 1, :] = pw_re[tc:tc + 1]
        lam_ref[2 * d + 1:2 * d + 2, :] = pw_im[tc:tc + 1]
        for j in range(tc):
            kk = tc - 1 - j if d == 0 else j
            pr, pi = pw_re[kk:kk + 1], pw_im[kk:kk + 1]
            rs = slice(j * rows, (j + 1) * rows)
            w_ref[rs, (2 * d) * sw:(2 * d + 1) * sw] = (pr * bb_re - pi * bb_im).astype(w_ref.dtype)
            w_ref[rs, (2 * d + 1) * sw:(2 * d + 2) * sw] = (pr * bb_im + pi * bb_re).astype(w_ref.dtype)


def _row_to_col(row, eye):
    return jnp.sum(jnp.where(eye, row, 0.0), axis=1, keepdims=True)


def _s5_prep_z_kernel(prow_ref, ctr_ref, cti_ref, z_ref):
    tc = S5_CHUNK
    sw, cols = ctr_ref.shape[1], ctr_ref.shape[2]
    mask = _same_group((sw, cols), sw // S5_OCT, cols // S5_OCT)
    eye = _same_group((sw, sw), 1, 1)
    for d in range(2):
        lam_re, lam_im = _lam_powers(prow_ref[d, 0:1, :], prow_ref[d, 1:2, :], prow_ref[d, 2:3, :], 1.0)
        lr = jnp.broadcast_to(_row_to_col(lam_re, eye), (sw, cols))
        li = jnp.broadcast_to(_row_to_col(lam_im, eye), (sw, cols))
        c_re = jnp.where(mask, ctr_ref[d], 0.0)
        c_im = jnp.where(mask, cti_ref[d], 0.0)
        z_re, z_im = c_re * lr - c_im * li, c_re * li + c_im * lr
        for step in range(tc):
            t = step if d == 0 else tc - 1 - step
            cs = slice(t * cols, (t + 1) * cols)
            z_ref[(2 * d) * sw:(2 * d + 1) * sw, cs] = z_re.astype(z_ref.dtype)
            z_ref[(2 * d + 1) * sw:(2 * d + 2) * sw, cs] = (-z_im).astype(z_ref.dtype)
            z_re, z_im = z_re * lr - z_im * li, z_re * li + z_im * lr


def _s5_prep_m_kernel(pcmp_ref, bcr_ref, bci_ref, ccr_ref, cci_ref, m_ref, xr_ref, xi_ref, taps_ref):
    tc = S5_CHUNK
    rows = bcr_ref.shape[1]
    mask = _same_group((rows, rows), rows // S5_OCT, rows // S5_OCT)
    for d in range(2):
        ldt, ar, ai = pcmp_ref[d, 0], pcmp_ref[d, 1], pcmp_ref[d, 2]
        lam_re, lam_im = _lam_powers(ldt, ar, ai, 1.0)
        co_re, co_im = _bbar_coef(ldt, ar, ai)
        x_re = co_re * bcr_ref[d] - co_im * bci_ref[d]
        x_im = co_re * bci_ref[d] + co_im * bcr_ref[d]
        for kk in range(tc):
            xr_ref[kk * rows:(kk + 1) * rows, :] = x_re
            xi_ref[kk * rows:(kk + 1) * rows, :] = x_im
            x_re, x_im = x_re * lam_re - x_im * lam_im, x_re * lam_im + x_im * lam_re
        taps = _dot3_nt(xr_ref[...], ccr_ref[d]) - _dot3_nt(xi_ref[...], cci_ref[d])
        for kk in range(tc):
            tile = jnp.where(mask, taps[kk * rows:(kk + 1) * rows, :], 0.0)
            col = (tc - 1 + kk) if d == 0 else (tc - 1 - kk)
            cs = slice(col * rows, (col + 1) * rows)
            if d == 1 and kk == 0:
                taps_ref[:, cs] += tile
            else:
                taps_ref[:, cs] = tile
    for j in range(tc):
        m_ref[j * rows:(j + 1) * rows, :] = taps_ref[:, (tc - 1 - j) * rows:(2 * tc - 1 - j) * rows].astype(m_ref.dtype)


def _s5_prep(lay, layer):
    _, _, no, rows, sw = lay["bt_re"].shape
    p_n = lay["bc_re"].shape[-1]
    tc = S5_CHUNK
    kw = tc * rows

    def per_octet(*tail):
        nd = len(tail)
        return pl.BlockSpec((None, 2, None) + tail, lambda o: (layer, 0, o) + (0,) * nd)

    def out(r, c):
        return pl.BlockSpec((None, r, c), lambda o: (o, 0, 0))

    w_o, lam_o = pl.pallas_call(
        _s5_prep_w_kernel, grid=(no,),
        in_specs=[per_octet(8, sw), per_octet(rows, sw), per_octet(rows, sw)],
        out_specs=[out(kw, 4 * sw), out(4, sw)],
        out_shape=[jax.ShapeDtypeStruct((no, kw, 4 * sw), BF16), jax.ShapeDtypeStruct((no, 4, sw), F32)],
        compiler_params=_cparams(1),
    )(lay["prow"], lay["bt_re"], lay["bt_im"])
    z_o = pl.pallas_call(
        _s5_prep_z_kernel, grid=(no,),
        in_specs=[per_octet(8, sw), per_octet(sw, rows), per_octet(sw, rows)],
        out_specs=out(4 * sw, kw),
        out_shape=jax.ShapeDtypeStruct((no, 4 * sw, kw), BF16),
        compiler_params=_cparams(1),
    )(lay["prow"], lay["ct_re"], lay["ct_im"])
    m_o = pl.pallas_call(
        _s5_prep_m_kernel, grid=(no,),
        in_specs=[per_octet(3, rows, p_n)] + [per_octet(rows, p_n)] * 4,
        out_specs=out(kw, kw),
        out_shape=jax.ShapeDtypeStruct((no, kw, kw), BF16),
        scratch_shapes=[pltpu.VMEM((kw, p_n), F32), pltpu.VMEM((kw, p_n), F32),
                        pltpu.VMEM((rows, (2 * tc - 1) * rows), F32)],
        compiler_params=_cparams(1),
    )(lay["pcmp"], lay["bc_re"], lay["bc_im"], lay["cc_re"], lay["cc_im"])
    return m_o, w_o, z_o, lam_o


def _s5_state_kernel(u_ref, w_ref, lam_ref, s0_ref, sp_ref, fin_ref, ubf_ref, vs_ref, *, batch, nchunk):
    tc = S5_CHUNK
    r = batch * nchunk
    for t in range(tc):
        ubf_ref[:, t * LANES:(t + 1) * LANES] = u_ref[pl.ds(t, r, stride=tc), :].astype(BF16)
    q = vs_ref.shape[-1] // 4
    vs_ref[...] = jnp.dot(ubf_ref[...], w_ref[...],
                          preferred_element_type=F32).reshape(batch, nchunk, 4 * q)

    def scan(part, order):
        a, b = lam_ref[part:part + 1, :], lam_ref[part + 1:part + 2, :]
        re, im = slice(part * q, (part + 1) * q), slice((part + 1) * q, (part + 2) * q)
        s_re, s_im = s0_ref[:, re], s0_ref[:, im]
        for c in order:
            v_re, v_im = vs_ref[:, c, re], vs_ref[:, c, im]
            vs_ref[:, c, re] = s_re
            vs_ref[:, c, im] = s_im
            s_re, s_im = a * s_re - b * s_im + v_re, a * s_im + b * s_re + v_im
        fin_ref[:, re] = s_re
        fin_ref[:, im] = s_im

    scan(0, range(nchunk))
    scan(2, range(nchunk - 1, -1, -1))
    sp_ref[...] = vs_ref[...].reshape(r, 4 * q).astype(sp_ref.dtype)


def _s5_out_kernel(ubf_ref, sp_ref, m_ref, z_ref, y_ref):
    tc = S5_CHUNK
    tr = ubf_ref.shape[0]
    y = (jnp.dot(ubf_ref[...], m_ref[...], preferred_element_type=F32)
         + jnp.dot(sp_ref[...], z_ref[...], preferred_element_type=F32))
    for t in range(tc):
        y_ref[pl.ds(t, tr, stride=tc), :] = y[:, t * LANES:(t + 1) * LANES]


def _s5_mix(ua, u0, width, m_o, w_o, z_o, lam_o, s0, *, batch, seq):
    t = ua.shape[0]
    tc = S5_CHUNK
    no = width // LANES
    assert u0 % LANES == 0
    uc = u0 // LANES
    r = t // tc
    nchunk = seq // tc
    sw = w_o.shape[-1]
    kw = tc * LANES
    sp, fin, ubf = pl.pallas_call(
        functools.partial(_s5_state_kernel, batch=batch, nchunk=nchunk),
        grid=(no,),
        in_specs=[pl.BlockSpec((t, LANES), lambda o: (0, uc + o)),
                  pl.BlockSpec((None, kw, sw), lambda o: (o, 0, 0)),
                  pl.BlockSpec((None, 4, sw // 4), lambda o: (o, 0, 0)),
                  pl.BlockSpec((None, batch, sw), lambda o: (o, 0, 0))],
        out_specs=[pl.BlockSpec((None, r, sw), lambda o: (o, 0, 0)),
                   pl.BlockSpec((None, batch, sw), lambda o: (o, 0, 0)),
                   pl.BlockSpec((None, r, kw), lambda o: (o, 0, 0))],
        out_shape=[jax.ShapeDtypeStruct((no, r, sw), BF16),
                   jax.ShapeDtypeStruct((no, batch, sw), F32),
                   jax.ShapeDtypeStruct((no, r, kw), BF16)],
        scratch_shapes=[pltpu.VMEM((batch, nchunk, sw), F32)],
        compiler_params=_cparams(1),
    )(ua, w_o, lam_o, s0)
    tr = _tile(r, 512)
    y = pl.pallas_call(
        _s5_out_kernel,
        grid=(no, r // tr),
        in_specs=[pl.BlockSpec((None, tr, kw), lambda o, i: (o, i, 0)),
                  pl.BlockSpec((None, tr, sw), lambda o, i: (o, i, 0)),
                  pl.BlockSpec((None, kw, kw), lambda o, i: (o, 0, 0)),
                  pl.BlockSpec((None, sw, kw), lambda o, i: (o, 0, 0))],
        out_specs=pl.BlockSpec((tr * tc, LANES), lambda o, i: (i, o)),
        out_shape=jax.ShapeDtypeStruct((t, width), F32),
        compiler_params=_cparams(2),
    )(ubf, sp, m_o, z_o)
    return y, fin


def _state_to_lanes(s_re, s_im):
    b, _, g_n, p_n = s_re.shape
    no = g_n // S5_OCT
    parts = jnp.stack([s_re[:, 0], s_im[:, 0], s_re[:, 1], s_im[:, 1]], axis=1)
    parts = parts.reshape(b, 4, no, S5_OCT * p_n)
    return jnp.transpose(parts, (2, 0, 1, 3)).reshape(no, b, 4 * S5_OCT * p_n).astype(F32)


def _lanes_to_state(fin, p_n):
    no, b, _ = fin.shape
    parts = jnp.transpose(fin.reshape(no, b, 4, S5_OCT, p_n), (1, 2, 0, 3, 4)).reshape(b, 4, no * S5_OCT, p_n)
    return jnp.stack([parts[:, 0], parts[:, 2]], axis=1), jnp.stack([parts[:, 1], parts[:, 3]], axis=1)


def _glu_kernel(y_ref, ua_ref, ub_ref, d_ref, w_ref, o_ref):
    u = jnp.concatenate([ua_ref[...], ub_ref[...]], axis=1)
    z = jax.nn.gelu(y_ref[...] + d_ref[...] * u).astype(BF16)
    g = jnp.dot(z, w_ref[...], preferred_element_type=F32)
    w = o_ref.shape[-1]
    o_ref[...] = (g[:, :w] * jax.nn.sigmoid(g[:, w:])).astype(o_ref.dtype)


def _glu(y, ua, u0, d, w_glu, layer):
    t, width = y.shape
    half = width // 2
    assert u0 % half == 0
    uc = u0 // half
    tm = _tile(t, 512)
    return pl.pallas_call(
        _glu_kernel,
        grid=(t // tm,),
        in_specs=[pl.BlockSpec((tm, width), lambda i: (i, 0)),
                  pl.BlockSpec((tm, half), lambda i: (i, uc)),
                  pl.BlockSpec((tm, half), lambda i: (i, uc + 1)),
                  pl.BlockSpec((None, 1, width), lambda i: (layer, 0, 0)),
                  pl.BlockSpec((None, width, 2 * width), lambda i: (layer, 0, 0), pipeline_mode=RESIDENT)],
        out_specs=pl.BlockSpec((tm, width), lambda i: (i, 0)),
        out_shape=jax.ShapeDtypeStruct((t, width), BF16),
        compiler_params=_cparams(1),
    )(y, ua, ua, d, w_glu)


def _merge_kernel(ow_ref, os_ref, og_ref, gw_ref, gs_ref, gg_ref, x_ref, g1_ref,
                  ww_ref, ws_ref, wg_ref, wo_ref, r_ref, *, alpha):
    def gate(g_ref):
        return jax.nn.sigmoid(g_ref[...].astype(F32))

    y = (gate(gw_ref) * jnp.dot(ow_ref[...], ww_ref[...], preferred_element_type=F32)
         + gate(gs_ref) * jnp.dot(os_ref[...], ws_ref[...], preferred_element_type=F32)
         + gate(gg_ref) * jnp.dot(og_ref[...], wg_ref[...], preferred_element_type=F32))
    o = jnp.dot(y.astype(BF16), wo_ref[...], preferred_element_type=F32)
    r_ref[...] = alpha * x_ref[...] + g1_ref[...] * o


def _merge(o_win, o_ssm, o_glob, gates, x, mod4, mod_row, w_win, w_ssm, w_glob, w_out, layer, alpha, tm):
    t, d = x.shape

    def rows(w):
        return pl.BlockSpec((tm, w), lambda i: (i, 0))

    def whole(a):
        return pl.BlockSpec((None,) + a.shape[1:], lambda i: (layer, 0, 0), pipeline_mode=RESIDENT)

    return pl.pallas_call(
        functools.partial(_merge_kernel, alpha=alpha),
        grid=(t // tm,),
        in_specs=[rows(o_win.shape[1]), rows(o_ssm.shape[1]), rows(o_glob.shape[1]),
                  pl.BlockSpec((tm, d), lambda i: (i, 0)),
                  pl.BlockSpec((tm, d), lambda i: (i, 1)),
                  pl.BlockSpec((tm, d), lambda i: (i, 2)),
                  rows(d), _mod_spec(d, 2, mod_row, tm),
                  whole(w_win), whole(w_ssm), whole(w_glob), whole(w_out)],
        out_specs=rows(d),
        out_shape=jax.ShapeDtypeStruct((t, d), F32),
        compiler_params=_cparams(1),
    )(o_win, o_ssm, o_glob, gates, gates, gates, x, mod4, w_win, w_ssm, w_glob, w_out)


def _post_ln_kernel(r_ref, sc2_ref, sh2_ref, lg_ref, lb_ref, rwh_ref, rwl_ref, x1_ref, h2_ref, logit_ref):
    x1 = _ln(r_ref[...]) * lg_ref[...] + lb_ref[...]
    x1_ref[...] = x1
    h2 = _ln(x1) * (1.0 + sc2_ref[...]) + sh2_ref[...]
    hi, lo = _split_bf16(h2)
    h2_ref[...] = hi
    logit_ref[...] = _dot3(hi, lo, rwh_ref[...], rwl_ref[...])


def _post_ln(r, mod4, mod_row, ln_g, ln_b, router_hi, router_lo, layer, tm):
    t, d = r.shape

    def rows(w):
        return pl.BlockSpec((tm, w), lambda i: (i, 0))

    def vec():
        return pl.BlockSpec((None, 1, d), lambda i: (layer, 0, 0))

    def router():
        return pl.BlockSpec((None, d, LANES), lambda i: (layer, 0, 0), pipeline_mode=RESIDENT)

    return pl.pallas_call(
        _post_ln_kernel,
        grid=(t // tm,),
        in_specs=[rows(d), _mod_spec(d, 4, mod_row, tm), _mod_spec(d, 3, mod_row, tm), vec(), vec(),
                  router(), router()],
        out_specs=[rows(d), rows(d), rows(LANES)],
        out_shape=[jax.ShapeDtypeStruct((t, d), F32), jax.ShapeDtypeStruct((t, d), BF16),
                   jax.ShapeDtypeStruct((t, LANES), F32)],
        compiler_params=_cparams(1),
    )(r, mod4, mod4, ln_g, ln_b, router_hi, router_lo)


def _route_kernel(logit_ref, h_ref, xsel_ref, vals_ref, rank_ref, *, n_exp, cap):
    nsub, _, n = rank_ref.shape
    jj = lax.broadcasted_iota(jnp.int32, (n, n), 0)
    ii = lax.broadcasted_iota(jnp.int32, (n, n), 1)
    slot = lax.broadcasted_iota(jnp.int32, (cap, n), 0)
    for s in range(nsub):
        tok = slice(s * n, (s + 1) * n)
        lg = logit_ref[tok, :]
        col = lax.broadcasted_iota(jnp.int32, lg.shape, 1)
        lg = jnp.where(col < n_exp, lg, -jnp.inf)
        ex = jnp.exp(lg - jnp.max(lg, axis=-1, keepdims=True))
        aff = ex / jnp.sum(ex, axis=-1, keepdims=True)
        aff_t = aff.T
        picks = []
        for e in range(n_exp):
            a_row = aff_t[e:e + 1, :]
            a_col = aff[:, e:e + 1]
            beats = (a_col > a_row) | ((a_col == a_row) & (jj < ii))
            rank = jnp.sum(beats.astype(jnp.int32), axis=0, keepdims=True)
            pick = slot == rank
            picks.append(pick.astype(BF16))
            vals_ref[e, s * cap:(s + 1) * cap, :] = jnp.sum(jnp.where(pick, a_row, 0.0), axis=1, keepdims=True)
            rank_ref[s, e:e + 1, :] = rank
        group = min(n_exp, max(1, GATHER_ROWS // cap))
        for e0 in range(0, n_exp, group):
            sel = jnp.dot(jnp.concatenate(picks[e0:e0 + group], axis=0), h_ref[tok, :],
                          preferred_element_type=F32)
            for e in range(e0, min(e0 + group, n_exp)):
                xsel_ref[e, s * cap:(s + 1) * cap, :] = sel[(e - e0) * cap:(e - e0 + 1) * cap, :].astype(
                    xsel_ref.dtype)


def _route(logits, h2, *, batch, seq, n_exp):
    t, d = h2.shape
    cap = EC_FACTOR * seq // n_exp
    nsub = ROUTE_SEQS_PER_STEP if seq <= ROW_BLOCK and batch % ROUTE_SEQS_PER_STEP == 0 else 1
    return pl.pallas_call(
        functools.partial(_route_kernel, n_exp=n_exp, cap=cap),
        grid=(batch // nsub,),
        in_specs=[pl.BlockSpec((nsub * seq, LANES), lambda b: (b, 0)),
                  pl.BlockSpec((nsub * seq, d), lambda b: (b, 0))],
        out_specs=[pl.BlockSpec((n_exp, nsub * cap, d), lambda b: (0, b, 0)),
                   pl.BlockSpec((n_exp, nsub * cap, 1), lambda b: (0, b, 0)),
                   pl.BlockSpec((nsub, n_exp, seq), lambda b: (b, 0, 0))],
        out_shape=[jax.ShapeDtypeStruct((n_exp, batch * cap, d), BF16),
                   jax.ShapeDtypeStruct((n_exp, batch * cap, 1), F32),
                   jax.ShapeDtypeStruct((batch, n_exp, seq), jnp.int32)],
        compiler_params=_cparams(1),
    )(logits, h2)


def _ffn_hidden_kernel(xp_ref, xs_ref, wg_ref, wu_ref, hp_ref, hs_ref):
    wg = wg_ref[...].astype(BF16)
    wu = wu_ref[...].astype(BF16)
    for x_ref, h_ref in ((xp_ref, hp_ref), (xs_ref, hs_ref)):
        x = x_ref[...]
        h_ref[...] = (jax.nn.silu(jnp.dot(x, wg, preferred_element_type=F32))
                      * jnp.dot(x, wu, preferred_element_type=F32)).astype(h_ref.dtype)


def _ffn_down_kernel(hp_ref, hs_ref, vp_ref, vs_ref, wd_ref, op_ref, os_ref):
    wd = wd_ref[...].astype(BF16)
    for h_ref, v_ref, o_ref in ((hp_ref, vp_ref, op_ref), (hs_ref, vs_ref, os_ref)):
        o_ref[...] = (jnp.dot(h_ref[...], wd, preferred_element_type=F32) * v_ref[...]).astype(o_ref.dtype)


def _ffn(xsel_p, xsel_s, vals_p, vals_s, w_gate, w_up, w_down, layer):
    n_exp, rp, d = xsel_p.shape
    rs = xsel_s.shape[1]
    ff = w_gate.shape[-1]
    tf = _tile(ff, 512)
    tn = _tile(d, 1024)

    def per_expert(r, w):
        return pl.BlockSpec((None, r, w), lambda e, j: (e, 0, 0))

    def col_tile(r, w):
        return pl.BlockSpec((None, r, w), lambda e, j: (e, 0, j))

    def w_tile(k, w):
        return pl.BlockSpec((None, None, k, w), lambda e, j: (layer, e, 0, j))

    hid_p, hid_s = pl.pallas_call(
        _ffn_hidden_kernel,
        grid=(n_exp, ff // tf),
        in_specs=[per_expert(rp, d), per_expert(rs, d), w_tile(d, tf), w_tile(d, tf)],
        out_specs=[col_tile(rp, tf), col_tile(rs, tf)],
        out_shape=[jax.ShapeDtypeStruct((n_exp, rp, ff), BF16), jax.ShapeDtypeStruct((n_exp, rs, ff), BF16)],
        compiler_params=_cparams(2),
    )(xsel_p, xsel_s, w_gate, w_up)
    return pl.pallas_call(
        _ffn_down_kernel,
        grid=(n_exp, d // tn),
        in_specs=[per_expert(rp, ff), per_expert(rs, ff), per_expert(rp, 1), per_expert(rs, 1),
                  w_tile(ff, tn)],
        out_specs=[col_tile(rp, tn), col_tile(rs, tn)],
        out_shape=[jax.ShapeDtypeStruct((n_exp, rp, d), BF16), jax.ShapeDtypeStruct((n_exp, rs, d), BF16)],
        compiler_params=_cparams(2),
    )(hid_p, hid_s, vals_p, vals_s, w_down)


def _scatter_kernel(out_ref, rank_ref, x_ref, g2_ref, lg_ref, lb_ref, *rest, alpha, emit_h):
    n_exp, cap, d = out_ref.shape
    n = x_ref.shape[0]
    slot = lax.broadcasted_iota(jnp.int32, (cap, n), 0)
    pick = jnp.concatenate([(slot == rank_ref[e:e + 1, :]).astype(F32) for e in range(n_exp)], axis=0)
    f = jnp.dot(pick.T.astype(BF16), out_ref[...].reshape(n_exp * cap, d), preferred_element_type=F32)
    x2 = _ln(alpha * x_ref[...] + g2_ref[...] * f) * lg_ref[...] + lb_ref[...]
    if emit_h:
        sc_ref, sh_ref, x2_ref, h_ref = rest
        h_ref[...] = (_ln(x2) * (1.0 + sc_ref[...]) + sh_ref[...]).astype(h_ref.dtype)
    else:
        x2_ref, = rest
    x2_ref[...] = x2


def _scatter(out, rank, x1, mod4, mod_row, ln_g, ln_b, layer, alpha, next_mod4, *, batch, seq):
    n_exp, _, d = out.shape
    cap = out.shape[1] // batch
    blocks_per_seq = seq // ROW_BLOCK
    emit_h = next_mod4 is not None

    ts = _tile(seq, 512)
    nts = seq // ts

    def mod(chunk):
        return pl.BlockSpec((None, None, 1, d), lambda b, j: (mod_row(b * blocks_per_seq), chunk, 0, 0))

    rows = pl.BlockSpec((ts, d), lambda b, j: (b * nts + j, 0))
    in_specs = [pl.BlockSpec((n_exp, cap, d), lambda b, j: (0, b, 0)),
                pl.BlockSpec((None, n_exp, ts), lambda b, j: (b, 0, j)),
                rows, mod(5),
                pl.BlockSpec((None, 1, d), lambda b, j: (layer, 0, 0)),
                pl.BlockSpec((None, 1, d), lambda b, j: (layer, 0, 0))]
    args = [out, rank, x1, mod4, ln_g, ln_b]
    out_specs = [rows]
    out_shape = [jax.ShapeDtypeStruct((batch * seq, d), F32)]
    if emit_h:
        in_specs += [mod(1), mod(0)]
        args += [next_mod4, next_mod4]
        out_specs.append(rows)
        out_shape.append(jax.ShapeDtypeStruct((batch * seq, d), BF16))
    res = pl.pallas_call(
        functools.partial(_scatter_kernel, alpha=alpha, emit_h=emit_h),
        grid=(batch, nts),
        in_specs=in_specs,
        out_specs=out_specs,
        out_shape=out_shape,
        compiler_params=_cparams(2),
    )(*args)
    return (res[0], res[1]) if emit_h else (res[0], None)


def _mixers(x, h, mod4, p, st, layer):
    batch, seq = st["batch"], st["seq"]
    dims = p["dims"]
    hw, kvw, hg, kvg, ssm_w = dims["heads_win"], dims["kv_win"], dims["heads_glob"], dims["kv_glob"], dims["ssm_w"]
    wa = (hw + 2 * kvw) * HEAD_DIM
    wb = (hg + 2 * kvg) * HEAD_DIM
    pa = _matmul(h, p["w_in"], layer, 0, wa + ssm_w, tn=(wa + ssm_w) // 2)
    pb = _matmul(h, p["w_in"], layer, wa + ssm_w, wb)
    gates = _matmul(h, p["w_in"], layer, wa + ssm_w + wb, p["w_in"].shape[-1] - wa - ssm_w - wb,
                    out_dtype=BF16, tn=1024)
    k_w0, v_w0 = hw * HEAD_DIM, (hw + kvw) * HEAD_DIM
    k_g0, v_g0 = hg * HEAD_DIM, (hg + kvg) * HEAD_DIM
    win = dict(heads=hw, kv=kvw, batch=batch, seq=seq, sink=p["win_sink"])
    glob = dict(heads=hg, kv=kvg, batch=batch, seq=seq, q_norm=p["q_norm"], k_norm=p["k_norm"])
    if st["latent"]:
        o_win = _attention(pa, 0, pa, k_w0, pa, v_w0, k_ctx=st["ck_w"], v_ctx=st["cv_w"],
                           rope=True, band=True, **win)
        o_glob = _attention(pb, 0, pb, k_g0, pb, v_g0, k_ctx=st["ck_g"], v_ctx=st["cv_g"], rope=True, **glob)
        k_g = None
    else:
        o_win = _attention(pa, 0, pa, k_w0, pa, v_w0, **win)
        o_glob, k_g = _attention(pb, 0, pb, k_g0, pb, v_g0, emit_k=True, **glob)
    y_ssm, fin = _s5_mix(pa, wa, ssm_w, *p["s5"], st["s0"], batch=batch, seq=seq)
    o_ssm = _glu(y_ssm, pa, wa, p["ssm_d"], p["ssm_w_glu"], layer)
    r = _merge(o_win, o_ssm, o_glob, gates, x, mod4, st["mod_row"], p["w_up_win"], p["w_up_ssm"],
               p["w_up_glob"], p["w_out"], layer, p["alpha"], st["tm"])
    extras = (pa[:, k_w0:v_w0], pa[:, v_w0:wa], k_g, pb[:, v_g0:v_g0 + kvg * HEAD_DIM], fin)
    return r, extras


def kernel(x_prompt, x_sample, cache_win_k, cache_win_v, cache_glob_k, cache_glob_v, state_ssm_re, state_ssm_im, c, c_ctx, w_mod, b_mod, w_in, win_sink, ssm_a_re, ssm_a_im, ssm_log_dt, ssm_b_re, ssm_b_im, ssm_c_re, ssm_c_im, ssm_d, ssm_w_glu, q_norm, k_norm, w_up_win, w_up_ssm, w_up_glob, w_out, ln1_g, ln1_b, ln2_g, ln2_b, router_w, exp_w_gate, exp_w_up, exp_w_down):
    bp, lp, d = x_prompt.shape
    bs, ls, _ = x_sample.shape
    depth = w_mod.shape[0]
    past = cache_win_k.shape[2]
    kv_win, kv_glob = cache_win_k.shape[3], cache_glob_k.shape[3]
    kvw, kvg = kv_win * HEAD_DIM, kv_glob * HEAD_DIM
    ssm_w = w_up_ssm.shape[1]
    n_exp = router_w.shape[-1]
    p_n = ssm_a_re.shape[-1]
    alpha = (2.0 * depth) ** 0.25
    dims = dict(heads_win=w_up_win.shape[1] // HEAD_DIM, kv_win=kv_win,
                heads_glob=w_up_glob.shape[1] // HEAD_DIM, kv_glob=kv_glob, ssm_w=ssm_w)
    assert lp % ROW_BLOCK == 0 and ls % ROW_BLOCK == 0 and 1 + bs <= 8

    cond = jnp.zeros((8, d), F32).at[0].set(c_ctx).at[1:1 + bs].set(c)
    sample_blocks = ls // ROW_BLOCK
    streams = [
        dict(batch=bp, seq=lp, latent=False, mod_row=lambda i: 0, tm=_tile(bp * lp, 512),
             s0=jnp.zeros((ssm_w // LANES, bp, 4 * S5_OCT * p_n), F32)),
        dict(batch=bs, seq=ls, latent=True, mod_row=lambda i: 1 + i // sample_blocks, tm=_tile(ls, 512)),
    ]
    xs = [x_prompt.reshape(bp * lp, d), x_sample.reshape(bs * ls, d)]
    new = {k: [] for k in ("wk", "wv", "gk", "gv", "sre", "sim")}

    lay = _s5_layouts(ssm_log_dt, ssm_a_re, ssm_a_im, ssm_b_re, ssm_b_im, ssm_c_re, ssm_c_im)
    p = dict(w_in=w_in, dims=dims, ssm_d=ssm_d.reshape(depth, 1, ssm_w),
             ssm_w_glu=ssm_w_glu.astype(BF16), w_up_win=w_up_win.astype(BF16),
             w_up_ssm=w_up_ssm.astype(BF16), w_up_glob=w_up_glob.astype(BF16))
    p.update(w_out=w_out.astype(BF16), alpha=alpha)
    router_pad = jnp.pad(router_w.astype(F32), ((0, 0), (0, 0), (0, LANES - n_exp)))
    router_hi = router_pad.astype(BF16)
    router_lo = (router_pad - router_hi.astype(F32)).astype(BF16)
    b_mod3 = b_mod.reshape(depth, 1, 6 * d)
    ln1_g3, ln1_b3 = ln1_g.reshape(depth, 1, d), ln1_b.reshape(depth, 1, d)
    ln2_g3, ln2_b3 = ln2_g.reshape(depth, 1, d), ln2_b.reshape(depth, 1, d)

    mods = [_matmul(cond, w_mod, l, 0, 6 * d, bias=b_mod3, act="silu", tn=1024).reshape(8, 6, 1, d)
            for l in range(depth)]
    hs = [_ln_mod(x, mods[0], st["mod_row"]) for x, st in zip(xs, streams)]

    for l in range(depth):
        p.update(win_sink=win_sink[l], q_norm=q_norm[l], k_norm=k_norm[l], s5=_s5_prep(lay, l))
        mod4 = mods[l]
        next_mod4 = mods[l + 1] if l + 1 < depth else None
        streams[1].update(
            ck_w=cache_win_k[:, l].reshape(bs, past, kvw), cv_w=cache_win_v[:, l].reshape(bs, past, kvw),
            ck_g=cache_glob_k[:, l].reshape(bs, past, kvg), cv_g=cache_glob_v[:, l].reshape(bs, past, kvg),
            s0=_state_to_lanes(state_ssm_re[:, l], state_ssm_im[:, l]))

        routed = []
        for si, st in enumerate(streams):
            x = xs[si]
            r, extras = _mixers(x, hs[si], mod4, p, st, l)
            x1, h2, logits = _post_ln(r, mod4, st["mod_row"], ln1_g3, ln1_b3, router_hi, router_lo, l, st["tm"])
            xsel, vals, rank = _route(logits, h2, batch=st["batch"], seq=st["seq"], n_exp=n_exp)
            routed.append((x1, xsel, vals, rank))
            if not st["latent"]:
                k_w, v_w, k_g, v_g, fin = extras
                new["wk"].append(k_w.reshape(bp, lp, -1, HEAD_DIM))
                new["wv"].append(v_w.reshape(bp, lp, -1, HEAD_DIM))
                new["gk"].append(k_g.reshape(bp, lp, -1, HEAD_DIM))
                new["gv"].append(v_g.reshape(bp, lp, -1, HEAD_DIM))
                s_re, s_im = _lanes_to_state(fin, p_n)
                new["sre"].append(s_re)
                new["sim"].append(s_im)
        outs = _ffn(routed[0][1], routed[1][1], routed[0][2], routed[1][2],
                    exp_w_gate, exp_w_up, exp_w_down, l)
        for si, st in enumerate(streams):
            xs[si], hs[si] = _scatter(outs[si], routed[si][3], routed[si][0], mod4, st["mod_row"],
                                      ln2_g3, ln2_b3, l, alpha, next_mod4, batch=st["batch"], seq=st["seq"])

    return (xs[0].reshape(bp, lp, d), xs[1].reshape(bs, ls, d),
            jnp.stack(new["wk"], axis=1), jnp.stack(new["wv"], axis=1),
            jnp.stack(new["gk"], axis=1), jnp.stack(new["gv"], axis=1),
            jnp.stack(new["sre"], axis=1), jnp.stack(new["sim"], axis=1))
```

```python
import functools

import numpy as np
import jax
import jax.numpy as jnp
from jax import lax
from jax.experimental import pallas as pl
from jax.experimental.pallas import tpu as pltpu

F32 = jnp.float32
BF16 = jnp.bfloat16
HIGHEST = lax.Precision.HIGHEST

HEAD_DIM = 128
LANES = 128
GRID_W = 64
WINDOW = 128
EC_FACTOR = 2
ROPE_THETA = 10000.0
EPS = 1e-6
NEG_INF = -1e30
ATTN_SCALE = HEAD_DIM ** -0.5
LOG2E = 1.4426950408889634
S5_CHUNK = 16
S5_OCT = 8
ROW_BLOCK = 256
GATHER_ROWS = 512
ROUTE_SEQS_PER_STEP = 1
VMEM_LIMIT = 60 * 1024 * 1024
RESIDENT = pl.Buffered(1)


def _cparams(n_axes):
    return pltpu.CompilerParams(dimension_semantics=("arbitrary",) * n_axes,
                                vmem_limit_bytes=VMEM_LIMIT)


def _tile(dim, pref):
    return pref if dim % pref == 0 else dim


def _ln(x):
    mu = jnp.mean(x, axis=-1, keepdims=True)
    xc = x - mu
    var = jnp.mean(xc * xc, axis=-1, keepdims=True)
    return xc * lax.rsqrt(var + EPS)


def _split_bf16(x):
    hi = x.astype(BF16)
    return hi, (x - hi.astype(F32)).astype(BF16)


def _dot3(a_hi, a_lo, b_hi, b_lo, dims=(((1,), (0,)), ((), ()))):
    def d(x, y):
        return lax.dot_general(x, y, dims, preferred_element_type=F32)
    return d(a_hi, b_hi) + d(a_lo, b_hi) + d(a_hi, b_lo)


def _dot3_nt(a, b):
    return _dot3(*_split_bf16(a), *_split_bf16(b), dims=(((1,), (1,)), ((), ())))


def _mm_kernel(x_ref, w_ref, *rest, act, has_bias):
    if has_bias:
        b_ref, o_ref, wbf_ref = rest
    else:
        o_ref, wbf_ref = rest

    @pl.when(pl.program_id(1) == 0)
    def _():
        wbf_ref[...] = w_ref[...].astype(BF16)

    x = x_ref[...]
    if act == "silu":
        x = jax.nn.silu(x.astype(F32))
    acc = jnp.dot(x.astype(BF16), wbf_ref[...], preferred_element_type=F32)
    if has_bias:
        acc = acc + b_ref[...]
    o_ref[...] = acc.astype(o_ref.dtype)


def _matmul(x, w, layer, col0, ncols, *, bias=None, act=None, out_dtype=F32, tm=1024, tn=512):
    m, k = x.shape
    tm = _tile(m, tm)
    tn = next(t for t in (tn, 512, 256, LANES) if ncols % t == 0 and col0 % t == 0)
    c0 = col0 // tn
    in_specs = [pl.BlockSpec((tm, k), lambda n, i: (i, 0)),
                pl.BlockSpec((None, k, tn), lambda n, i: (layer, 0, c0 + n))]
    args = [x, w]
    if bias is not None:
        in_specs.append(pl.BlockSpec((None, 1, tn), lambda n, i: (layer, 0, c0 + n)))
        args.append(bias)
    return pl.pallas_call(
        functools.partial(_mm_kernel, act=act, has_bias=bias is not None),
        grid=(ncols // tn, m // tm),
        in_specs=in_specs,
        out_specs=pl.BlockSpec((tm, tn), lambda n, i: (i, n)),
        out_shape=jax.ShapeDtypeStruct((m, ncols), out_dtype),
        scratch_shapes=[pltpu.VMEM((k, tn), BF16)],
        compiler_params=_cparams(2),
    )(*args)


def _ln_mod_kernel(x_ref, sc_ref, sh_ref, h_ref):
    h_ref[...] = (_ln(x_ref[...]) * (1.0 + sc_ref[...]) + sh_ref[...]).astype(h_ref.dtype)


def _mod_spec(d, chunk, mod_row, rows_per_step=ROW_BLOCK):
    scale = rows_per_step // ROW_BLOCK
    return pl.BlockSpec((None, None, 1, d), lambda i: (mod_row(i * scale), chunk, 0, 0))


def _ln_mod(x, mod4, mod_row):
    t, d = x.shape
    return pl.pallas_call(
        _ln_mod_kernel,
        grid=(t // ROW_BLOCK,),
        in_specs=[pl.BlockSpec((ROW_BLOCK, d), lambda i: (i, 0)),
                  _mod_spec(d, 1, mod_row), _mod_spec(d, 0, mod_row)],
        out_specs=pl.BlockSpec((ROW_BLOCK, d), lambda i: (i, 0)),
        out_shape=jax.ShapeDtypeStruct((t, d), BF16),
        compiler_params=_cparams(1),
    )(x, mod4, mod4)


def _rope_tables(seq_len):
    half = HEAD_DIM // 4
    inv = ROPE_THETA ** (-np.arange(half, dtype=np.float64) / half)
    tok = np.arange(seq_len)
    ang_r = (tok // GRID_W)[:, None] * inv[None, :]
    ang_c = (tok % GRID_W)[:, None] * inv[None, :]
    cos = np.concatenate([np.cos(ang_r), np.cos(ang_r), np.cos(ang_c), np.cos(ang_c)], axis=-1)
    sin = np.concatenate([-np.sin(ang_r), np.sin(ang_r), -np.sin(ang_c), np.sin(ang_c)], axis=-1)
    return jnp.asarray(cos, F32), jnp.asarray(sin, F32)


def _rope(x, cos, sin_signed):
    lane = lax.broadcasted_iota(jnp.int32, x.shape, 1)
    swapped = jnp.where((lane % 64) < 32, pltpu.roll(x, 96, 1), pltpu.roll(x, 32, 1))
    return x * cos + swapped * sin_signed


def _rms(x, w):
    return x * lax.rsqrt(jnp.mean(x * x, axis=-1, keepdims=True) + EPS) * w


def _attn_kernel(*refs, heads, kv, tq, rope, norm, sink, ctx, band, emit_k):
    it = iter(refs)
    q_ref, k_ref, v_ref = next(it), next(it), next(it)
    kc_ref = vc_ref = sink_ref = qn_ref = kn_ref = cq_ref = sq_ref = ck_ref = sk_ref = None
    if ctx:
        kc_ref, vc_ref = next(it), next(it)
    if sink:
        sink_ref = next(it)
    if norm:
        qn_ref, kn_ref = next(it), next(it)
    if rope:
        cq_ref, sq_ref, ck_ref, sk_ref = next(it), next(it), next(it), next(it)
    o_ref = next(it)
    ko_ref = next(it) if emit_k else None
    kp_ref, vp_ref = next(it), next(it)

    qi = pl.program_id(1)
    rep = heads // kv

    @pl.when(qi == 0)
    def _():
        for g in range(kv):
            sl = slice(g * HEAD_DIM, (g + 1) * HEAD_DIM)
            kg = k_ref[:, sl]
            if norm:
                kg = _rms(kg, kn_ref[...])
            if rope:
                kg = _rope(kg, ck_ref[...], sk_ref[...])
            if emit_k:
                ko_ref[:, sl] = kg
            kp_ref[:, sl] = kg.astype(BF16)
        vp_ref[...] = v_ref[...].astype(BF16)

    nt = (((1,), (1,)), ((), ()))
    seq = kp_ref.shape[0]
    if band:
        span = min(seq, tq + 2 * max(WINDOW, tq))
        k0 = pl.multiple_of(jnp.clip(qi * tq - max(WINDOW, tq), 0, seq - span), tq)
        rows = pl.ds(k0, span)
        qpos = qi * tq + lax.broadcasted_iota(jnp.int32, (tq, span), 0)
        kpos = k0 + lax.broadcasted_iota(jnp.int32, (tq, span), 1)
        in_band = jnp.abs(qpos - kpos) <= WINDOW
    else:
        rows = slice(None)
    for h in range(heads):
        g = h // rep
        gs = slice(g * HEAD_DIM, (g + 1) * HEAD_DIM)
        qh = q_ref[:, h * HEAD_DIM:(h + 1) * HEAD_DIM]
        if norm:
            qh = _rms(qh, qn_ref[...])
        if rope:
            qh = _rope(qh, cq_ref[...], sq_ref[...])
        qh = (qh * (ATTN_SCALE * LOG2E)).astype(BF16)
        s = lax.dot_general(qh, kp_ref[rows, gs], nt, preferred_element_type=F32)
        if band:
            s = jnp.where(in_band, s, NEG_INF)
        m = jnp.max(s, axis=-1, keepdims=True)
        if ctx:
            sc = lax.dot_general(qh, kc_ref[:, gs].astype(BF16), nt, preferred_element_type=F32)
            m = jnp.maximum(m, jnp.max(sc, axis=-1, keepdims=True))
        if sink:
            m = jnp.maximum(m, sink_ref[h] * LOG2E)
        e = jnp.exp2(s - m)
        den = jnp.sum(e, axis=-1, keepdims=True)
        o = jnp.dot(e.astype(BF16), vp_ref[rows, gs], preferred_element_type=F32)
        if ctx:
            ec = jnp.exp2(sc - m)
            den = den + jnp.sum(ec, axis=-1, keepdims=True)
            o = o + jnp.dot(ec.astype(BF16), vc_ref[:, gs].astype(BF16), preferred_element_type=F32)
        if sink:
            den = den + jnp.exp2(sink_ref[h] * LOG2E - m)
        o_ref[:, h * HEAD_DIM:(h + 1) * HEAD_DIM] = (o / den).astype(o_ref.dtype)


def _attention(qa, q0, ka, k0, va, v0, *, heads, kv, batch, seq, k_ctx=None, v_ctx=None, sink=None,
               q_norm=None, k_norm=None, rope=False, band=False, emit_k=False):
    t = qa.shape[0]
    qw, kw = heads * HEAD_DIM, kv * HEAD_DIM
    assert q0 % qw == 0 and k0 % kw == 0 and v0 % kw == 0
    qc, kc, vc = q0 // qw, k0 // kw, v0 // kw
    tq = _tile(seq, 256)
    nq = seq // tq
    ctx, has_sink, norm = k_ctx is not None, sink is not None, q_norm is not None
    in_specs = [pl.BlockSpec((tq, qw), lambda b, i: (b * nq + i, qc)),
                pl.BlockSpec((seq, kw), lambda b, i: (b, kc)),
                pl.BlockSpec((seq, kw), lambda b, i: (b, vc))]
    args = [qa, ka, va]
    if ctx:
        lc = k_ctx.shape[1]
        in_specs += [pl.BlockSpec((None, lc, kw), lambda b, i: (b, 0, 0))] * 2
        args += [k_ctx, v_ctx]
    if has_sink:
        in_specs.append(pl.BlockSpec(memory_space=pltpu.SMEM))
        args.append(sink)
    if norm:
        in_specs += [pl.BlockSpec((1, HEAD_DIM), lambda b, i: (0, 0))] * 2
        args += [q_norm.reshape(1, HEAD_DIM), k_norm.reshape(1, HEAD_DIM)]
    if rope:
        cos, sin = _rope_tables(seq)
        in_specs += [pl.BlockSpec((tq, HEAD_DIM), lambda b, i: (i, 0))] * 2
        in_specs += [pl.BlockSpec((seq, HEAD_DIM), lambda b, i: (0, 0))] * 2
        args += [cos, sin, cos, sin]
    out_specs = [pl.BlockSpec((tq, qw), lambda b, i: (b * nq + i, 0))]
    out_shape = [jax.ShapeDtypeStruct((t, qw), BF16)]
    if emit_k:
        out_specs.append(pl.BlockSpec((seq, kw), lambda b, i: (b, 0)))
        out_shape.append(jax.ShapeDtypeStruct((t, kw), F32))
    res = pl.pallas_call(
        functools.partial(_attn_kernel, heads=heads, kv=kv, tq=tq, rope=rope, norm=norm,
                          sink=has_sink, ctx=ctx, band=band, emit_k=emit_k),
        grid=(batch, nq),
        in_specs=in_specs,
        out_specs=out_specs,
        out_shape=out_shape,
        scratch_shapes=[pltpu.VMEM((seq, kw), BF16), pltpu.VMEM((seq, kw), BF16)],
        compiler_params=_cparams(2),
    )(*args)
    return res if emit_k else res[0]


def _s5_layouts(log_dt, a_re, a_im, b_re, b_im, c_re, c_im):
    dep, _, g_n, p_n = a_re.shape
    gc = b_re.shape[-1]
    no = g_n // S5_OCT
    sw = S5_OCT * p_n
    ldt = jnp.broadcast_to(log_dt[..., None], a_re.shape)
    prm = jnp.stack([ldt, a_re, a_im], axis=2).astype(F32)
    prow = jnp.pad(prm.reshape(dep, 2, 3, no, sw).transpose(0, 1, 3, 2, 4),
                   ((0, 0),) * 3 + ((0, 5), (0, 0)))
    pcmp = jnp.broadcast_to(prm[:, :, :, :, None, :], (dep, 2, 3, g_n, gc, p_n))
    pcmp = pcmp.reshape(dep, 2, 3, no, S5_OCT * gc, p_n).transpose(0, 1, 3, 2, 4, 5)

    def b_views(b):
        bt = jnp.swapaxes(b.astype(F32), -1, -2).reshape(dep, 2, no, S5_OCT * gc, p_n)
        return bt, jnp.tile(bt, (1, 1, 1, 1, S5_OCT))

    def c_views(c):
        cc = c.astype(F32).reshape(dep, 2, no, S5_OCT * gc, p_n)
        ct = jnp.swapaxes(c.astype(F32), -1, -2).reshape(dep, 2, no, sw, gc)
        return cc, jnp.tile(ct, (1, 1, 1, 1, S5_OCT))

    bc_re, bt_re = b_views(b_re)
    bc_im, bt_im = b_views(b_im)
    cc_re, ct_re = c_views(c_re)
    cc_im, ct_im = c_views(c_im)
    return dict(prow=prow, pcmp=pcmp, bc_re=bc_re, bc_im=bc_im, bt_re=bt_re, bt_im=bt_im,
                cc_re=cc_re, cc_im=cc_im, ct_re=ct_re, ct_im=ct_im)


def _lam_powers(ldt, ar, ai, k):
    dt = jnp.exp(ldt)
    mag = jnp.exp(k * (dt * ar))
    return mag * jnp.cos(k * (dt * ai)), mag * jnp.sin(k * (dt * ai))


def _bbar_coef(ldt, ar, ai):
    lam_re, lam_im = _lam_powers(ldt, ar, ai, 1.0)
    den = ar * ar + ai * ai
    return ((lam_re - 1.0) * ar + lam_im * ai) / den, (lam_im * ar - (lam_re - 1.0) * ai) / den


def _same_group(shape, rows_per_group, cols_per_group):
    r = lax.broadcasted_iota(jnp.int32, shape, 0) // rows_per_group
    c = lax.broadcasted_iota(jnp.int32, shape, 1) // cols_per_group
    return r == c


def _s5_prep_w_kernel(prow_ref, btr_ref, bti_ref, w_ref, lam_ref):
    tc = S5_CHUNK
    rows, sw = btr_ref.shape[1], btr_ref.shape[2]
    mask = _same_group((rows, sw), rows // S5_OCT, sw // S5_OCT)
    k = lax.broadcasted_iota(jnp.int32, (24, sw), 0).astype(F32)
    for d in range(2):
        ldt, ar, ai = prow_ref[d, 0:1, :], prow_ref[d, 1:2, :], prow_ref[d, 2:3, :]
        pw_re, pw_im = _lam_powers(ldt, ar, ai, k)
        co_re, co_im = _bbar_coef(ldt, ar, ai)
        bb_re = jnp.where(mask, co_re * btr_ref[d] - co_im * bti_ref[d], 0.0)
        bb_im = jnp.where(mask, co_re * bti_ref[d] + co_im * btr_ref[d], 0.0)
        lam_ref[2 * d:2 * d + 1, :] = pw_re[tc:tc + 1]
        lam_ref[2 * d + 1:2 * d + 2, :] = pw_im[tc:tc + 1]
        for j in range(tc):
            kk = tc - 1 - j if d == 0 else j
            pr, pi = pw_re[kk:kk + 1], pw_im[kk:kk + 1]
            rs = slice(j * rows, (j + 1) * rows)
            w_ref[rs, (2 * d) * sw:(2 * d + 1) * sw] = (pr * bb_re - pi * bb_im).astype(w_ref.dtype)
            w_ref[rs, (2 * d + 1) * sw:(2 * d + 2) * sw] = (pr * bb_im + pi * bb_re).astype(w_ref.dtype)


def _row_to_col(row, eye):
    return jnp.sum(jnp.where(eye, row, 0.0), axis=1, keepdims=True)


def _s5_prep_z_kernel(prow_ref, ctr_ref, cti_ref, z_ref):
    tc = S5_CHUNK
    sw, cols = ctr_ref.shape[1], ctr_ref.shape[2]
    mask = _same_group((sw, cols), sw // S5_OCT, cols // S5_OCT)
    eye = _same_group((sw, sw), 1, 1)
    for d in range(2):
        lam_re, lam_im = _lam_powers(prow_ref[d, 0:1, :], prow_ref[d, 1:2, :], prow_ref[d, 2:3, :], 1.0)
        lr = jnp.broadcast_to(_row_to_col(lam_re, eye), (sw, cols))
        li = jnp.broadcast_to(_row_to_col(lam_im, eye), (sw, cols))
        c_re = jnp.where(mask, ctr_ref[d], 0.0)
        c_im = jnp.where(mask, cti_ref[d], 0.0)
        z_re, z_im = c_re * lr - c_im * li, c_re * li + c_im * lr
        for step in range(tc):
            t = step if d == 0 else tc - 1 - step
            cs = slice(t * cols, (t + 1) * cols)
            z_ref[(2 * d) * sw:(2 * d + 1) * sw, cs] = z_re.astype(z_ref.dtype)
            z_ref[(2 * d + 1) * sw:(2 * d + 2) * sw, cs] = (-z_im).astype(z_ref.dtype)
            z_re, z_im = z_re * lr - z_im * li, z_re * li + z_im * lr


def _s5_prep_m_kernel(pcmp_ref, bcr_ref, bci_ref, ccr_ref, cci_ref, m_ref, xr_ref, xi_ref, taps_ref):
    tc = S5_CHUNK
    rows = bcr_ref.shape[1]
    mask = _same_group((rows, rows), rows // S5_OCT, rows // S5_OCT)
    for d in range(2):
        ldt, ar, ai = pcmp_ref[d, 0], pcmp_ref[d, 1], pcmp_ref[d, 2]
        lam_re, lam_im = _lam_powers(ldt, ar, ai, 1.0)
        co_re, co_im = _bbar_coef(ldt, ar, ai)
        x_re = co_re * bcr_ref[d] - co_im * bci_ref[d]
        x_im = co_re * bci_ref[d] + co_im * bcr_ref[d]
        for kk in range(tc):
            xr_ref[kk * rows:(kk + 1) * rows, :] = x_re
            xi_ref[kk * rows:(kk + 1) * rows, :] = x_im
            x_re, x_im = x_re * lam_re - x_im * lam_im, x_re * lam_im + x_im * lam_re
        taps = _dot3_nt(xr_ref[...], ccr_ref[d]) - _dot3_nt(xi_ref[...], cci_ref[d])
        for kk in range(tc):
            tile = jnp.where(mask, taps[kk * rows:(kk + 1) * rows, :], 0.0)
            col = (tc - 1 + kk) if d == 0 else (tc - 1 - kk)
            cs = slice(col * rows, (col + 1) * rows)
            if d == 1 and kk == 0:
                taps_ref[:, cs] += tile
            else:
                taps_ref[:, cs] = tile
    for j in range(tc):
        m_ref[j * rows:(j + 1) * rows, :] = taps_ref[:, (tc - 1 - j) * rows:(2 * tc - 1 - j) * rows].astype(m_ref.dtype)


def _s5_prep(lay, layer):
    _, _, no, rows, sw = lay["bt_re"].shape
    p_n = lay["bc_re"].shape[-1]
    tc = S5_CHUNK
    kw = tc * rows

    def per_octet(*tail):
        nd = len(tail)
        return pl.BlockSpec((None, 2, None) + tail, lambda o: (layer, 0, o) + (0,) * nd)

    def out(r, c):
        return pl.BlockSpec((None, r, c), lambda o: (o, 0, 0))

    w_o, lam_o = pl.pallas_call(
        _s5_prep_w_kernel, grid=(no,),
        in_specs=[per_octet(8, sw), per_octet(rows, sw), per_octet(rows, sw)],
        out_specs=[out(kw, 4 * sw), out(4, sw)],
        out_shape=[jax.ShapeDtypeStruct((no, kw, 4 * sw), BF16), jax.ShapeDtypeStruct((no, 4, sw), F32)],
        compiler_params=_cparams(1),
    )(lay["prow"], lay["bt_re"], lay["bt_im"])
    z_o = pl.pallas_call(
        _s5_prep_z_kernel, grid=(no,),
        in_specs=[per_octet(8, sw), per_octet(sw, rows), per_octet(sw, rows)],
        out_specs=out(4 * sw, kw),
        out_shape=jax.ShapeDtypeStruct((no, 4 * sw, kw), BF16),
        compiler_params=_cparams(1),
    )(lay["prow"], lay["ct_re"], lay["ct_im"])
    m_o = pl.pallas_call(
        _s5_prep_m_kernel, grid=(no,),
        in_specs=[per_octet(3, rows, p_n)] + [per_octet(rows, p_n)] * 4,
        out_specs=out(kw, kw),
        out_shape=jax.ShapeDtypeStruct((no, kw, kw), BF16),
        scratch_shapes=[pltpu.VMEM((kw, p_n), F32), pltpu.VMEM((kw, p_n), F32),
                        pltpu.VMEM((rows, (2 * tc - 1) * rows), F32)],
        compiler_params=_cparams(1),
    )(lay["pcmp"], lay["bc_re"], lay["bc_im"], lay["cc_re"], lay["cc_im"])
    return m_o, w_o, z_o, lam_o


def _s5_state_kernel(u_ref, w_ref, lam_ref, s0_ref, sp_ref, fin_ref, ubf_ref, vs_ref, *, batch, nchunk):
    tc = S5_CHUNK
    r = batch * nchunk
    for t in range(tc):
        ubf_ref[:, t * LANES:(t + 1) * LANES] = u_ref[pl.ds(t, r, stride=tc), :].astype(BF16)
    q = vs_ref.shape[-1] // 4
    vs_ref[...] = jnp.dot(ubf_ref[...], w_ref[...],
                          preferred_element_type=F32).reshape(batch, nchunk, 4 * q)

    def scan(part, order):
        a, b = lam_ref[part:part + 1, :], lam_ref[part + 1:part + 2, :]
        re, im = slice(part * q, (part + 1) * q), slice((part + 1) * q, (part + 2) * q)
        s_re, s_im = s0_ref[:, re], s0_ref[:, im]
        for c in order:
            v_re, v_im = vs_ref[:, c, re], vs_ref[:, c, im]
            vs_ref[:, c, re] = s_re
            vs_ref[:, c, im] = s_im
            s_re, s_im = a * s_re - b * s_im + v_re, a * s_im + b * s_re + v_im
        fin_ref[:, re] = s_re
        fin_ref[:, im] = s_im

    scan(0, range(nchunk))
    scan(2, range(nchunk - 1, -1, -1))
    sp_ref[...] = vs_ref[...].reshape(r, 4 * q).astype(sp_ref.dtype)


def _s5_out_kernel(ubf_ref, sp_ref, m_ref, z_ref, y_ref):
    tc = S5_CHUNK
    tr = ubf_ref.shape[0]
    y = (jnp.dot(ubf_ref[...], m_ref[...], preferred_element_type=F32)
         + jnp.dot(sp_ref[...], z_ref[...], preferred_element_type=F32))
    for t in range(tc):
        y_ref[pl.ds(t, tr, stride=tc), :] = y[:, t * LANES:(t + 1) * LANES]


def _s5_mix(ua, u0, width, m_o, w_o, z_o, lam_o, s0, *, batch, seq):
    t = ua.shape[0]
    tc = S5_CHUNK
    no = width // LANES
    assert u0 % LANES == 0
    uc = u0 // LANES
    r = t // tc
    nchunk = seq // tc
    sw = w_o.shape[-1]
    kw = tc * LANES
    sp, fin, ubf = pl.pallas_call(
        functools.partial(_s5_state_kernel, batch=batch, nchunk=nchunk),
        grid=(no,),
        in_specs=[pl.BlockSpec((t, LANES), lambda o: (0, uc + o)),
                  pl.BlockSpec((None, kw, sw), lambda o: (o, 0, 0)),
                  pl.BlockSpec((None, 4, sw // 4), lambda o: (o, 0, 0)),
                  pl.BlockSpec((None, batch, sw), lambda o: (o, 0, 0))],
        out_specs=[pl.BlockSpec((None, r, sw), lambda o: (o, 0, 0)),
                   pl.BlockSpec((None, batch, sw), lambda o: (o, 0, 0)),
                   pl.BlockSpec((None, r, kw), lambda o: (o, 0, 0))],
        out_shape=[jax.ShapeDtypeStruct((no, r, sw), BF16),
                   jax.ShapeDtypeStruct((no, batch, sw), F32),
                   jax.ShapeDtypeStruct((no, r, kw), BF16)],
        scratch_shapes=[pltpu.VMEM((batch, nchunk, sw), F32)],
        compiler_params=_cparams(1),
    )(ua, w_o, lam_o, s0)
    tr = _tile(r, 512)
    y = pl.pallas_call(
        _s5_out_kernel,
        grid=(no, r // tr),
        in_specs=[pl.BlockSpec((None, tr, kw), lambda o, i: (o, i, 0)),
                  pl.BlockSpec((None, tr, sw), lambda o, i: (o, i, 0)),
                  pl.BlockSpec((None, kw, kw), lambda o, i: (o, 0, 0)),
                  pl.BlockSpec((None, sw, kw), lambda o, i: (o, 0, 0))],
        out_specs=pl.BlockSpec((tr * tc, LANES), lambda o, i: (i, o)),
        out_shape=jax.ShapeDtypeStruct((t, width), F32),
        compiler_params=_cparams(2),
    )(ubf, sp, m_o, z_o)
    return y, fin


def _state_to_lanes(s_re, s_im):
    b, _, g_n, p_n = s_re.shape
    no = g_n // S5_OCT
    parts = jnp.stack([s_re[:, 0], s_im[:, 0], s_re[:, 1], s_im[:, 1]], axis=1)
    parts = parts.reshape(b, 4, no, S5_OCT * p_n)
    return jnp.transpose(parts, (2, 0, 1, 3)).reshape(no, b, 4 * S5_OCT * p_n).astype(F32)


def _lanes_to_state(fin, p_n):
    no, b, _ = fin.shape
    parts = jnp.transpose(fin.reshape(no, b, 4, S5_OCT, p_n), (1, 2, 0, 3, 4)).reshape(b, 4, no * S5_OCT, p_n)
    return jnp.stack([parts[:, 0], parts[:, 2]], axis=1), jnp.stack([parts[:, 1], parts[:, 3]], axis=1)


def _glu_kernel(y_ref, ua_ref, ub_ref, d_ref, w_ref, o_ref):
    u = jnp.concatenate([ua_ref[...], ub_ref[...]], axis=1)
    z = jax.nn.gelu(y_ref[...] + d_ref[...] * u).astype(BF16)
    g = jnp.dot(z, w_ref[...], preferred_element_type=F32)
    w = o_ref.shape[-1]
    o_ref[...] = (g[:, :w] * jax.nn.sigmoid(g[:, w:])).astype(o_ref.dtype)


def _glu(y, ua, u0, d, w_glu, layer):
    t, width = y.shape
    half = width // 2
    assert u0 % half == 0
    uc = u0 // half
    tm = _tile(t, 512)
    return pl.pallas_call(
        _glu_kernel,
        grid=(t // tm,),
        in_specs=[pl.BlockSpec((tm, width), lambda i: (i, 0)),
                  pl.BlockSpec((tm, half), lambda i: (i, uc)),
                  pl.BlockSpec((tm, half), lambda i: (i, uc + 1)),
                  pl.BlockSpec((None, 1, width), lambda i: (layer, 0, 0)),
                  pl.BlockSpec((None, width, 2 * width), lambda i: (layer, 0, 0), pipeline_mode=RESIDENT)],
        out_specs=pl.BlockSpec((tm, width), lambda i: (i, 0)),
        out_shape=jax.ShapeDtypeStruct((t, width), BF16),
        compiler_params=_cparams(1),
    )(y, ua, ua, d, w_glu)


def _merge_kernel(ow_ref, os_ref, og_ref, gw_ref, gs_ref, gg_ref, x_ref, g1_ref,
                  ww_ref, ws_ref, wg_ref, wo_ref, r_ref, *, alpha):
    def gate(g_ref):
        return jax.nn.sigmoid(g_ref[...].astype(F32))

    y = (gate(gw_ref) * jnp.dot(ow_ref[...], ww_ref[...], preferred_element_type=F32)
         + gate(gs_ref) * jnp.dot(os_ref[...], ws_ref[...], preferred_element_type=F32)
         + gate(gg_ref) * jnp.dot(og_ref[...], wg_ref[...], preferred_element_type=F32))
    o = jnp.dot(y.astype(BF16), wo_ref[...], preferred_element_type=F32)
    r_ref[...] = alpha * x_ref[...] + g1_ref[...] * o


def _merge(o_win, o_ssm, o_glob, gates, x, mod4, mod_row, w_win, w_ssm, w_glob, w_out, layer, alpha, tm):
    t, d = x.shape

    def rows(w):
        return pl.BlockSpec((tm, w), lambda i: (i, 0))

    def whole(a):
        return pl.BlockSpec((None,) + a.shape[1:], lambda i: (layer, 0, 0), pipeline_mode=RESIDENT)

    return pl.pallas_call(
        functools.partial(_merge_kernel, alpha=alpha),
        grid=(t // tm,),
        in_specs=[rows(o_win.shape[1]), rows(o_ssm.shape[1]), rows(o_glob.shape[1]),
                  pl.BlockSpec((tm, d), lambda i: (i, 0)),
                  pl.BlockSpec((tm, d), lambda i: (i, 1)),
                  pl.BlockSpec((tm, d), lambda i: (i, 2)),
                  rows(d), _mod_spec(d, 2, mod_row, tm),
                  whole(w_win), whole(w_ssm), whole(w_glob), whole(w_out)],
        out_specs=rows(d),
        out_shape=jax.ShapeDtypeStruct((t, d), F32),
        compiler_params=_cparams(1),
    )(o_win, o_ssm, o_glob, gates, gates, gates, x, mod4, w_win, w_ssm, w_glob, w_out)


def _post_ln_kernel(r_ref, sc2_ref, sh2_ref, lg_ref, lb_ref, rwh_ref, rwl_ref, x1_ref, h2_ref, logit_ref):
    x1 = _ln(r_ref[...]) * lg_ref[...] + lb_ref[...]
    x1_ref[...] = x1
    h2 = _ln(x1) * (1.0 + sc2_ref[...]) + sh2_ref[...]
    hi, lo = _split_bf16(h2)
    h2_ref[...] = hi
    logit_ref[...] = _dot3(hi, lo, rwh_ref[...], rwl_ref[...])


def _post_ln(r, mod4, mod_row, ln_g, ln_b, router_hi, router_lo, layer, tm):
    t, d = r.shape

    def rows(w):
        return pl.BlockSpec((tm, w), lambda i: (i, 0))

    def vec():
        return pl.BlockSpec((None, 1, d), lambda i: (layer, 0, 0))

    def router():
        return pl.BlockSpec((None, d, LANES), lambda i: (layer, 0, 0), pipeline_mode=RESIDENT)

    return pl.pallas_call(
        _post_ln_kernel,
        grid=(t // tm,),
        in_specs=[rows(d), _mod_spec(d, 4, mod_row, tm), _mod_spec(d, 3, mod_row, tm), vec(), vec(),
                  router(), router()],
        out_specs=[rows(d), rows(d), rows(LANES)],
        out_shape=[jax.ShapeDtypeStruct((t, d), F32), jax.ShapeDtypeStruct((t, d), BF16),
                   jax.ShapeDtypeStruct((t, LANES), F32)],
        compiler_params=_cparams(1),
    )(r, mod4, mod4, ln_g, ln_b, router_hi, router_lo)


def _route_kernel(logit_ref, h_ref, xsel_ref, vals_ref, rank_ref, *, n_exp, cap):
    nsub, _, n = rank_ref.shape
    jj = lax.broadcasted_iota(jnp.int32, (n, n), 0)
    ii = lax.broadcasted_iota(jnp.int32, (n, n), 1)
    slot = lax.broadcasted_iota(jnp.int32, (cap, n), 0)
    for s in range(nsub):
        tok = slice(s * n, (s + 1) * n)
        lg = logit_ref[tok, :]
        col = lax.broadcasted_iota(jnp.int32, lg.shape, 1)
        lg = jnp.where(col < n_exp, lg, -jnp.inf)
        ex = jnp.exp(lg - jnp.max(lg, axis=-1, keepdims=True))
        aff = ex / jnp.sum(ex, axis=-1, keepdims=True)
        aff_t = aff.T
        picks = []
        for e in range(n_exp):
            a_row = aff_t[e:e + 1, :]
            a_col = aff[:, e:e + 1]
            beats = (a_col > a_row) | ((a_col == a_row) & (jj < ii))
            rank = jnp.sum(beats.astype(jnp.int32), axis=0, keepdims=True)
            pick = slot == rank
            picks.append(pick.astype(BF16))
            vals_ref[e, s * cap:(s + 1) * cap, :] = jnp.sum(jnp.where(pick, a_row, 0.0), axis=1, keepdims=True)
            rank_ref[s, e:e + 1, :] = rank
        group = min(n_exp, max(1, GATHER_ROWS // cap))
        for e0 in range(0, n_exp, group):
            sel = jnp.dot(jnp.concatenate(picks[e0:e0 + group], axis=0), h_ref[tok, :],
                          preferred_element_type=F32)
            for e in range(e0, min(e0 + group, n_exp)):
                xsel_ref[e, s * cap:(s + 1) * cap, :] = sel[(e - e0) * cap:(e - e0 + 1) * cap, :].astype(
                    xsel_ref.dtype)


def _route(logits, h2, *, batch, seq, n_exp):
    t, d = h2.shape
    cap = EC_FACTOR * seq // n_exp
    nsub = ROUTE_SEQS_PER_STEP if seq <= ROW_BLOCK and batch % ROUTE_SEQS_PER_STEP == 0 else 1
    return pl.pallas_call(
        functools.partial(_route_kernel, n_exp=n_exp, cap=cap),
        grid=(batch // nsub,),
        in_specs=[pl.BlockSpec((nsub * seq, LANES), lambda b: (b, 0)),
                  pl.BlockSpec((nsub * seq, d), lambda b: (b, 0))],
        out_specs=[pl.BlockSpec((n_exp, nsub * cap, d), lambda b: (0, b, 0)),
                   pl.BlockSpec((n_exp, nsub * cap, 1), lambda b: (0, b, 0)),
                   pl.BlockSpec((nsub, n_exp, seq), lambda b: (b, 0, 0))],
        out_shape=[jax.ShapeDtypeStruct((n_exp, batch * cap, d), BF16),
                   jax.ShapeDtypeStruct((n_exp, batch * cap, 1), F32),
                   jax.ShapeDtypeStruct((batch, n_exp, seq), jnp.int32)],
        compiler_params=_cparams(1),
    )(logits, h2)


def _ffn_hidden_kernel(xp_ref, xs_ref, wg_ref, wu_ref, hp_ref, hs_ref):
    wg = wg_ref[...].astype(BF16)
    wu = wu_ref[...].astype(BF16)
    for x_ref, h_ref in ((xp_ref, hp_ref), (xs_ref, hs_ref)):
        x = x_ref[...]
        h_ref[...] = (jax.nn.silu(jnp.dot(x, wg, preferred_element_type=F32))
                      * jnp.dot(x, wu, preferred_element_type=F32)).astype(h_ref.dtype)


def _ffn_down_kernel(hp_ref, hs_ref, vp_ref, vs_ref, wd_ref, op_ref, os_ref):
    wd = wd_ref[...].astype(BF16)
    for h_ref, v_ref, o_ref in ((hp_ref, vp_ref, op_ref), (hs_ref, vs_ref, os_ref)):
        o_ref[...] = (jnp.dot(h_ref[...], wd, preferred_element_type=F32) * v_ref[...]).astype(o_ref.dtype)


def _ffn(xsel_p, xsel_s, vals_p, vals_s, w_gate, w_up, w_down, layer):
    n_exp, rp, d = xsel_p.shape
    rs = xsel_s.shape[1]
    ff = w_gate.shape[-1]
    tf = _tile(ff, 512)
    tn = _tile(d, 1024)

    def per_expert(r, w):
        return pl.BlockSpec((None, r, w), lambda e, j: (e, 0, 0))

    def col_tile(r, w):
        return pl.BlockSpec((None, r, w), lambda e, j: (e, 0, j))

    def w_tile(k, w):
        return pl.BlockSpec((None, None, k, w), lambda e, j: (layer, e, 0, j))

    hid_p, hid_s = pl.pallas_call(
        _ffn_hidden_kernel,
        grid=(n_exp, ff // tf),
        in_specs=[per_expert(rp, d), per_expert(rs, d), w_tile(d, tf), w_tile(d, tf)],
        out_specs=[col_tile(rp, tf), col_tile(rs, tf)],
        out_shape=[jax.ShapeDtypeStruct((n_exp, rp, ff), BF16), jax.ShapeDtypeStruct((n_exp, rs, ff), BF16)],
        compiler_params=_cparams(2),
    )(xsel_p, xsel_s, w_gate, w_up)
    return pl.pallas_call(
        _ffn_down_kernel,
        grid=(n_exp, d // tn),
        in_specs=[per_expert(rp, ff), per_expert(rs, ff), per_expert(rp, 1), per_expert(rs, 1),
                  w_tile(ff, tn)],
        out_specs=[col_tile(rp, tn), col_tile(rs, tn)],
        out_shape=[jax.ShapeDtypeStruct((n_exp, rp, d), BF16), jax.ShapeDtypeStruct((n_exp, rs, d), BF16)],
        compiler_params=_cparams(2),
    )(hid_p, hid_s, vals_p, vals_s, w_down)


def _scatter_kernel(out_ref, rank_ref, x_ref, g2_ref, lg_ref, lb_ref, *rest, alpha, emit_h):
    n_exp, cap, d = out_ref.shape
    n = x_ref.shape[0]
    slot = lax.broadcasted_iota(jnp.int32, (cap, n), 0)
    pick = jnp.concatenate([(slot == rank_ref[e:e + 1, :]).astype(F32) for e in range(n_exp)], axis=0)
    f = jnp.dot(pick.T.astype(BF16), out_ref[...].reshape(n_exp * cap, d), preferred_element_type=F32)
    x2 = _ln(alpha * x_ref[...] + g2_ref[...] * f) * lg_ref[...] + lb_ref[...]
    if emit_h:
        sc_ref, sh_ref, x2_ref, h_ref = rest
        h_ref[...] = (_ln(x2) * (1.0 + sc_ref[...]) + sh_ref[...]).astype(h_ref.dtype)
    else:
        x2_ref, = rest
    x2_ref[...] = x2


def _scatter(out, rank, x1, mod4, mod_row, ln_g, ln_b, layer, alpha, next_mod4, *, batch, seq):
    n_exp, _, d = out.shape
    cap = out.shape[1] // batch
    blocks_per_seq = seq // ROW_BLOCK
    emit_h = next_mod4 is not None

    ts = _tile(seq, 512)
    nts = seq // ts

    def mod(chunk):
        return pl.BlockSpec((None, None, 1, d), lambda b, j: (mod_row(b * blocks_per_seq), chunk, 0, 0))

    rows = pl.BlockSpec((ts, d), lambda b, j: (b * nts + j, 0))
    in_specs = [pl.BlockSpec((n_exp, cap, d), lambda b, j: (0, b, 0)),
                pl.BlockSpec((None, n_exp, ts), lambda b, j: (b, 0, j)),
                rows, mod(5),
                pl.BlockSpec((None, 1, d), lambda b, j: (layer, 0, 0)),
                pl.BlockSpec((None, 1, d), lambda b, j: (layer, 0, 0))]
    args = [out, rank, x1, mod4, ln_g, ln_b]
    out_specs = [rows]
    out_shape = [jax.ShapeDtypeStruct((batch * seq, d), F32)]
    if emit_h:
        in_specs += [mod(1), mod(0)]
        args += [next_mod4, next_mod4]
        out_specs.append(rows)
        out_shape.append(jax.ShapeDtypeStruct((batch * seq, d), BF16))
    res = pl.pallas_call(
        functools.partial(_scatter_kernel, alpha=alpha, emit_h=emit_h),
        grid=(batch, nts),
        in_specs=in_specs,
        out_specs=out_specs,
        out_shape=out_shape,
        compiler_params=_cparams(2),
    )(*args)
    return (res[0], res[1]) if emit_h else (res[0], None)


def _mixers(x, h, mod4, p, st, layer):
    batch, seq = st["batch"], st["seq"]
    dims = p["dims"]
    hw, kvw, hg, kvg, ssm_w = dims["heads_win"], dims["kv_win"], dims["heads_glob"], dims["kv_glob"], dims["ssm_w"]
    wa = (hw + 2 * kvw) * HEAD_DIM
    wb = (hg + 2 * kvg) * HEAD_DIM
    pa = _matmul(h, p["w_in"], layer, 0, wa + ssm_w, tn=(wa + ssm_w) // 2)
    pb = _matmul(h, p["w_in"], layer, wa + ssm_w, wb)
    gates = _matmul(h, p["w_in"], layer, wa + ssm_w + wb, p["w_in"].shape[-1] - wa - ssm_w - wb,
                    out_dtype=BF16, tn=1024)
    k_w0, v_w0 = hw * HEAD_DIM, (hw + kvw) * HEAD_DIM
    k_g0, v_g0 = hg * HEAD_DIM, (hg + kvg) * HEAD_DIM
    win = dict(heads=hw, kv=kvw, batch=batch, seq=seq, sink=p["win_sink"])
    glob = dict(heads=hg, kv=kvg, batch=batch, seq=seq, q_norm=p["q_norm"], k_norm=p["k_norm"])
    if st["latent"]:
        o_win = _attention(pa, 0, pa, k_w0, pa, v_w0, k_ctx=st["ck_w"], v_ctx=st["cv_w"],
                           rope=True, band=True, **win)
        o_glob = _attention(pb, 0, pb, k_g0, pb, v_g0, k_ctx=st["ck_g"], v_ctx=st["cv_g"], rope=True, **glob)
        k_g = None
    else:
        o_win = _attention(pa, 0, pa, k_w0, pa, v_w0, **win)
        o_glob, k_g = _attention(pb, 0, pb, k_g0, pb, v_g0, emit_k=True, **glob)
    y_ssm, fin = _s5_mix(pa, wa, ssm_w, *p["s5"], st["s0"], batch=batch, seq=seq)
    o_ssm = _glu(y_ssm, pa, wa, p["ssm_d"], p["ssm_w_glu"], layer)
    r = _merge(o_win, o_ssm, o_glob, gates, x, mod4, st["mod_row"], p["w_up_win"], p["w_up_ssm"],
               p["w_up_glob"], p["w_out"], layer, p["alpha"], st["tm"])
    extras = (pa[:, k_w0:v_w0], pa[:, v_w0:wa], k_g, pb[:, v_g0:v_g0 + kvg * HEAD_DIM], fin)
    return r, extras


def kernel(x_prompt, x_sample, cache_win_k, cache_win_v, cache_glob_k, cache_glob_v, state_ssm_re, state_ssm_im, c, c_ctx, w_mod, b_mod, w_in, win_sink, ssm_a_re, ssm_a_im, ssm_log_dt, ssm_b_re, ssm_b_im, ssm_c_re, ssm_c_im, ssm_d, ssm_w_glu, q_norm, k_norm, w_up_win, w_up_ssm, w_up_glob, w_out, ln1_g, ln1_b, ln2_g, ln2_b, router_w, exp_w_gate, exp_w_up, exp_w_down):
    bp, lp, d = x_prompt.shape
    bs, ls, _ = x_sample.shape
    depth = w_mod.shape[0]
    past = cache_win_k.shape[2]
    kv_win, kv_glob = cache_win_k.shape[3], cache_glob_k.shape[3]
    kvw, kvg = kv_win * HEAD_DIM, kv_glob * HEAD_DIM
    ssm_w = w_up_ssm.shape[1]
    n_exp = router_w.shape[-1]
    p_n = ssm_a_re.shape[-1]
    alpha = (2.0 * depth) ** 0.25
    dims = dict(heads_win=w_up_win.shape[1] // HEAD_DIM, kv_win=kv_win,
                heads_glob=w_up_glob.shape[1] // HEAD_DIM, kv_glob=kv_glob, ssm_w=ssm_w)
    assert lp % ROW_BLOCK == 0 and ls % ROW_BLOCK == 0 and 1 + bs <= 8

    cond = jnp.zeros((8, d), F32).at[0].set(c_ctx).at[1:1 + bs].set(c)
    sample_blocks = ls // ROW_BLOCK
    streams = [
        dict(batch=bp, seq=lp, latent=False, mod_row=lambda i: 0, tm=_tile(bp * lp, 512),
             s0=jnp.zeros((ssm_w // LANES, bp, 4 * S5_OCT * p_n), F32)),
        dict(batch=bs, seq=ls, latent=True, mod_row=lambda i: 1 + i // sample_blocks, tm=_tile(ls, 512)),
    ]
    xs = [x_prompt.reshape(bp * lp, d), x_sample.reshape(bs * ls, d)]
    new = {k: [] for k in ("wk", "wv", "gk", "gv", "sre", "sim")}

    lay = _s5_layouts(ssm_log_dt, ssm_a_re, ssm_a_im, ssm_b_re, ssm_b_im, ssm_c_re, ssm_c_im)
    p = dict(w_in=w_in, dims=dims, ssm_d=ssm_d.reshape(depth, 1, ssm_w),
             ssm_w_glu=ssm_w_glu.astype(BF16), w_up_win=w_up_win.astype(BF16),
             w_up_ssm=w_up_ssm.astype(BF16), w_up_glob=w_up_glob.astype(BF16))
    p.update(w_out=w_out.astype(BF16), alpha=alpha)
    router_pad = jnp.pad(router_w.astype(F32), ((0, 0), (0, 0), (0, LANES - n_exp)))
    router_hi = router_pad.astype(BF16)
    router_lo = (router_pad - router_hi.astype(F32)).astype(BF16)
    b_mod3 = b_mod.reshape(depth, 1, 6 * d)
    ln1_g3, ln1_b3 = ln1_g.reshape(depth, 1, d), ln1_b.reshape(depth, 1, d)
    ln2_g3, ln2_b3 = ln2_g.reshape(depth, 1, d), ln2_b.reshape(depth, 1, d)

    mods = [_matmul(cond, w_mod, l, 0, 6 * d, bias=b_mod3, act="silu", tn=1024).reshape(8, 6, 1, d)
            for l in range(depth)]
    hs = [_ln_mod(x, mods[0], st["mod_row"]) for x, st in zip(xs, streams)]

    for l in range(depth):
        p.update(win_sink=win_sink[l], q_norm=q_norm[l], k_norm=k_norm[l], s5=_s5_prep(lay, l))
        mod4 = mods[l]
        next_mod4 = mods[l + 1] if l + 1 < depth else None
        streams[1].update(
            ck_w=cache_win_k[:, l].reshape(bs, past, kvw), cv_w=cache_win_v[:, l].reshape(bs, past, kvw),
            ck_g=cache_glob_k[:, l].reshape(bs, past, kvg), cv_g=cache_glob_v[:, l].reshape(bs, past, kvg),
            s0=_state_to_lanes(state_ssm_re[:, l], state_ssm_im[:, l]))

        routed = []
        for si, st in enumerate(streams):
            x = xs[si]
            r, extras = _mixers(x, hs[si], mod4, p, st, l)
            x1, h2, logits = _post_ln(r, mod4, st["mod_row"], ln1_g3, ln1_b3, router_hi, router_lo, l, st["tm"])
            xsel, vals, rank = _route(logits, h2, batch=st["batch"], seq=st["seq"], n_exp=n_exp)
            routed.append((x1, xsel, vals, rank))
            if not st["latent"]:
                k_w, v_w, k_g, v_g, fin = extras
                new["wk"].append(k_w.reshape(bp, lp, -1))
                new["wv"].append(v_w.reshape(bp, lp, -1))
                new["gk"].append(k_g.reshape(bp, lp, -1))
                new["gv"].append(v_g.reshape(bp, lp, -1))
                s_re, s_im = _lanes_to_state(fin, p_n)
                new["sre"].append(s_re)
                new["sim"].append(s_im)
        outs = _ffn(routed[0][1], routed[1][1], routed[0][2], routed[1][2],
                    exp_w_gate, exp_w_up, exp_w_down, l)
        for si, st in enumerate(streams):
            xs[si], hs[si] = _scatter(outs[si], routed[si][3], routed[si][0], mod4, st["mod_row"],
                                      ln2_g3, ln2_b3, l, alpha, next_mod4, batch=st["batch"], seq=st["seq"])

    def cache(layers):
        return jnp.stack(layers, axis=1).reshape(bp, depth, lp, -1, HEAD_DIM)

    return (xs[0].reshape(bp, lp, d), xs[1].reshape(bs, ls, d),
            cache(new["wk"]), cache(new["wv"]), cache(new["gk"]), cache(new["gv"]),
            jnp.stack(new["sre"], axis=1), jnp.stack(new["sim"], axis=1))
```

```python
import functools

import numpy as np
import jax
import jax.numpy as jnp
from jax import lax
from jax.experimental import pallas as pl
from jax.experimental.pallas import tpu as pltpu

F32 = jnp.float32
BF16 = jnp.bfloat16
HIGHEST = lax.Precision.HIGHEST

HEAD_DIM = 128
LANES = 128
GRID_W = 64
WINDOW = 128
EC_FACTOR = 2
ROPE_THETA = 10000.0
EPS = 1e-6
NEG_INF = -1e30
ATTN_SCALE = HEAD_DIM ** -0.5
LOG2E = 1.4426950408889634
S5_CHUNK = 16
S5_OCT = 8
ROW_BLOCK = 256
GATHER_ROWS = 512
ROUTE_SEQS_PER_STEP = 1
VMEM_LIMIT = 60 * 1024 * 1024
RESIDENT = pl.Buffered(1)


def _cparams(n_axes):
    return pltpu.CompilerParams(dimension_semantics=("arbitrary",) * n_axes,
                                vmem_limit_bytes=VMEM_LIMIT)


def _tile(dim, pref):
    return pref if dim % pref == 0 else dim


def _ln(x):
    mu = jnp.mean(x, axis=-1, keepdims=True)
    xc = x - mu
    var = jnp.mean(xc * xc, axis=-1, keepdims=True)
    return xc * lax.rsqrt(var + EPS)


def _split_bf16(x):
    hi = x.astype(BF16)
    return hi, (x - hi.astype(F32)).astype(BF16)


def _dot3(a_hi, a_lo, b_hi, b_lo, dims=(((1,), (0,)), ((), ()))):
    def d(x, y):
        return lax.dot_general(x, y, dims, preferred_element_type=F32)
    return d(a_hi, b_hi) + d(a_lo, b_hi) + d(a_hi, b_lo)


def _dot3_nt(a, b):
    return _dot3(*_split_bf16(a), *_split_bf16(b), dims=(((1,), (1,)), ((), ())))


def _mm_kernel(x_ref, w_ref, *rest, act, has_bias):
    if has_bias:
        b_ref, o_ref, wbf_ref = rest
    else:
        o_ref, wbf_ref = rest

    @pl.when(pl.program_id(1) == 0)
    def _():
        wbf_ref[...] = w_ref[...].astype(BF16)

    x = x_ref[...]
    if act == "silu":
        x = jax.nn.silu(x.astype(F32))
    acc = jnp.dot(x.astype(BF16), wbf_ref[...], preferred_element_type=F32)
    if has_bias:
        acc = acc + b_ref[...]
    o_ref[...] = acc.astype(o_ref.dtype)


def _matmul(x, w, layer, col0, ncols, *, bias=None, act=None, out_dtype=F32, tm=1024, tn=512):
    m, k = x.shape
    tm = _tile(m, tm)
    tn = next(t for t in (tn, 512, 256, LANES) if ncols % t == 0 and col0 % t == 0)
    c0 = col0 // tn
    in_specs = [pl.BlockSpec((tm, k), lambda n, i: (i, 0)),
                pl.BlockSpec((None, k, tn), lambda n, i: (layer, 0, c0 + n))]
    args = [x, w]
    if bias is not None:
        in_specs.append(pl.BlockSpec((None, 1, tn), lambda n, i: (layer, 0, c0 + n)))
        args.append(bias)
    return pl.pallas_call(
        functools.partial(_mm_kernel, act=act, has_bias=bias is not None),
        grid=(ncols // tn, m // tm),
        in_specs=in_specs,
        out_specs=pl.BlockSpec((tm, tn), lambda n, i: (i, n)),
        out_shape=jax.ShapeDtypeStruct((m, ncols), out_dtype),
        scratch_shapes=[pltpu.VMEM((k, tn), BF16)],
        compiler_params=_cparams(2),
    )(*args)


def _ln_mod_kernel(x_ref, sc_ref, sh_ref, h_ref):
    h_ref[...] = (_ln(x_ref[...]) * (1.0 + sc_ref[...]) + sh_ref[...]).astype(h_ref.dtype)


def _mod_spec(d, chunk, mod_row, rows_per_step=ROW_BLOCK):
    scale = rows_per_step // ROW_BLOCK
    return pl.BlockSpec((None, None, 1, d), lambda i: (mod_row(i * scale), chunk, 0, 0))


def _ln_mod(x, mod4, mod_row):
    t, d = x.shape
    return pl.pallas_call(
        _ln_mod_kernel,
        grid=(t // ROW_BLOCK,),
        in_specs=[pl.BlockSpec((ROW_BLOCK, d), lambda i: (i, 0)),
                  _mod_spec(d, 1, mod_row), _mod_spec(d, 0, mod_row)],
        out_specs=pl.BlockSpec((ROW_BLOCK, d), lambda i: (i, 0)),
        out_shape=jax.ShapeDtypeStruct((t, d), BF16),
        compiler_params=_cparams(1),
    )(x, mod4, mod4)


def _rope_tables(seq_len):
    half = HEAD_DIM // 4
    inv = ROPE_THETA ** (-np.arange(half, dtype=np.float64) / half)
    tok = np.arange(seq_len)
    ang_r = (tok // GRID_W)[:, None] * inv[None, :]
    ang_c = (tok % GRID_W)[:, None] * inv[None, :]
    cos = np.concatenate([np.cos(ang_r), np.cos(ang_r), np.cos(ang_c), np.cos(ang_c)], axis=-1)
    sin = np.concatenate([-np.sin(ang_r), np.sin(ang_r), -np.sin(ang_c), np.sin(ang_c)], axis=-1)
    return jnp.asarray(cos, F32), jnp.asarray(sin, F32)


def _rope(x, cos, sin_signed):
    lane = lax.broadcasted_iota(jnp.int32, x.shape, 1)
    swapped = jnp.where((lane % 64) < 32, pltpu.roll(x, 96, 1), pltpu.roll(x, 32, 1))
    return x * cos + swapped * sin_signed


def _rms(x, w):
    return x * lax.rsqrt(jnp.mean(x * x, axis=-1, keepdims=True) + EPS) * w


def _attn_kernel(*refs, heads, kv, tq, rope, norm, sink, ctx, band, emit_k):
    it = iter(refs)
    q_refs, k_ref, v_ref = (next(it), next(it)), next(it), next(it)
    kc_ref = vc_ref = sink_ref = qn_ref = kn_ref = cq_ref = sq_ref = ck_ref = sk_ref = None
    if ctx:
        kc_ref, vc_ref = next(it), next(it)
    if sink:
        sink_ref = next(it)
    if norm:
        qn_ref, kn_ref = next(it), next(it)
    if rope:
        cq_ref, sq_ref, ck_ref, sk_ref = next(it), next(it), next(it), next(it)
    o_ref = next(it)
    ko_ref = next(it) if emit_k else None
    kp_ref, vp_ref = next(it), next(it)

    qi = pl.program_id(1)
    rep = heads // kv

    @pl.when(qi == 0)
    def _():
        for g in range(kv):
            sl = slice(g * HEAD_DIM, (g + 1) * HEAD_DIM)
            kg = k_ref[:, sl]
            if norm:
                kg = _rms(kg, kn_ref[...])
            if rope:
                kg = _rope(kg, ck_ref[...], sk_ref[...])
            if emit_k:
                ko_ref[:, sl] = kg
            kp_ref[:, sl] = kg.astype(BF16)
        vp_ref[...] = v_ref[...].astype(BF16)

    nt = (((1,), (1,)), ((), ()))
    seq = kp_ref.shape[0]
    if band:
        span = min(seq, tq + 2 * max(WINDOW, tq))
        k0 = pl.multiple_of(jnp.clip(qi * tq - max(WINDOW, tq), 0, seq - span), tq)
        rows = pl.ds(k0, span)
        qpos = qi * tq + lax.broadcasted_iota(jnp.int32, (tq, span), 0)
        kpos = k0 + lax.broadcasted_iota(jnp.int32, (tq, span), 1)
        in_band = jnp.abs(qpos - kpos) <= WINDOW
    else:
        rows = slice(None)
    for h in range(heads):
        g = h // rep
        gs = slice(g * HEAD_DIM, (g + 1) * HEAD_DIM)
        hh = h % (heads // 2)
        qh = q_refs[h // (heads // 2)][:, hh * HEAD_DIM:(hh + 1) * HEAD_DIM]
        if norm:
            qh = _rms(qh, qn_ref[...])
        if rope:
            qh = _rope(qh, cq_ref[...], sq_ref[...])
        qh = (qh * (ATTN_SCALE * LOG2E)).astype(BF16)
        s = lax.dot_general(qh, kp_ref[rows, gs], nt, preferred_element_type=F32)
        if band:
            s = jnp.where(in_band, s, NEG_INF)
        m = jnp.max(s, axis=-1, keepdims=True)
        if ctx:
            sc = lax.dot_general(qh, kc_ref[:, gs].astype(BF16), nt, preferred_element_type=F32)
            m = jnp.maximum(m, jnp.max(sc, axis=-1, keepdims=True))
        if sink:
            m = jnp.maximum(m, sink_ref[h] * LOG2E)
        e = jnp.exp2(s - m)
        den = jnp.sum(e, axis=-1, keepdims=True)
        o = jnp.dot(e.astype(BF16), vp_ref[rows, gs], preferred_element_type=F32)
        if ctx:
            ec = jnp.exp2(sc - m)
            den = den + jnp.sum(ec, axis=-1, keepdims=True)
            o = o + jnp.dot(ec.astype(BF16), vc_ref[:, gs].astype(BF16), preferred_element_type=F32)
        if sink:
            den = den + jnp.exp2(sink_ref[h] * LOG2E - m)
        o_ref[:, h * HEAD_DIM:(h + 1) * HEAD_DIM] = (o / den).astype(o_ref.dtype)


def _attention(qa, q0, ka, k0, va, v0, *, heads, kv, batch, seq, k_ctx=None, v_ctx=None, sink=None,
               q_norm=None, k_norm=None, rope=False, band=False, emit_k=False):
    t = qa.shape[0]
    qw, kw = heads * HEAD_DIM, kv * HEAD_DIM
    qhw = qw // 2
    assert heads % 2 == 0 and q0 % qhw == 0 and k0 % kw == 0 and v0 % kw == 0
    qc, kc, vc = q0 // qhw, k0 // kw, v0 // kw
    tq = _tile(seq, 256)
    nq = seq // tq
    ctx, has_sink, norm = k_ctx is not None, sink is not None, q_norm is not None
    in_specs = [pl.BlockSpec((tq, qhw), lambda b, i: (b * nq + i, qc)),
                pl.BlockSpec((tq, qhw), lambda b, i: (b * nq + i, qc + 1)),
                pl.BlockSpec((seq, kw), lambda b, i: (b, kc)),
                pl.BlockSpec((seq, kw), lambda b, i: (b, vc))]
    args = [qa, qa, ka, va]
    if ctx:
        lc = k_ctx.shape[1]
        in_specs += [pl.BlockSpec((None, lc, kw), lambda b, i: (b, 0, 0))] * 2
        args += [k_ctx, v_ctx]
    if has_sink:
        in_specs.append(pl.BlockSpec(memory_space=pltpu.SMEM))
        args.append(sink)
    if norm:
        in_specs += [pl.BlockSpec((1, HEAD_DIM), lambda b, i: (0, 0))] * 2
        args += [q_norm.reshape(1, HEAD_DIM), k_norm.reshape(1, HEAD_DIM)]
    if rope:
        cos, sin = _rope_tables(seq)
        in_specs += [pl.BlockSpec((tq, HEAD_DIM), lambda b, i: (i, 0))] * 2
        in_specs += [pl.BlockSpec((seq, HEAD_DIM), lambda b, i: (0, 0))] * 2
        args += [cos, sin, cos, sin]
    out_specs = [pl.BlockSpec((tq, qw), lambda b, i: (b * nq + i, 0))]
    out_shape = [jax.ShapeDtypeStruct((t, qw), BF16)]
    if emit_k:
        out_specs.append(pl.BlockSpec((seq, kw), lambda b, i: (b, 0)))
        out_shape.append(jax.ShapeDtypeStruct((t, kw), F32))
    res = pl.pallas_call(
        functools.partial(_attn_kernel, heads=heads, kv=kv, tq=tq, rope=rope, norm=norm,
                          sink=has_sink, ctx=ctx, band=band, emit_k=emit_k),
        grid=(batch, nq),
        in_specs=in_specs,
        out_specs=out_specs,
        out_shape=out_shape,
        scratch_shapes=[pltpu.VMEM((seq, kw), BF16), pltpu.VMEM((seq, kw), BF16)],
        compiler_params=_cparams(2),
    )(*args)
    return res if emit_k else res[0]


def _s5_layouts(log_dt, a_re, a_im, b_re, b_im, c_re, c_im):
    dep, _, g_n, p_n = a_re.shape
    gc = b_re.shape[-1]
    no = g_n // S5_OCT
    sw = S5_OCT * p_n
    ldt = jnp.broadcast_to(log_dt[..., None], a_re.shape)
    prm = jnp.stack([ldt, a_re, a_im], axis=2).astype(F32)
    prow = jnp.pad(prm.reshape(dep, 2, 3, no, sw).transpose(0, 1, 3, 2, 4),
                   ((0, 0),) * 3 + ((0, 5), (0, 0)))
    pcmp = jnp.broadcast_to(prm[:, :, :, :, None, :], (dep, 2, 3, g_n, gc, p_n))
    pcmp = pcmp.reshape(dep, 2, 3, no, S5_OCT * gc, p_n).transpose(0, 1, 3, 2, 4, 5)

    def b_views(b):
        bt = jnp.swapaxes(b.astype(F32), -1, -2).reshape(dep, 2, no, S5_OCT * gc, p_n)
        return bt, jnp.tile(bt, (1, 1, 1, 1, S5_OCT))

    def c_views(c):
        cc = c.astype(F32).reshape(dep, 2, no, S5_OCT * gc, p_n)
        ct = jnp.swapaxes(c.astype(F32), -1, -2).reshape(dep, 2, no, sw, gc)
        return cc, jnp.tile(ct, (1, 1, 1, 1, S5_OCT))

    bc_re, bt_re = b_views(b_re)
    bc_im, bt_im = b_views(b_im)
    cc_re, ct_re = c_views(c_re)
    cc_im, ct_im = c_views(c_im)
    return dict(prow=prow, pcmp=pcmp, bc_re=bc_re, bc_im=bc_im, bt_re=bt_re, bt_im=bt_im,
                cc_re=cc_re, cc_im=cc_im, ct_re=ct_re, ct_im=ct_im)


def _lam_powers(ldt, ar, ai, k):
    dt = jnp.exp(ldt)
    mag = jnp.exp(k * (dt * ar))
    return mag * jnp.cos(k * (dt * ai)), mag * jnp.sin(k * (dt * ai))


def _bbar_coef(ldt, ar, ai):
    lam_re, lam_im = _lam_powers(ldt, ar, ai, 1.0)
    den = ar * ar + ai * ai
    return ((lam_re - 1.0) * ar + lam_im * ai) / den, (lam_im * ar - (lam_re - 1.0) * ai) / den


def _same_group(shape, rows_per_group, cols_per_group):
    r = lax.broadcasted_iota(jnp.int32, shape, 0) // rows_per_group
    c = lax.broadcasted_iota(jnp.int32, shape, 1) // cols_per_group
    return r == c


def _s5_prep_w_kernel(prow_ref, btr_ref, bti_ref, w_ref, lam_ref):
    tc = S5_CHUNK
    rows, sw = btr_ref.shape[1], btr_ref.shape[2]
    mask = _same_group((rows, sw), rows // S5_OCT, sw // S5_OCT)
    k = lax.broadcasted_iota(jnp.int32, (24, sw), 0).astype(F32)
    for d in range(2):
        ldt, ar, ai = prow_ref[d, 0:1, :], prow_ref[d, 1:2, :], prow_ref[d, 2:3, :]
        pw_re, pw_im = _lam_powers(ldt, ar, ai, k)
        co_re, co_im = _bbar_coef(ldt, ar, ai)
        bb_re = jnp.where(mask, co_re * btr_ref[d] - co_im * bti_ref[d], 0.0)
        bb_im = jnp.where(mask, co_re * bti_ref[d] + co_im * btr_ref[d], 0.0)
        lam_ref[2 * d:2 * d + 1, :] = pw_re[tc:tc + 1]
        lam_ref[2 * d + 1:2 * d + 2, :] = pw_im[tc:tc + 1]
        for j in range(tc):
            kk = tc - 1 - j if d == 0 else j
            pr, pi = pw_re[kk:kk + 1], pw_im[kk:kk + 1]
            rs = slice(j * rows, (j + 1) * rows)
            w_ref[rs, (2 * d) * sw:(2 * d + 1) * sw] = (pr * bb_re - pi * bb_im).astype(w_ref.dtype)
            w_ref[rs, (2 * d + 1) * sw:(2 * d + 2) * sw] = (pr * bb_im + pi * bb_re).astype(w_ref.dtype)


def _row_to_col(row, eye):
    return jnp.sum(jnp.where(eye, row, 0.0), axis=1, keepdims=True)


def _s5_prep_z_kernel(prow_ref, ctr_ref, cti_ref, z_ref):
    tc = S5_CHUNK
    sw, cols = ctr_ref.shape[1], ctr_ref.shape[2]
    mask = _same_group((sw, cols), sw // S5_OCT, cols // S5_OCT)
    eye = _same_group((sw, sw), 1, 1)
    for d in range(2):
        lam_re, lam_im = _lam_powers(prow_ref[d, 0:1, :], prow_ref[d, 1:2, :], prow_ref[d, 2:3, :], 1.0)
        lr = jnp.broadcast_to(_row_to_col(lam_re, eye), (sw, cols))
        li = jnp.broadcast_to(_row_to_col(lam_im, eye), (sw, cols))
        c_re = jnp.where(mask, ctr_ref[d], 0.0)
        c_im = jnp.where(mask, cti_ref[d], 0.0)
        z_re, z_im = c_re * lr - c_im * li, c_re * li + c_im * lr
        for step in range(tc):
            t = step if d == 0 else tc - 1 - step
            cs = slice(t * cols, (t + 1) * cols)
            z_ref[(2 * d) * sw:(2 * d + 1) * sw, cs] = z_re.astype(z_ref.dtype)
            z_ref[(2 * d + 1) * sw:(2 * d + 2) * sw, cs] = (-z_im).astype(z_ref.dtype)
            z_re, z_im = z_re * lr - z_im * li, z_re * li + z_im * lr


def _s5_prep_m_kernel(pcmp_ref, bcr_ref, bci_ref, ccr_ref, cci_ref, m_ref, xr_ref, xi_ref, taps_ref):
    tc = S5_CHUNK
    rows = bcr_ref.shape[1]
    mask = _same_group((rows, rows), rows // S5_OCT, rows // S5_OCT)
    for d in range(2):
        ldt, ar, ai = pcmp_ref[d, 0], pcmp_ref[d, 1], pcmp_ref[d, 2]
        lam_re, lam_im = _lam_powers(ldt, ar, ai, 1.0)
        co_re, co_im = _bbar_coef(ldt, ar, ai)
        x_re = co_re * bcr_ref[d] - co_im * bci_ref[d]
        x_im = co_re * bci_ref[d] + co_im * bcr_ref[d]
        for kk in range(tc):
            xr_ref[kk * rows:(kk + 1) * rows, :] = x_re
            xi_ref[kk * rows:(kk + 1) * rows, :] = x_im
            x_re, x_im = x_re * lam_re - x_im * lam_im, x_re * lam_im + x_im * lam_re
        taps = _dot3_nt(xr_ref[...], ccr_ref[d]) - _dot3_nt(xi_ref[...], cci_ref[d])
        for kk in range(tc):
            tile = jnp.where(mask, taps[kk * rows:(kk + 1) * rows, :], 0.0)
            col = (tc - 1 + kk) if d == 0 else (tc - 1 - kk)
            cs = slice(col * rows, (col + 1) * rows)
            if d == 1 and kk == 0:
                taps_ref[:, cs] += tile
            else:
                taps_ref[:, cs] = tile
    for j in range(tc):
        m_ref[j * rows:(j + 1) * rows, :] = taps_ref[:, (tc - 1 - j) * rows:(2 * tc - 1 - j) * rows].astype(m_ref.dtype)


def _s5_prep(lay, layer):
    _, _, no, rows, sw = lay["bt_re"].shape
    p_n = lay["bc_re"].shape[-1]
    tc = S5_CHUNK
    kw = tc * rows

    def per_octet(*tail):
        nd = len(tail)
        return pl.BlockSpec((None, 2, None) + tail, lambda o: (layer, 0, o) + (0,) * nd)

    def out(r, c):
        return pl.BlockSpec((None, r, c), lambda o: (o, 0, 0))

    w_o, lam_o = pl.pallas_call(
        _s5_prep_w_kernel, grid=(no,),
        in_specs=[per_octet(8, sw), per_octet(rows, sw), per_octet(rows, sw)],
        out_specs=[out(kw, 4 * sw), out(4, sw)],
        out_shape=[jax.ShapeDtypeStruct((no, kw, 4 * sw), BF16), jax.ShapeDtypeStruct((no, 4, sw), F32)],
        compiler_params=_cparams(1),
    )(lay["prow"], lay["bt_re"], lay["bt_im"])
    z_o = pl.pallas_call(
        _s5_prep_z_kernel, grid=(no,),
        in_specs=[per_octet(8, sw), per_octet(sw, rows), per_octet(sw, rows)],
        out_specs=out(4 * sw, kw),
        out_shape=jax.ShapeDtypeStruct((no, 4 * sw, kw), BF16),
        compiler_params=_cparams(1),
    )(lay["prow"], lay["ct_re"], lay["ct_im"])
    m_o = pl.pallas_call(
        _s5_prep_m_kernel, grid=(no,),
        in_specs=[per_octet(3, rows, p_n)] + [per_octet(rows, p_n)] * 4,
        out_specs=out(kw, kw),
        out_shape=jax.ShapeDtypeStruct((no, kw, kw), BF16),
        scratch_shapes=[pltpu.VMEM((kw, p_n), F32), pltpu.VMEM((kw, p_n), F32),
                        pltpu.VMEM((rows, (2 * tc - 1) * rows), F32)],
        compiler_params=_cparams(1),
    )(lay["pcmp"], lay["bc_re"], lay["bc_im"], lay["cc_re"], lay["cc_im"])
    return m_o, w_o, z_o, lam_o


def _s5_state_kernel(u_ref, w_ref, lam_ref, s0_ref, sp_ref, fin_ref, ubf_ref, vs_ref, *, batch, nchunk):
    tc = S5_CHUNK
    r = batch * nchunk
    for t in range(tc):
        ubf_ref[:, t * LANES:(t + 1) * LANES] = u_ref[pl.ds(t, r, stride=tc), :].astype(BF16)
    q = vs_ref.shape[-1] // 4
    vs_ref[...] = jnp.dot(ubf_ref[...], w_ref[...],
                          preferred_element_type=F32).reshape(batch, nchunk, 4 * q)

    def scan(part, order):
        a, b = lam_ref[part:part + 1, :], lam_ref[part + 1:part + 2, :]
        re, im = slice(part * q, (part + 1) * q), slice((part + 1) * q, (part + 2) * q)
        s_re, s_im = s0_ref[:, re], s0_ref[:, im]
        for c in order:
            v_re, v_im = vs_ref[:, c, re], vs_ref[:, c, im]
            vs_ref[:, c, re] = s_re
            vs_ref[:, c, im] = s_im
            s_re, s_im = a * s_re - b * s_im + v_re, a * s_im + b * s_re + v_im
        fin_ref[:, re] = s_re
        fin_ref[:, im] = s_im

    scan(0, range(nchunk))
    scan(2, range(nchunk - 1, -1, -1))
    sp_ref[...] = vs_ref[...].reshape(r, 4 * q).astype(sp_ref.dtype)


def _s5_out_kernel(ubf_ref, sp_ref, m_ref, z_ref, y_ref):
    tc = S5_CHUNK
    tr = ubf_ref.shape[0]
    y = (jnp.dot(ubf_ref[...], m_ref[...], preferred_element_type=F32)
         + jnp.dot(sp_ref[...], z_ref[...], preferred_element_type=F32))
    for t in range(tc):
        y_ref[pl.ds(t, tr, stride=tc), :] = y[:, t * LANES:(t + 1) * LANES]


def _s5_mix(ua, u0, width, m_o, w_o, z_o, lam_o, s0, *, batch, seq):
    t = ua.shape[0]
    tc = S5_CHUNK
    no = width // LANES
    assert u0 % LANES == 0
    uc = u0 // LANES
    r = t // tc
    nchunk = seq // tc
    sw = w_o.shape[-1]
    kw = tc * LANES
    sp, fin, ubf = pl.pallas_call(
        functools.partial(_s5_state_kernel, batch=batch, nchunk=nchunk),
        grid=(no,),
        in_specs=[pl.BlockSpec((t, LANES), lambda o: (0, uc + o)),
                  pl.BlockSpec((None, kw, sw), lambda o: (o, 0, 0)),
                  pl.BlockSpec((None, 4, sw // 4), lambda o: (o, 0, 0)),
                  pl.BlockSpec((None, batch, sw), lambda o: (o, 0, 0))],
        out_specs=[pl.BlockSpec((None, r, sw), lambda o: (o, 0, 0)),
                   pl.BlockSpec((None, batch, sw), lambda o: (o, 0, 0)),
                   pl.BlockSpec((None, r, kw), lambda o: (o, 0, 0))],
        out_shape=[jax.ShapeDtypeStruct((no, r, sw), BF16),
                   jax.ShapeDtypeStruct((no, batch, sw), F32),
                   jax.ShapeDtypeStruct((no, r, kw), BF16)],
        scratch_shapes=[pltpu.VMEM((batch, nchunk, sw), F32)],
        compiler_params=_cparams(1),
    )(ua, w_o, lam_o, s0)
    tr = _tile(r, 512)
    y = pl.pallas_call(
        _s5_out_kernel,
        grid=(no, r // tr),
        in_specs=[pl.BlockSpec((None, tr, kw), lambda o, i: (o, i, 0)),
                  pl.BlockSpec((None, tr, sw), lambda o, i: (o, i, 0)),
                  pl.BlockSpec((None, kw, kw), lambda o, i: (o, 0, 0)),
                  pl.BlockSpec((None, sw, kw), lambda o, i: (o, 0, 0))],
        out_specs=pl.BlockSpec((tr * tc, LANES), lambda o, i: (i, o)),
        out_shape=jax.ShapeDtypeStruct((t, width), F32),
        compiler_params=_cparams(2),
    )(ubf, sp, m_o, z_o)
    return y, fin


def _state_to_lanes(s_re, s_im):
    b, _, g_n, p_n = s_re.shape
    no = g_n // S5_OCT
    parts = jnp.stack([s_re[:, 0], s_im[:, 0], s_re[:, 1], s_im[:, 1]], axis=1)
    parts = parts.reshape(b, 4, no, S5_OCT * p_n)
    return jnp.transpose(parts, (2, 0, 1, 3)).reshape(no, b, 4 * S5_OCT * p_n).astype(F32)


def _lanes_to_state(fin, p_n):
    no, b, _ = fin.shape
    parts = jnp.transpose(fin.reshape(no, b, 4, S5_OCT, p_n), (1, 2, 0, 3, 4)).reshape(b, 4, no * S5_OCT, p_n)
    return jnp.stack([parts[:, 0], parts[:, 2]], axis=1), jnp.stack([parts[:, 1], parts[:, 3]], axis=1)


def _glu_kernel(y_ref, ua_ref, ub_ref, d_ref, w_ref, o_ref):
    u = jnp.concatenate([ua_ref[...], ub_ref[...]], axis=1)
    z = jax.nn.gelu(y_ref[...] + d_ref[...] * u).astype(BF16)
    g = jnp.dot(z, w_ref[...], preferred_element_type=F32)
    w = o_ref.shape[-1]
    o_ref[...] = (g[:, :w] * jax.nn.sigmoid(g[:, w:])).astype(o_ref.dtype)


def _glu(y, ua, u0, d, w_glu, layer):
    t, width = y.shape
    half = width // 2
    assert u0 % half == 0
    uc = u0 // half
    tm = _tile(t, 512)
    return pl.pallas_call(
        _glu_kernel,
        grid=(t // tm,),
        in_specs=[pl.BlockSpec((tm, width), lambda i: (i, 0)),
                  pl.BlockSpec((tm, half), lambda i: (i, uc)),
                  pl.BlockSpec((tm, half), lambda i: (i, uc + 1)),
                  pl.BlockSpec((None, 1, width), lambda i: (layer, 0, 0)),
                  pl.BlockSpec((None, width, 2 * width), lambda i: (layer, 0, 0), pipeline_mode=RESIDENT)],
        out_specs=pl.BlockSpec((tm, width), lambda i: (i, 0)),
        out_shape=jax.ShapeDtypeStruct((t, width), BF16),
        compiler_params=_cparams(1),
    )(y, ua, ua, d, w_glu)


def _merge_kernel(ow_ref, os_ref, og_ref, gw_ref, gs_ref, gg_ref, x_ref, g1_ref,
                  ww_ref, ws_ref, wg_ref, wo_ref, r_ref, *, alpha):
    def gate(g_ref):
        return jax.nn.sigmoid(g_ref[...].astype(F32))

    y = (gate(gw_ref) * jnp.dot(ow_ref[...], ww_ref[...], preferred_element_type=F32)
         + gate(gs_ref) * jnp.dot(os_ref[...], ws_ref[...], preferred_element_type=F32)
         + gate(gg_ref) * jnp.dot(og_ref[...], wg_ref[...], preferred_element_type=F32))
    o = jnp.dot(y.astype(BF16), wo_ref[...], preferred_element_type=F32)
    r_ref[...] = alpha * x_ref[...] + g1_ref[...] * o


def _merge(o_win, o_ssm, o_glob, gates, x, mod4, mod_row, w_win, w_ssm, w_glob, w_out, layer, alpha, tm):
    t, d = x.shape

    def rows(w):
        return pl.BlockSpec((tm, w), lambda i: (i, 0))

    def whole(a):
        return pl.BlockSpec((None,) + a.shape[1:], lambda i: (layer, 0, 0), pipeline_mode=RESIDENT)

    return pl.pallas_call(
        functools.partial(_merge_kernel, alpha=alpha),
        grid=(t // tm,),
        in_specs=[rows(o_win.shape[1]), rows(o_ssm.shape[1]), rows(o_glob.shape[1]),
                  pl.BlockSpec((tm, d), lambda i: (i, 0)),
                  pl.BlockSpec((tm, d), lambda i: (i, 1)),
                  pl.BlockSpec((tm, d), lambda i: (i, 2)),
                  rows(d), _mod_spec(d, 2, mod_row, tm),
                  whole(w_win), whole(w_ssm), whole(w_glob), whole(w_out)],
        out_specs=rows(d),
        out_shape=jax.ShapeDtypeStruct((t, d), F32),
        compiler_params=_cparams(1),
    )(o_win, o_ssm, o_glob, gates, gates, gates, x, mod4, w_win, w_ssm, w_glob, w_out)


def _post_ln_kernel(r_ref, sc2_ref, sh2_ref, lg_ref, lb_ref, rwh_ref, rwl_ref, x1_ref, h2_ref, logit_ref):
    x1 = _ln(r_ref[...]) * lg_ref[...] + lb_ref[...]
    x1_ref[...] = x1
    h2 = _ln(x1) * (1.0 + sc2_ref[...]) + sh2_ref[...]
    hi, lo = _split_bf16(h2)
    h2_ref[...] = hi
    logit_ref[...] = _dot3(hi, lo, rwh_ref[...], rwl_ref[...])


def _post_ln(r, mod4, mod_row, ln_g, ln_b, router_hi, router_lo, layer, tm):
    t, d = r.shape

    def rows(w):
        return pl.BlockSpec((tm, w), lambda i: (i, 0))

    def vec():
        return pl.BlockSpec((None, 1, d), lambda i: (layer, 0, 0))

    def router():
        return pl.BlockSpec((None, d, LANES), lambda i: (layer, 0, 0), pipeline_mode=RESIDENT)

    return pl.pallas_call(
        _post_ln_kernel,
        grid=(t // tm,),
        in_specs=[rows(d), _mod_spec(d, 4, mod_row, tm), _mod_spec(d, 3, mod_row, tm), vec(), vec(),
                  router(), router()],
        out_specs=[rows(d), rows(d), rows(LANES)],
        out_shape=[jax.ShapeDtypeStruct((t, d), F32), jax.ShapeDtypeStruct((t, d), BF16),
                   jax.ShapeDtypeStruct((t, LANES), F32)],
        compiler_params=_cparams(1),
    )(r, mod4, mod4, ln_g, ln_b, router_hi, router_lo)


def _route_kernel(logit_ref, h_ref, xsel_ref, vals_ref, rank_ref, *, n_exp, cap):
    nsub, _, n = rank_ref.shape
    jj = lax.broadcasted_iota(jnp.int32, (n, n), 0)
    ii = lax.broadcasted_iota(jnp.int32, (n, n), 1)
    slot = lax.broadcasted_iota(jnp.int32, (cap, n), 0)
    for s in range(nsub):
        tok = slice(s * n, (s + 1) * n)
        lg = logit_ref[tok, :]
        col = lax.broadcasted_iota(jnp.int32, lg.shape, 1)
        lg = jnp.where(col < n_exp, lg, -jnp.inf)
        ex = jnp.exp(lg - jnp.max(lg, axis=-1, keepdims=True))
        aff = ex / jnp.sum(ex, axis=-1, keepdims=True)
        aff_t = aff.T
        picks = []
        for e in range(n_exp):
            a_row = aff_t[e:e + 1, :]
            a_col = aff[:, e:e + 1]
            beats = (a_col > a_row) | ((a_col == a_row) & (jj < ii))
            rank = jnp.sum(beats.astype(jnp.int32), axis=0, keepdims=True)
            pick = slot == rank
            picks.append(pick.astype(BF16))
            vals_ref[e, s * cap:(s + 1) * cap, :] = jnp.sum(jnp.where(pick, a_row, 0.0), axis=1, keepdims=True)
            rank_ref[s, e:e + 1, :] = rank
        group = min(n_exp, max(1, GATHER_ROWS // cap))
        for e0 in range(0, n_exp, group):
            sel = jnp.dot(jnp.concatenate(picks[e0:e0 + group], axis=0), h_ref[tok, :],
                          preferred_element_type=F32)
            for e in range(e0, min(e0 + group, n_exp)):
                xsel_ref[e, s * cap:(s + 1) * cap, :] = sel[(e - e0) * cap:(e - e0 + 1) * cap, :].astype(
                    xsel_ref.dtype)


def _route(logits, h2, *, batch, seq, n_exp):
    t, d = h2.shape
    cap = EC_FACTOR * seq // n_exp
    nsub = ROUTE_SEQS_PER_STEP if seq <= ROW_BLOCK and batch % ROUTE_SEQS_PER_STEP == 0 else 1
    return pl.pallas_call(
        functools.partial(_route_kernel, n_exp=n_exp, cap=cap),
        grid=(batch // nsub,),
        in_specs=[pl.BlockSpec((nsub * seq, LANES), lambda b: (b, 0)),
                  pl.BlockSpec((nsub * seq, d), lambda b: (b, 0))],
        out_specs=[pl.BlockSpec((n_exp, nsub * cap, d), lambda b: (0, b, 0)),
                   pl.BlockSpec((n_exp, nsub * cap, 1), lambda b: (0, b, 0)),
                   pl.BlockSpec((nsub, n_exp, seq), lambda b: (b, 0, 0))],
        out_shape=[jax.ShapeDtypeStruct((n_exp, batch * cap, d), BF16),
                   jax.ShapeDtypeStruct((n_exp, batch * cap, 1), F32),
                   jax.ShapeDtypeStruct((batch, n_exp, seq), jnp.int32)],
        compiler_params=_cparams(1),
    )(logits, h2)


def _ffn_hidden_kernel(xp_ref, xs_ref, wg_ref, wu_ref, hp_ref, hs_ref):
    wg = wg_ref[...].astype(BF16)
    wu = wu_ref[...].astype(BF16)
    for x_ref, h_ref in ((xp_ref, hp_ref), (xs_ref, hs_ref)):
        x = x_ref[...]
        h_ref[...] = (jax.nn.silu(jnp.dot(x, wg, preferred_element_type=F32))
                      * jnp.dot(x, wu, preferred_element_type=F32)).astype(h_ref.dtype)


def _ffn_down_kernel(hp_ref, hs_ref, vp_ref, vs_ref, wd_ref, op_ref, os_ref):
    wd = wd_ref[...].astype(BF16)
    for h_ref, v_ref, o_ref in ((hp_ref, vp_ref, op_ref), (hs_ref, vs_ref, os_ref)):
        o_ref[...] = (jnp.dot(h_ref[...], wd, preferred_element_type=F32) * v_ref[...]).astype(o_ref.dtype)


def _ffn(xsel_p, xsel_s, vals_p, vals_s, w_gate, w_up, w_down, layer):
    n_exp, rp, d = xsel_p.shape
    rs = xsel_s.shape[1]
    ff = w_gate.shape[-1]
    tf = _tile(ff, 512)
    tn = _tile(d, 1024)

    def per_expert(r, w):
        return pl.BlockSpec((None, r, w), lambda e, j: (e, 0, 0))

    def col_tile(r, w):
        return pl.BlockSpec((None, r, w), lambda e, j: (e, 0, j))

    def w_tile(k, w):
        return pl.BlockSpec((None, None, k, w), lambda e, j: (layer, e, 0, j))

    hid_p, hid_s = pl.pallas_call(
        _ffn_hidden_kernel,
        grid=(n_exp, ff // tf),
        in_specs=[per_expert(rp, d), per_expert(rs, d), w_tile(d, tf), w_tile(d, tf)],
        out_specs=[col_tile(rp, tf), col_tile(rs, tf)],
        out_shape=[jax.ShapeDtypeStruct((n_exp, rp, ff), BF16), jax.ShapeDtypeStruct((n_exp, rs, ff), BF16)],
        compiler_params=_cparams(2),
    )(xsel_p, xsel_s, w_gate, w_up)
    return pl.pallas_call(
        _ffn_down_kernel,
        grid=(n_exp, d // tn),
        in_specs=[per_expert(rp, ff), per_expert(rs, ff), per_expert(rp, 1), per_expert(rs, 1),
                  w_tile(ff, tn)],
        out_specs=[col_tile(rp, tn), col_tile(rs, tn)],
        out_shape=[jax.ShapeDtypeStruct((n_exp, rp, d), BF16), jax.ShapeDtypeStruct((n_exp, rs, d), BF16)],
        compiler_params=_cparams(2),
    )(hid_p, hid_s, vals_p, vals_s, w_down)


def _scatter_kernel(out_ref, rank_ref, x_ref, g2_ref, lg_ref, lb_ref, *rest, alpha, emit_h):
    n_exp, cap, d = out_ref.shape
    n = x_ref.shape[0]
    slot = lax.broadcasted_iota(jnp.int32, (cap, n), 0)
    pick = jnp.concatenate([(slot == rank_ref[e:e + 1, :]).astype(F32) for e in range(n_exp)], axis=0)
    f = jnp.dot(pick.T.astype(BF16), out_ref[...].reshape(n_exp * cap, d), preferred_element_type=F32)
    x2 = _ln(alpha * x_ref[...] + g2_ref[...] * f) * lg_ref[...] + lb_ref[...]
    if emit_h:
        sc_ref, sh_ref, x2_ref, h_ref = rest
        h_ref[...] = (_ln(x2) * (1.0 + sc_ref[...]) + sh_ref[...]).astype(h_ref.dtype)
    else:
        x2_ref, = rest
    x2_ref[...] = x2


def _scatter(out, rank, x1, mod4, mod_row, ln_g, ln_b, layer, alpha, next_mod4, *, batch, seq):
    n_exp, _, d = out.shape
    cap = out.shape[1] // batch
    blocks_per_seq = seq // ROW_BLOCK
    emit_h = next_mod4 is not None

    ts = _tile(seq, 512)
    nts = seq // ts

    def mod(chunk):
        return pl.BlockSpec((None, None, 1, d), lambda b, j: (mod_row(b * blocks_per_seq), chunk, 0, 0))

    rows = pl.BlockSpec((ts, d), lambda b, j: (b * nts + j, 0))
    in_specs = [pl.BlockSpec((n_exp, cap, d), lambda b, j: (0, b, 0)),
                pl.BlockSpec((None, n_exp, ts), lambda b, j: (b, 0, j)),
                rows, mod(5),
                pl.BlockSpec((None, 1, d), lambda b, j: (layer, 0, 0)),
                pl.BlockSpec((None, 1, d), lambda b, j: (layer, 0, 0))]
    args = [out, rank, x1, mod4, ln_g, ln_b]
    out_specs = [rows]
    out_shape = [jax.ShapeDtypeStruct((batch * seq, d), F32)]
    if emit_h:
        in_specs += [mod(1), mod(0)]
        args += [next_mod4, next_mod4]
        out_specs.append(rows)
        out_shape.append(jax.ShapeDtypeStruct((batch * seq, d), BF16))
    res = pl.pallas_call(
        functools.partial(_scatter_kernel, alpha=alpha, emit_h=emit_h),
        grid=(batch, nts),
        in_specs=in_specs,
        out_specs=out_specs,
        out_shape=out_shape,
        compiler_params=_cparams(2),
    )(*args)
    return (res[0], res[1]) if emit_h else (res[0], None)


def _mixers(x, h, mod4, p, st, layer):
    batch, seq = st["batch"], st["seq"]
    dims = p["dims"]
    hw, kvw, hg, kvg, ssm_w = dims["heads_win"], dims["kv_win"], dims["heads_glob"], dims["kv_glob"], dims["ssm_w"]
    wa = (hw + 2 * kvw) * HEAD_DIM
    wb = (hg + 2 * kvg) * HEAD_DIM
    pa = _matmul(h, p["w_in"], layer, 0, wa + ssm_w + wb, tn=1024)
    gates = _matmul(h, p["w_in"], layer, wa + ssm_w + wb, p["w_in"].shape[-1] - wa - ssm_w - wb,
                    out_dtype=BF16, tn=1024)
    k_w0, v_w0 = hw * HEAD_DIM, (hw + kvw) * HEAD_DIM
    q_g0 = wa + ssm_w
    k_g0, v_g0 = q_g0 + hg * HEAD_DIM, q_g0 + (hg + kvg) * HEAD_DIM
    win = dict(heads=hw, kv=kvw, batch=batch, seq=seq, sink=p["win_sink"])
    glob = dict(heads=hg, kv=kvg, batch=batch, seq=seq, q_norm=p["q_norm"], k_norm=p["k_norm"])
    if st["latent"]:
        o_win = _attention(pa, 0, pa, k_w0, pa, v_w0, k_ctx=st["ck_w"], v_ctx=st["cv_w"],
                           rope=True, band=True, **win)
        o_glob = _attention(pa, q_g0, pa, k_g0, pa, v_g0, k_ctx=st["ck_g"], v_ctx=st["cv_g"], rope=True, **glob)
        k_g = None
    else:
        o_win = _attention(pa, 0, pa, k_w0, pa, v_w0, **win)
        o_glob, k_g = _attention(pa, q_g0, pa, k_g0, pa, v_g0, emit_k=True, **glob)
    y_ssm, fin = _s5_mix(pa, wa, ssm_w, *p["s5"], st["s0"], batch=batch, seq=seq)
    o_ssm = _glu(y_ssm, pa, wa, p["ssm_d"], p["ssm_w_glu"], layer)
    r = _merge(o_win, o_ssm, o_glob, gates, x, mod4, st["mod_row"], p["w_up_win"], p["w_up_ssm"],
               p["w_up_glob"], p["w_out"], layer, p["alpha"], st["tm"])
    extras = (pa[:, k_w0:v_w0], pa[:, v_w0:wa], k_g, pa[:, v_g0:v_g0 + kvg * HEAD_DIM], fin)
    return r, extras


def kernel(x_prompt, x_sample, cache_win_k, cache_win_v, cache_glob_k, cache_glob_v, state_ssm_re, state_ssm_im, c, c_ctx, w_mod, b_mod, w_in, win_sink, ssm_a_re, ssm_a_im, ssm_log_dt, ssm_b_re, ssm_b_im, ssm_c_re, ssm_c_im, ssm_d, ssm_w_glu, q_norm, k_norm, w_up_win, w_up_ssm, w_up_glob, w_out, ln1_g, ln1_b, ln2_g, ln2_b, router_w, exp_w_gate, exp_w_up, exp_w_down):
    bp, lp, d = x_prompt.shape
    bs, ls, _ = x_sample.shape
    depth = w_mod.shape[0]
    past = cache_win_k.shape[2]
    kv_win, kv_glob = cache_win_k.shape[3], cache_glob_k.shape[3]
    kvw, kvg = kv_win * HEAD_DIM, kv_glob * HEAD_DIM
    ssm_w = w_up_ssm.shape[1]
    n_exp = router_w.shape[-1]
    p_n = ssm_a_re.shape[-1]
    alpha = (2.0 * depth) ** 0.25
    dims = dict(heads_win=w_up_win.shape[1] // HEAD_DIM, kv_win=kv_win,
                heads_glob=w_up_glob.shape[1] // HEAD_DIM, kv_glob=kv_glob, ssm_w=ssm_w)
    assert lp % ROW_BLOCK == 0 and ls % ROW_BLOCK == 0 and 1 + bs <= 8

    cond = jnp.zeros((8, d), F32).at[0].set(c_ctx).at[1:1 + bs].set(c)
    sample_blocks = ls // ROW_BLOCK
    streams = [
        dict(batch=bp, seq=lp, latent=False, mod_row=lambda i: 0, tm=_tile(bp * lp, 512),
             s0=jnp.zeros((ssm_w // LANES, bp, 4 * S5_OCT * p_n), F32)),
        dict(batch=bs, seq=ls, latent=True, mod_row=lambda i: 1 + i // sample_blocks, tm=_tile(ls, 512)),
    ]
    xs = [x_prompt.reshape(bp * lp, d), x_sample.reshape(bs * ls, d)]
    new = {k: [] for k in ("wk", "wv", "gk", "gv", "sre", "sim")}

    lay = _s5_layouts(ssm_log_dt, ssm_a_re, ssm_a_im, ssm_b_re, ssm_b_im, ssm_c_re, ssm_c_im)
    p = dict(w_in=w_in, dims=dims, ssm_d=ssm_d.reshape(depth, 1, ssm_w),
             ssm_w_glu=ssm_w_glu.astype(BF16), w_up_win=w_up_win.astype(BF16),
             w_up_ssm=w_up_ssm.astype(BF16), w_up_glob=w_up_glob.astype(BF16))
    p.update(w_out=w_out.astype(BF16), alpha=alpha)
    router_pad = jnp.pad(router_w.astype(F32), ((0, 0), (0, 0), (0, LANES - n_exp)))
    router_hi = router_pad.astype(BF16)
    router_lo = (router_pad - router_hi.astype(F32)).astype(BF16)
    b_mod3 = b_mod.reshape(depth, 1, 6 * d)
    ln1_g3, ln1_b3 = ln1_g.reshape(depth, 1, d), ln1_b.reshape(depth, 1, d)
    ln2_g3, ln2_b3 = ln2_g.reshape(depth, 1, d), ln2_b.reshape(depth, 1, d)

    mods = [_matmul(cond, w_mod, l, 0, 6 * d, bias=b_mod3, act="silu", tn=1024).reshape(8, 6, 1, d)
            for l in range(depth)]
    hs = [_ln_mod(x, mods[0], st["mod_row"]) for x, st in zip(xs, streams)]

    for l in range(depth):
        p.update(win_sink=win_sink[l], q_norm=q_norm[l], k_norm=k_norm[l], s5=_s5_prep(lay, l))
        mod4 = mods[l]
        next_mod4 = mods[l + 1] if l + 1 < depth else None
        streams[1].update(
            ck_w=cache_win_k[:, l].reshape(bs, past, kvw), cv_w=cache_win_v[:, l].reshape(bs, past, kvw),
            ck_g=cache_glob_k[:, l].reshape(bs, past, kvg), cv_g=cache_glob_v[:, l].reshape(bs, past, kvg),
            s0=_state_to_lanes(state_ssm_re[:, l], state_ssm_im[:, l]))

        routed = []
        for si, st in enumerate(streams):
            x = xs[si]
            r, extras = _mixers(x, hs[si], mod4, p, st, l)
            x1, h2, logits = _post_ln(r, mod4, st["mod_row"], ln1_g3, ln1_b3, router_hi, router_lo, l, st["tm"])
            xsel, vals, rank = _route(logits, h2, batch=st["batch"], seq=st["seq"], n_exp=n_exp)
            routed.append((x1, xsel, vals, rank))
            if not st["latent"]:
                k_w, v_w, k_g, v_g, fin = extras
                new["wk"].append(k_w.reshape(bp, lp, -1))
                new["wv"].append(v_w.reshape(bp, lp, -1))
                new["gk"].append(k_g.reshape(bp, lp, -1))
                new["gv"].append(v_g.reshape(bp, lp, -1))
                s_re, s_im = _lanes_to_state(fin, p_n)
                new["sre"].append(s_re)
                new["sim"].append(s_im)
        outs = _ffn(routed[0][1], routed[1][1], routed[0][2], routed[1][2],
                    exp_w_gate, exp_w_up, exp_w_down, l)
        for si, st in enumerate(streams):
            xs[si], hs[si] = _scatter(outs[si], routed[si][3], routed[si][0], mod4, st["mod_row"],
                                      ln2_g3, ln2_b3, l, alpha, next_mod4, batch=st["batch"], seq=st["seq"])

    def cache(layers):
        return jnp.stack(layers, axis=1).reshape(bp, depth, lp, -1, HEAD_DIM)

    return (xs[0].reshape(bp, lp, d), xs[1].reshape(bs, ls, d),
            cache(new["wk"]), cache(new["wv"]), cache(new["gk"]), cache(new["gv"]),
            jnp.stack(new["sre"], axis=1), jnp.stack(new["sim"], axis=1))
```
